```python
import jax, jax.numpy as jnp
from jax import lax
import numpy as np

D_MODEL = 1024
BATCH = 8
SEQ = 2048
DEPTH = 4

N_MIXERS = 3
HEAD_DIM = 64
RMS_EPS = 1e-6

NSA_HEADS = 16
NSA_GROUPS = 4
NSA_REP = NSA_HEADS // NSA_GROUPS
NSA_KV = NSA_GROUPS * HEAD_DIM
CMP_BLOCK = 32
CMP_STRIDE = 16
CMP_HIDDEN = 256
SLC_BLOCK = 64
SLC_TOPK = 16
WINDOW = 512
WIN_QBLOCK = 128
NSA_Q_CHUNK = 64
NSA_IN = NSA_HEADS * HEAD_DIM + 6 * NSA_KV + 3 * NSA_HEADS
NSA_SPLITS = [NSA_HEADS * HEAD_DIM + i * NSA_KV for i in range(7)]

MOBA_HEADS = 16
MOBA_BLOCK = 256
MOBA_TOPK = 3
MOBA_Q_CHUNK = 32

CONV_WIDTH = 3

D_FF = -(-8 * D_MODEL // (3 * 256)) * 256

kernel_name = "hybrid_nsa_moba_shortconv_trunk"


def rms_norm(x, g):
    xf = x.astype(jnp.float32)
    y = xf * lax.rsqrt(jnp.mean(xf * xf, axis=-1, keepdims=True) + RMS_EPS)
    return (y * g.astype(jnp.float32)).astype(x.dtype)


def masked_softmax(s, mask):
    s = jnp.where(mask, s.astype(jnp.float32), -jnp.inf)
    m = jnp.max(s, axis=-1, keepdims=True)
    m = jnp.where(jnp.isfinite(m), m, 0.0)
    e = jnp.where(mask, jnp.exp(s - m), 0.0)
    d = jnp.sum(e, axis=-1, keepdims=True)
    return e / jnp.where(d > 0, d, 1.0)


def gather_blocks(blocks, idx):
    return jax.vmap(jax.vmap(lambda b, i: b[i]))(blocks, idx)


def swiglu(a, w_gate, w_up, w_down):
    return (jax.nn.silu(a @ w_gate) * (a @ w_up)) @ w_down


def nsa_mixer(h, w_in, w_out, pos_k, pos_v, k_w1, k_w2, v_w1, v_w2):
    B_, S_, _ = h.shape
    G, R, dh = NSA_GROUPS, NSA_REP, HEAD_DIM
    q, k_c, v_c, k_s, v_s, k_w, v_w, gates = jnp.split(h @ w_in, NSA_SPLITS, axis=-1)
    q = q.reshape(B_, S_, G, R, dh) * (HEAD_DIM ** -0.5)
    kvs = [a.reshape(B_, S_, G, dh) for a in (k_c, v_c, k_s, v_s, k_w, v_w)]
    k_c, v_c, k_s, v_s, k_w, v_w = kvs
    gates = jax.nn.sigmoid(gates.astype(jnp.float32)).reshape(B_, S_, 3, G, R)
    t = jnp.arange(S_)

    n_cmp = (S_ - CMP_BLOCK) // CMP_STRIDE + 1
    cmp_idx = jnp.arange(n_cmp)[:, None] * CMP_STRIDE + jnp.arange(CMP_BLOCK)[None, :]

    def compress(kv, pos, w1, w2):
        blk = kv[:, cmp_idx] + pos[None, None, :, None, :]
        blk = jnp.moveaxis(blk, 3, 2).reshape(B_, n_cmp, G, CMP_BLOCK * dh)
        return jax.nn.gelu(blk @ w1) @ w2

    kc = compress(k_c, pos_k, k_w1, k_w2)
    vc = compress(v_c, pos_v, v_w1, v_w2)
    cmp_end = jnp.arange(n_cmp) * CMP_STRIDE + CMP_BLOCK - 1
    cmp_mask = cmp_end[None, :] <= t[:, None]
    s_cmp = jnp.einsum('bsgrd,bcgd->bgrsc', q, kc)
    p_cmp = masked_softmax(s_cmp, cmp_mask)
    o_cmp = jnp.einsum('bgrsc,bcgd->bsgrd', p_cmp.astype(vc.dtype), vc)

    n_slc = S_ // SLC_BLOCK
    c_start = jnp.arange(n_cmp) * CMP_STRIDE
    j_start = jnp.arange(n_slc) * SLC_BLOCK
    overlap = ((c_start[:, None] < j_start[None, :] + SLC_BLOCK)
               & (c_start[:, None] + CMP_BLOCK > j_start[None, :])).astype(jnp.float32)
    imp = jnp.einsum('bgrsc,cn->bgsn', p_cmp, overlap)
    cur = t // SLC_BLOCK
    blk_ids = jnp.arange(n_slc)
    forced = ((blk_ids[None, :] == 0) | (blk_ids[None, :] == cur[:, None])
              | (blk_ids[None, :] == cur[:, None] - 1))
    imp = jnp.where(forced, jnp.inf,
                    jnp.where(blk_ids[None, :] <= cur[:, None], imp, -jnp.inf))
    n_sel = min(SLC_TOPK, n_slc)
    _, sel = lax.top_k(imp, n_sel)

    kb = k_s.reshape(B_, n_slc, SLC_BLOCK, G, dh).transpose(0, 3, 1, 2, 4)
    vb = v_s.reshape(B_, n_slc, SLC_BLOCK, G, dh).transpose(0, 3, 1, 2, 4)
    Qc = NSA_Q_CHUNK
    n_qc = S_ // Qc
    q_chunks = q.reshape(B_, n_qc, Qc, G, R, dh).transpose(1, 0, 2, 3, 4, 5)
    sel_chunks = sel.reshape(B_, G, n_qc, Qc, n_sel).transpose(2, 0, 1, 3, 4)

    def slc_chunk(args):
        ci, qc, ix = args
        tq = ci * Qc + jnp.arange(Qc)
        kg = gather_blocks(kb, ix)
        vg = gather_blocks(vb, ix)
        kpos = ix[..., None] * SLC_BLOCK + jnp.arange(SLC_BLOCK)
        mask = (kpos <= tq[:, None, None]).reshape(B_, G, Qc, n_sel * SLC_BLOCK)[:, :, None]
        s = jnp.einsum('bqgrd,bgqnkd->bgrqnk', qc, kg).reshape(B_, G, R, Qc, n_sel * SLC_BLOCK)
        p = masked_softmax(s, mask).reshape(B_, G, R, Qc, n_sel, SLC_BLOCK)
        return jnp.einsum('bgrqnk,bgqnkd->bqgrd', p.astype(vg.dtype), vg)

    o_slc = lax.map(slc_chunk, (jnp.arange(n_qc), q_chunks, sel_chunks))
    o_slc = o_slc.transpose(1, 0, 2, 3, 4, 5).reshape(B_, S_, G, R, dh)

    QB = WIN_QBLOCK
    nqb = S_ // QB
    n_back = WINDOW // QB
    pad = n_back * QB

    def band(kv):
        kp = jnp.pad(kv, ((0, 0), (pad, 0), (0, 0), (0, 0))).reshape(B_, nqb + n_back, QB, G, dh)
        return jnp.concatenate([kp[:, i:i + nqb] for i in range(n_back + 1)], axis=2)

    kwb = band(k_w)
    vwb = band(v_w)
    qb = q.reshape(B_, nqb, QB, G, R, dh)
    qpos = t.reshape(nqb, QB)
    kpos = (jnp.arange(nqb)[:, None] - n_back) * QB + jnp.arange((n_back + 1) * QB)[None, :]
    wmask = ((kpos[:, None, :] <= qpos[:, :, None])
             & (kpos[:, None, :] > qpos[:, :, None] - WINDOW)
             & (kpos[:, None, :] >= 0))
    s_win = jnp.einsum('bnqgrd,bnkgd->bgrnqk', qb, kwb)
    p_win = masked_softmax(s_win, wmask)
    o_win = jnp.einsum('bgrnqk,bnkgd->bnqgrd', p_win.astype(vwb.dtype), vwb).reshape(B_, S_, G, R, dh)

    o = (gates[:, :, 0, :, :, None] * o_cmp + gates[:, :, 1, :, :, None] * o_slc
         + gates[:, :, 2, :, :, None] * o_win).astype(h.dtype)
    return o.reshape(B_, S_, NSA_HEADS * dh) @ w_out


def moba_mixer(h, w_in, w_out):
    B_, S_, _ = h.shape
    H, dh = MOBA_HEADS, HEAD_DIM
    q, k, v = jnp.split(h @ w_in, 3, axis=-1)
    n_blk = -(-S_ // MOBA_BLOCK)
    S_pad = n_blk * MOBA_BLOCK

    def heads(a):
        a = jnp.pad(a, ((0, 0), (0, S_pad - S_), (0, 0)))
        return a.reshape(B_, S_pad, H, dh).transpose(0, 2, 1, 3)

    q = heads(q) * (HEAD_DIM ** -0.5)
    k = heads(k)
    v = heads(v)
    kb = k.reshape(B_, H, n_blk, MOBA_BLOCK, dh)
    vb = v.reshape(B_, H, n_blk, MOBA_BLOCK, dh)
    k_mean = jnp.mean(kb.astype(jnp.float32), axis=3)
    t = jnp.arange(S_pad)
    cur = t // MOBA_BLOCK
    n_sel = min(MOBA_TOPK, n_blk - 1)
    if n_sel > 0:
        score = jnp.einsum('bhsd,bhnd->bhsn', q.astype(jnp.float32), k_mean)
        past = jnp.arange(n_blk)[None, :] < cur[:, None]
        _, sel = lax.top_k(jnp.where(past, score, -jnp.inf), n_sel)
    else:
        sel = jnp.zeros((B_, H, S_pad, 0), jnp.int32)
    Qc = MOBA_Q_CHUNK
    n_qc = S_pad // Qc
    q_chunks = q.reshape(B_, H, n_qc, Qc, dh).transpose(2, 0, 1, 3, 4)
    sel_chunks = sel.reshape(B_, H, n_qc, Qc, n_sel).transpose(2, 0, 1, 3, 4)

    def chunk(args):
        ci, qc, ix = args
        tq = ci * Qc + jnp.arange(Qc)
        own = (ci * Qc) // MOBA_BLOCK
        k_own = lax.dynamic_index_in_dim(kb, own, axis=2, keepdims=False)
        v_own = lax.dynamic_index_in_dim(vb, own, axis=2, keepdims=False)
        own_pos = own * MOBA_BLOCK + jnp.arange(MOBA_BLOCK)
        m_own = jnp.broadcast_to(own_pos[None, :] <= tq[:, None], (B_, H, Qc, MOBA_BLOCK))
        s_own = jnp.einsum('bhqd,bhkd->bhqk', qc, k_own)
        if n_sel == 0:
            p = masked_softmax(s_own, m_own)
            return jnp.einsum('bhqk,bhkd->bhqd', p.astype(v_own.dtype), v_own)
        kg = gather_blocks(kb, ix)
        vg = gather_blocks(vb, ix)
        n_keys = n_sel * MOBA_BLOCK
        s_sel = jnp.einsum('bhqd,bhqnkd->bhqnk', qc, kg).reshape(B_, H, Qc, n_keys)
        valid = ix < own
        m_sel = jnp.broadcast_to(valid[..., None], (B_, H, Qc, n_sel, MOBA_BLOCK)).reshape(B_, H, Qc, n_keys)
        p = masked_softmax(jnp.concatenate([s_sel, s_own], axis=-1),
                           jnp.concatenate([m_sel, m_own], axis=-1))
        p_sel = p[..., :n_keys].reshape(B_, H, Qc, n_sel, MOBA_BLOCK).astype(vg.dtype)
        p_own = p[..., n_keys:].astype(v_own.dtype)
        return (jnp.einsum('bhqnk,bhqnkd->bhqd', p_sel, vg)
                + jnp.einsum('bhqk,bhkd->bhqd', p_own, v_own))

    o = lax.map(chunk, (jnp.arange(n_qc), q_chunks, sel_chunks))
    o = o.transpose(1, 2, 0, 3, 4).reshape(B_, H, S_pad, dh)[:, :, :S_]
    o = o.transpose(0, 2, 1, 3).reshape(B_, S_, H * dh)
    return o @ w_out


def short_conv_mixer(h, w_in, conv_w, w_out):
    d = h.shape[-1]
    b_gate, c_gate, u = jnp.split(h @ w_in, 3, axis=-1)
    z = c_gate * u
    y = lax.conv_general_dilated(
        z, conv_w[:, None, :].astype(z.dtype), window_strides=(1,),
        padding=[(CONV_WIDTH - 1, 0)],
        dimension_numbers=('NWC', 'WIO', 'NWC'), feature_group_count=d)
    return (b_gate * y) @ w_out


def setup_inputs(seed: int = 0) -> dict:
    key = jax.random.key(seed)
    ks = jax.random.split(key, 20)
    f32 = jnp.float32
    n_a = (DEPTH + 2) // 3
    n_b = (DEPTH + 1) // 3
    n_c = DEPTH // 3

    def w(k, shape, fan_in):
        return jax.random.normal(k, shape, f32) * (fan_in ** -0.5)

    def gain(k, shape):
        return 1.0 + 0.02 * jax.random.normal(k, shape, f32)

    return {
        "x": jax.random.normal(ks[0], (BATCH, SEQ, D_MODEL), f32),
        "norm_mix": gain(ks[1], (DEPTH, D_MODEL)),
        "norm_ffn": gain(ks[2], (DEPTH, D_MODEL)),
        "norm_final": gain(ks[3], (D_MODEL,)),
        "ffn_w_gate": w(ks[4], (DEPTH, D_MODEL, D_FF), D_MODEL),
        "ffn_w_up": w(ks[5], (DEPTH, D_MODEL, D_FF), D_MODEL),
        "ffn_w_down": w(ks[6], (DEPTH, D_FF, D_MODEL), D_FF),
        "nsa_w_in": w(ks[7], (n_a, D_MODEL, NSA_IN), D_MODEL),
        "nsa_w_out": w(ks[8], (n_a, NSA_HEADS * HEAD_DIM, D_MODEL), NSA_HEADS * HEAD_DIM),
        "nsa_cmp_pos_k": 0.1 * jax.random.normal(ks[9], (n_a, CMP_BLOCK, HEAD_DIM), f32),
        "nsa_cmp_pos_v": 0.1 * jax.random.normal(ks[10], (n_a, CMP_BLOCK, HEAD_DIM), f32),
        "nsa_cmp_k_w1": w(ks[11], (n_a, CMP_BLOCK * HEAD_DIM, CMP_HIDDEN), CMP_BLOCK * HEAD_DIM),
        "nsa_cmp_k_w2": w(ks[12], (n_a, CMP_HIDDEN, HEAD_DIM), CMP_HIDDEN),
        "nsa_cmp_v_w1": w(ks[13], (n_a, CMP_BLOCK * HEAD_DIM, CMP_HIDDEN), CMP_BLOCK * HEAD_DIM),
        "nsa_cmp_v_w2": w(ks[14], (n_a, CMP_HIDDEN, HEAD_DIM), CMP_HIDDEN),
        "moba_w_in": w(ks[15], (n_b, D_MODEL, 3 * MOBA_HEADS * HEAD_DIM), D_MODEL),
        "moba_w_out": w(ks[16], (n_b, MOBA_HEADS * HEAD_DIM, D_MODEL), MOBA_HEADS * HEAD_DIM),
        "conv_w_in": w(ks[17], (n_c, D_MODEL, 3 * D_MODEL), D_MODEL),
        "conv_w": w(ks[18], (n_c, CONV_WIDTH, D_MODEL), CONV_WIDTH),
        "conv_w_out": w(ks[19], (n_c, D_MODEL, D_MODEL), D_MODEL),
    }


def reference(x, norm_mix, norm_ffn, norm_final, ffn_w_gate, ffn_w_up, ffn_w_down,
              nsa_w_in, nsa_w_out, nsa_cmp_pos_k, nsa_cmp_pos_v, nsa_cmp_k_w1,
              nsa_cmp_k_w2, nsa_cmp_v_w1, nsa_cmp_v_w2, moba_w_in, moba_w_out,
              conv_w_in, conv_w, conv_w_out):
    h = x
    for i in range(DEPTH):
        kind, j = i % N_MIXERS, i // N_MIXERS
        a = rms_norm(h, norm_mix[i])
        if kind == 0:
            m = nsa_mixer(a, nsa_w_in[j], nsa_w_out[j], nsa_cmp_pos_k[j], nsa_cmp_pos_v[j],
                          nsa_cmp_k_w1[j], nsa_cmp_k_w2[j], nsa_cmp_v_w1[j], nsa_cmp_v_w2[j])
        elif kind == 1:
            m = moba_mixer(a, moba_w_in[j], moba_w_out[j])
        else:
            m = short_conv_mixer(a, conv_w_in[j], conv_w[j], conv_w_out[j])
        h = h + m
        a = rms_norm(h, norm_ffn[i])
        h = h + swiglu(a, ffn_w_gate[i], ffn_w_up[i], ffn_w_down[i])
    return rms_norm(h, norm_final)
```

```python
import functools

import jax
import jax.numpy as jnp
from jax import lax
from jax.experimental import pallas as pl
from jax.experimental.pallas import tpu as pltpu

F32 = jnp.float32
BF16 = jnp.bfloat16

D_MODEL = 1024
HEAD_DIM = 64
RMS_EPS = 1e-6
N_MIXERS = 3

NSA_HEADS = 16
NSA_GROUPS = 4
NSA_REP = NSA_HEADS // NSA_GROUPS
NSA_KV = NSA_GROUPS * HEAD_DIM
CMP_BLOCK = 32
CMP_STRIDE = 16
CMP_HIDDEN = 256
SLC_BLOCK = 64
SLC_TOPK = 16
WINDOW = 512

MOBA_HEADS = 16
MOBA_BLOCK = 256
MOBA_TOPK = 3

CONV_WIDTH = 3

LANES = 128
VMEM_LIMIT = 56 * 1024 * 1024
MASKED = -1e30
M_FLOOR = -1e29

ATT_TQ = 256
ATT_TK = 256


def _dot(a, b):
    return jnp.dot(a, b, preferred_element_type=F32)


def _dot_tn(a, b):
    return lax.dot_general(a, b, (((0,), (0,)), ((), ())), preferred_element_type=F32)


def _rms(x, g):
    y = x * lax.rsqrt(jnp.mean(x * x, axis=-1, keepdims=True) + RMS_EPS)
    return y * g


def _cparams(sem):
    return pltpu.CompilerParams(dimension_semantics=sem, vmem_limit_bytes=VMEM_LIMIT)


def _proj_kernel(*refs, n_out, chunk):
    x_ref, g_ref = refs[0], refs[1]
    w_refs = refs[2:2 + n_out]
    o_refs = refs[2 + n_out:2 + 2 * n_out]
    a = _rms(x_ref[...], g_ref[...]).astype(BF16)
    for w_ref, o_ref in zip(w_refs, o_refs):
        n = w_ref.shape[1]
        for n0 in range(0, n, chunk):
            n1 = min(n0 + chunk, n)
            o_ref[:, n0:n1] = _dot(a, w_ref[:, n0:n1]).astype(o_ref.dtype)


def _proj(h, gain, weights, out_dtypes, tm=512):
    t = h.shape[0]
    n_out = len(weights)
    in_specs = [pl.BlockSpec((tm, D_MODEL), lambda i: (i, 0)),
                pl.BlockSpec((1, D_MODEL), lambda i: (0, 0))]
    in_specs += [pl.BlockSpec(w.shape, lambda i: (0, 0)) for w in weights]
    out_specs = [pl.BlockSpec((tm, w.shape[1]), lambda i: (i, 0)) for w in weights]
    out_shape = [jax.ShapeDtypeStruct((t, w.shape[1]), dt) for w, dt in zip(weights, out_dtypes)]
    return pl.pallas_call(
        functools.partial(_proj_kernel, n_out=n_out, chunk=512),
        grid=(t // tm,),
        in_specs=in_specs, out_specs=out_specs, out_shape=out_shape,
        compiler_params=_cparams(("parallel",)),
        name="proj",
    )(h, gain.reshape(1, D_MODEL), *weights)


def _ffn_kernel(h_ref, o_ref, wo_ref, g_ref, wg_ref, wu_ref, wd_ref, gf_ref, out_ref,
                h1_sc, a_sc, acc_sc, *, final_norm):
    j = pl.program_id(1)

    @pl.when(j == 0)
    def _():
        h1 = h_ref[...] + _dot(o_ref[...], wo_ref[...])
        h1_sc[...] = h1
        a_sc[...] = _rms(h1, g_ref[...]).astype(BF16)
        acc_sc[...] = jnp.zeros_like(acc_sc)

    a = a_sc[...]
    gate = _dot(a, wg_ref[...])
    up = _dot(a, wu_ref[...])
    t = (jax.nn.silu(gate) * up).astype(BF16)
    acc_sc[...] += _dot(t, wd_ref[...])

    @pl.when(j == pl.num_programs(1) - 1)
    def _():
        y = h1_sc[...] + acc_sc[...]
        if final_norm:
            y = _rms(y, gf_ref[...])
        out_ref[...] = y


def _ffn(h, o, wo, gain, wg, wu, wd, final_gain, final_norm, tm=512, n_ff=2):
    t = h.shape[0]
    d_ff = wg.shape[1]
    tf = d_ff // n_ff
    assert tf * n_ff == d_ff and tf % LANES == 0
    return pl.pallas_call(
        functools.partial(_ffn_kernel, final_norm=final_norm),
        grid=(t // tm, n_ff),
        in_specs=[
            pl.BlockSpec((tm, D_MODEL), lambda i, j: (i, 0)),
            pl.BlockSpec((tm, o.shape[1]), lambda i, j: (i, 0)),
            pl.BlockSpec(wo.shape, lambda i, j: (0, 0)),
            pl.BlockSpec((1, D_MODEL), lambda i, j: (0, 0)),
            pl.BlockSpec((D_MODEL, tf), lambda i, j: (0, j)),
            pl.BlockSpec((D_MODEL, tf), lambda i, j: (0, j)),
            pl.BlockSpec((tf, D_MODEL), lambda i, j: (j, 0)),
            pl.BlockSpec((1, D_MODEL), lambda i, j: (0, 0)),
        ],
        out_specs=pl.BlockSpec((tm, D_MODEL), lambda i, j: (i, 0)),
        out_shape=jax.ShapeDtypeStruct((t, D_MODEL), F32),
        scratch_shapes=[pltpu.VMEM((tm, D_MODEL), F32),
                        pltpu.VMEM((tm, D_MODEL), BF16),
                        pltpu.VMEM((tm, D_MODEL), F32)],
        compiler_params=_cparams(("parallel", "arbitrary")),
        name="ffn",
    )(h, o, wo, gain.reshape(1, D_MODEL), wg, wu, wd, final_gain.reshape(1, D_MODEL))


def _flash_init(m_sc, l_sc, acc_sc):
    m_sc[...] = jnp.full(m_sc.shape, M_FLOOR, F32)
    l_sc[...] = jnp.zeros_like(l_sc)
    acc_sc[...] = jnp.zeros_like(acc_sc)


def _flash_step(kv, qp, bias, m_sc, l_sc, acc_sc):
    s = _dot(kv, qp)
    if bias is not None:
        w = s.shape[1] // len(bias)
        s = jnp.concatenate([s[:, r * w:(r + 1) * w] + b for r, b in enumerate(bias)], axis=1)
    m_old = m_sc[...]
    m_new = jnp.maximum(m_old, jnp.max(s, axis=0, keepdims=True))
    alpha = jnp.exp(m_old - m_new)
    p = jnp.exp(s - m_new)
    l_sc[...] = alpha * l_sc[...] + jnp.sum(p, axis=0, keepdims=True)
    acc_sc[...] = alpha * acc_sc[...] + _dot_tn(kv, p.astype(BF16))
    m_sc[...] = m_new


def _flash_out(l_sc, acc_sc):
    l = l_sc[...]
    return acc_sc[HEAD_DIM:2 * HEAD_DIM, :] / jnp.where(l > 0, l, 1.0)


def _iota(shape, dim):
    return lax.broadcasted_iota(jnp.int32, shape, dim)


def _causal_bias(tk, tq, base=None):
    ok = _iota((tk, tq), 0) <= _iota((tk, tq), 1)
    return jnp.where(ok, 0.0 if base is None else base, MASKED)


def _rank_select(adj, n_rows, topk):
    j = _iota(adj.shape, 0)
    rank = jnp.zeros(adj.shape, jnp.int32)
    for k in range(n_rows):
        rk = adj[k:k + 1, :]
        beats = (rk > adj) | ((rk == adj) & (j > k))
        rank = rank + beats.astype(jnp.int32)
    return rank < topk


def _nsa_cmp_kernel(xk_ref, xv_ref, w1k_ref, w2k_ref, pk_ref, w1v_ref, w2v_ref, pv_ref, o_ref):
    half = CMP_STRIDE * HEAD_DIM
    for x_ref, w1_ref, w2_ref, p_ref, lane0 in ((xk_ref, w1k_ref, w2k_ref, pk_ref, 0),
                                                (xv_ref, w1v_ref, w2v_ref, pv_ref, HEAD_DIM)):
        posb = _dot(p_ref[...], w1_ref[...])[0:1]
        for g in range(NSA_GROUPS):
            x = x_ref[0, g]
            lo = _dot(x, w1_ref[0:half, :])
            hi = _dot(x, w1_ref[half:2 * half, :])
            n = hi.shape[0]
            hid = lo + pltpu.roll(hi, n - 1, 0) + posb
            y = _dot(jax.nn.gelu(hid).astype(BF16), w2_ref[...])
            o_ref[0, g, :, lane0:lane0 + HEAD_DIM] = y.astype(o_ref.dtype)


def _nsa_cmp(xk, xv, w1k, w2k, pk, w1v, w2v, pv):
    b, g, n, w = xk.shape
    full = lambda a: pl.BlockSpec(a.shape, lambda i: (0,) * a.ndim)
    xspec = pl.BlockSpec((1, g, n, w), lambda i: (i, 0, 0, 0))
    return pl.pallas_call(
        _nsa_cmp_kernel,
        grid=(b,),
        in_specs=[xspec, xspec, full(w1k), full(w2k), full(pk), full(w1v), full(w2v), full(pv)],
        out_specs=pl.BlockSpec((1, g, n, 2 * HEAD_DIM), lambda i: (i, 0, 0, 0)),
        out_shape=jax.ShapeDtypeStruct((b, g, n, 2 * HEAD_DIM), BF16),
        compiler_params=_cparams(("parallel",)),
        name="nsa_cmp",
    )(xk, xv, w1k, w2k, pk, w1v, w2v, pv)


def _nsa_attn_kernel(q_ref, kvs_ref, kvw_ref, kvc_ref, gt_ref, o_ref,
                     m_sc, l_sc, acc_sc, bias_sc, out_sc, *, seq):
    tq, tk = ATT_TQ, ATT_TK
    i = pl.program_id(2)
    q0 = i * tq
    n_slc = seq // SLC_BLOCK
    n_cmp_pad = seq // CMP_STRIDE
    rep = NSA_REP

    qt = q_ref[...].astype(F32).T
    zpad = jnp.zeros((HEAD_DIM, tq), F32)
    qp = jnp.concatenate(
        [jnp.concatenate([qt[r * HEAD_DIM:(r + 1) * HEAD_DIM], zpad], axis=0) for r in range(rep)],
        axis=1).astype(BF16)
    gates = jax.nn.sigmoid(gt_ref[...]).T

    def emit(branch, o_t, first):
        for r in range(rep):
            row = branch * rep + r
            val = gates[row:row + 1, :] * o_t[:, r * tq:(r + 1) * tq]
            sl = slice(r * HEAD_DIM, (r + 1) * HEAD_DIM)
            if first:
                out_sc[sl, :] = val
            else:
                out_sc[sl, :] += val

    kvc = kvc_ref[0, 0]
    c_end = _iota((n_cmp_pad, tq), 0) * CMP_STRIDE + (CMP_BLOCK - 1)
    t_pos = q0 + _iota((n_cmp_pad, tq), 1)
    c_ok = (c_end <= t_pos) & (c_end < seq)
    c_bias = jnp.where(c_ok, 0.0, MASKED)
    s = _dot(kvc, qp)
    s = jnp.concatenate([s[:, r * tq:(r + 1) * tq] + c_bias for r in range(rep)], axis=1)
    m = jnp.maximum(jnp.max(s, axis=0, keepdims=True), M_FLOOR)
    e = jnp.exp(s - m)
    d = jnp.sum(e, axis=0, keepdims=True)
    p = e / jnp.where(d > 0, d, 1.0)
    emit(0, _dot_tn(kvc, p.astype(BF16))[HEAD_DIM:2 * HEAD_DIM, :], True)

    p_sum = p[:, 0:tq]
    for r in range(1, rep):
        p_sum = p_sum + p[:, r * tq:(r + 1) * tq]
    jj = _iota((n_slc, n_cmp_pad), 0) * SLC_BLOCK
    cc = _iota((n_slc, n_cmp_pad), 1) * CMP_STRIDE
    overlap_t = jnp.where((cc < jj + SLC_BLOCK) & (cc + CMP_BLOCK > jj) & (cc + CMP_BLOCK <= seq),
                          1.0, 0.0).astype(BF16)
    p_hi = p_sum.astype(BF16)
    p_lo = (p_sum - p_hi.astype(F32)).astype(BF16)
    imp = _dot(overlap_t, p_hi) + _dot(overlap_t, p_lo)
    j_blk = _iota((n_slc, tq), 0)
    cur = (q0 + _iota((n_slc, tq), 1)) // SLC_BLOCK
    forced = (j_blk == 0) | (j_blk == cur) | (j_blk == cur - 1)
    valid = j_blk <= cur
    adj = jnp.where(forced, jnp.inf, jnp.where(valid, imp, -jnp.inf))
    sel = valid & _rank_select(adj, n_slc, min(SLC_TOPK, n_slc))
    sel_t = jnp.where(sel, 1.0, 0.0).astype(BF16)
    expand = jnp.where(_iota((seq, n_slc), 0) // SLC_BLOCK == _iota((seq, n_slc), 1),
                       1.0, 0.0).astype(BF16)
    bias_sc[...] = (_dot(expand, sel_t) - 1.0) * (-MASKED)

    _flash_init(m_sc, l_sc, acc_sc)

    def slc_body(kt, carry):
        k0 = pl.multiple_of(kt * tk, tk)
        b = bias_sc[pl.ds(k0, tk), :]
        _flash_step(kvs_ref[pl.ds(k0, tk), :], qp, [b] * rep, m_sc, l_sc, acc_sc)
        return carry

    lax.fori_loop(0, i, slc_body, 0)
    k0 = pl.multiple_of(q0, tk)
    b = _causal_bias(tk, tq, bias_sc[pl.ds(k0, tk), :])
    _flash_step(kvs_ref[pl.ds(k0, tk), :], qp, [b] * rep, m_sc, l_sc, acc_sc)
    emit(1, _flash_out(l_sc, acc_sc), False)

    _flash_init(m_sc, l_sc, acc_sc)

    @pl.when(i >= 2)
    def _():
        k0 = pl.multiple_of(q0 - 2 * tk, tk)
        ok = _iota((tk, tq), 0) > _iota((tk, tq), 1)
        _flash_step(kvw_ref[pl.ds(k0, tk), :], qp, [jnp.where(ok, 0.0, MASKED)] * rep,
                    m_sc, l_sc, acc_sc)

    @pl.when(i >= 1)
    def _():
        k0 = pl.multiple_of(q0 - tk, tk)
        _flash_step(kvw_ref[pl.ds(k0, tk), :], qp, None, m_sc, l_sc, acc_sc)

    k0 = pl.multiple_of(q0, tk)
    _flash_step(kvw_ref[pl.ds(k0, tk), :], qp, [_causal_bias(tk, tq)] * rep, m_sc, l_sc, acc_sc)
    emit(2, _flash_out(l_sc, acc_sc), False)

    o_ref[...] = out_sc[...].T.astype(o_ref.dtype)


def _nsa_attn(main, kvc, gates, batch, seq):
    tq = ATT_TQ
    assert ATT_TK == tq and WINDOW == 2 * tq and seq % tq == 0
    nq = seq // tq
    t = batch * seq
    gw = NSA_REP * HEAD_DIM
    q_blocks = NSA_HEADS * HEAD_DIM // LANES
    m = NSA_REP * tq
    return pl.pallas_call(
        functools.partial(_nsa_attn_kernel, seq=seq),
        grid=(batch, NSA_GROUPS, nq),
        in_specs=[
            pl.BlockSpec((tq, gw), lambda b, g, i: (b * nq + i, g)),
            pl.BlockSpec((seq, LANES), lambda b, g, i: (b, q_blocks + g)),
            pl.BlockSpec((seq, LANES), lambda b, g, i: (b, q_blocks + NSA_GROUPS + g)),
            pl.BlockSpec((1, 1, seq // CMP_STRIDE, LANES), lambda b, g, i: (b, g, 0, 0)),
            pl.BlockSpec((tq, LANES), lambda b, g, i: (b * nq + i, g)),
        ],
        out_specs=pl.BlockSpec((tq, gw), lambda b, g, i: (b * nq + i, g)),
        out_shape=jax.ShapeDtypeStruct((t, NSA_HEADS * HEAD_DIM), BF16),
        scratch_shapes=[pltpu.VMEM((1, m), F32), pltpu.VMEM((1, m), F32),
                        pltpu.VMEM((LANES, m), F32),
                        pltpu.VMEM((seq, tq), F32),
                        pltpu.VMEM((gw, tq), F32)],
        compiler_params=_cparams(("parallel", "parallel", "arbitrary")),
        name="nsa_attn",
    )(main, main, main, kvc, gates)


def _interleave_heads(wa, wb, n):
    d = wa.shape[0]
    return jnp.stack([wa.reshape(d, n, HEAD_DIM), wb.reshape(d, n, HEAD_DIM)],
                     axis=2).reshape(d, 2 * n * HEAD_DIM)


def _nsa_mixer(h, gain, w_in, pos_k, pos_v, k_w1, k_w2, v_w1, v_w2, batch, seq):
    nh = NSA_HEADS * HEAD_DIM
    c = [nh + i * NSA_KV for i in range(7)]
    wq = w_in[:, :c[0]] * (HEAD_DIM ** -0.5)
    wkc, wvc, wks, wvs, wkw, wvw = (w_in[:, c[i]:c[i + 1]] for i in range(6))
    w_gate = w_in[:, c[6]:]
    w_main = jnp.concatenate(
        [wq, _interleave_heads(wks, wvs, NSA_GROUPS), _interleave_heads(wkw, wvw, NSA_GROUPS),
         wkc, wvc], axis=1).astype(BF16)
    wg = w_gate.reshape(D_MODEL, 3, NSA_GROUPS, NSA_REP).transpose(0, 2, 1, 3)
    wg = wg.reshape(D_MODEL, NSA_GROUPS, 3 * NSA_REP)
    wg = jnp.pad(wg, ((0, 0), (0, 0), (0, LANES - 3 * NSA_REP)))
    wg = wg.reshape(D_MODEL, NSA_GROUPS * LANES).astype(BF16)

    main, gates = _proj(h, gain, [w_main, wg], [BF16, F32])

    n_seg = seq // CMP_STRIDE
    base = nh + 4 * NSA_KV

    def segs(cols):
        x = cols.reshape(batch, n_seg, CMP_STRIDE, NSA_GROUPS, HEAD_DIM)
        return x.transpose(0, 3, 1, 2, 4).reshape(batch, NSA_GROUPS, n_seg, CMP_STRIDE * HEAD_DIM)

    xk = segs(main[:, base:base + NSA_KV])
    xv = segs(main[:, base + NSA_KV:base + 2 * NSA_KV])

    def pos8(p):
        return jnp.pad(p.reshape(1, CMP_BLOCK * HEAD_DIM), ((0, 7), (0, 0))).astype(BF16)

    kvc = _nsa_cmp(xk, xv, k_w1.astype(BF16), k_w2.astype(BF16), pos8(pos_k),
                   v_w1.astype(BF16), v_w2.astype(BF16), pos8(pos_v))
    return _nsa_attn(main, kvc, gates, batch, seq)


def _moba_attn_kernel(q_ref, kv_ref, o_ref, m_sc, l_sc, acc_sc, selb_sc, *, seq):
    tq, tk = ATT_TQ, ATT_TK
    i = pl.program_id(2)
    q0 = i * tq
    n_blk = seq // MOBA_BLOCK

    qt = q_ref[...].astype(F32).T
    zpad = jnp.zeros((HEAD_DIM, tq), F32)
    outs = []
    for e in range(2):
        lanes = slice(e * LANES, (e + 1) * LANES)
        qp = jnp.concatenate([qt[e * HEAD_DIM:(e + 1) * HEAD_DIM], zpad], axis=0).astype(BF16)

        kmean = jnp.concatenate(
            [jnp.sum(kv_ref[j * MOBA_BLOCK:(j + 1) * MOBA_BLOCK, lanes].astype(F32),
                     axis=0, keepdims=True) for j in range(n_blk)], axis=0) * (1.0 / MOBA_BLOCK)
        score = _dot(kmean.astype(BF16), qp)
        past = _iota((n_blk, tq), 0) < i
        adj = jnp.where(past, score, -jnp.inf)
        sel = past & _rank_select(adj, n_blk, MOBA_TOPK)
        selb_sc[...] = jnp.where(sel, 0.0, MASKED)

        _flash_init(m_sc, l_sc, acc_sc)

        def body(kt, carry):
            k0 = pl.multiple_of(kt * tk, tk)
            _flash_step(kv_ref[pl.ds(k0, tk), lanes], qp, [selb_sc[pl.ds(kt, 1), :]],
                        m_sc, l_sc, acc_sc)
            return carry

        lax.fori_loop(0, i, body, 0)
        k0 = pl.multiple_of(q0, tk)
        _flash_step(kv_ref[pl.ds(k0, tk), lanes], qp, [_causal_bias(tk, tq)], m_sc, l_sc, acc_sc)
        outs.append(_flash_out(l_sc, acc_sc))

    o_ref[...] = jnp.concatenate(outs, axis=0).T.astype(o_ref.dtype)


def _moba_attn(main, batch, seq):
    tq = ATT_TQ
    assert ATT_TK == tq and MOBA_BLOCK == tq and seq % tq == 0
    nq = seq // tq
    t = batch * seq
    nh = MOBA_HEADS * HEAD_DIM
    return pl.pallas_call(
        functools.partial(_moba_attn_kernel, seq=seq),
        grid=(batch, MOBA_HEADS // 2, nq),
        in_specs=[
            pl.BlockSpec((tq, LANES), lambda b, hp, i: (b * nq + i, hp)),
            pl.BlockSpec((seq, 2 * LANES), lambda b, hp, i: (b, nh // (2 * LANES) + hp)),
        ],
        out_specs=pl.BlockSpec((tq, LANES), lambda b, hp, i: (b * nq + i, hp)),
        out_shape=jax.ShapeDtypeStruct((t, nh), BF16),
        scratch_shapes=[pltpu.VMEM((1, tq), F32), pltpu.VMEM((1, tq), F32),
                        pltpu.VMEM((LANES, tq), F32),
                        pltpu.VMEM((seq // MOBA_BLOCK, tq), F32)],
        compiler_params=_cparams(("parallel", "parallel", "arbitrary")),
        name="moba_attn",
    )(main, main)


def _moba_mixer(h, gain, w_in, batch, seq):
    nh = MOBA_HEADS * HEAD_DIM
    wq = w_in[:, :nh] * (HEAD_DIM ** -0.5)
    wk = w_in[:, nh:2 * nh]
    wv = w_in[:, 2 * nh:]
    w_main = jnp.concatenate([wq, _interleave_heads(wk, wv, MOBA_HEADS)], axis=1).astype(BF16)
    (main,) = _proj(h, gain, [w_main], [BF16])
    return _moba_attn(main, batch, seq)


def _conv_kernel(h_ref, hp_ref, g_ref, win_ref, cw_ref, o_ref):
    i = pl.program_id(1)
    d = D_MODEL
    g = g_ref[...]
    a = _rms(h_ref[...], g).astype(BF16)
    ap = _rms(hp_ref[...], g).astype(BF16)
    b_gate = _dot(a, win_ref[:, 0:d])
    z = _dot(a, win_ref[:, d:2 * d]) * _dot(a, win_ref[:, 2 * d:3 * d])
    zp = _dot(ap, win_ref[:, d:2 * d]) * _dot(ap, win_ref[:, 2 * d:3 * d])
    zp = jnp.where(i == 0, 0.0, zp)
    row = _iota(z.shape, 0)
    z1 = jnp.where(row == 0, zp[7:8], pltpu.roll(z, 1, 0))
    z2 = jnp.where(row == 0, zp[6:7], jnp.where(row == 1, zp[7:8], pltpu.roll(z, 2, 0)))
    y = cw_ref[0:1, :] * z2 + cw_ref[1:2, :] * z1 + cw_ref[2:3, :] * z
    o_ref[...] = (b_gate * y).astype(o_ref.dtype)


def _conv_mixer(h, gain, w_in, conv_w, batch, seq, ts=512):
    t = batch * seq
    ns = seq // ts
    halo = 8
    cw = jnp.pad(conv_w, ((0, 8 - CONV_WIDTH), (0, 0)))
    return pl.pallas_call(
        _conv_kernel,
        grid=(batch, ns),
        in_specs=[
            pl.BlockSpec((ts, D_MODEL), lambda b, i: (b * ns + i, 0)),
            pl.BlockSpec((halo, D_MODEL),
                         lambda b, i: (jnp.maximum((b * ns + i) * (ts // halo) - 1, 0), 0)),
            pl.BlockSpec((1, D_MODEL), lambda b, i: (0, 0)),
            pl.BlockSpec(w_in.shape, lambda b, i: (0, 0)),
            pl.BlockSpec(cw.shape, lambda b, i: (0, 0)),
        ],
        out_specs=pl.BlockSpec((ts, D_MODEL), lambda b, i: (b * ns + i, 0)),
        out_shape=jax.ShapeDtypeStruct((t, D_MODEL), BF16),
        compiler_params=_cparams(("parallel", "arbitrary")),
        name="conv_mix",
    )(h, h, gain.reshape(1, D_MODEL), w_in.astype(BF16), cw)


def kernel(x, norm_mix, norm_ffn, norm_final, ffn_w_gate, ffn_w_up, ffn_w_down, nsa_w_in, nsa_w_out, nsa_cmp_pos_k, nsa_cmp_pos_v, nsa_cmp_k_w1, nsa_cmp_k_w2, nsa_cmp_v_w1, nsa_cmp_v_w2, moba_w_in, moba_w_out, conv_w_in, conv_w, conv_w_out):
    batch, seq, d = x.shape
    depth = norm_mix.shape[0]
    h = x.reshape(batch * seq, d)
    for i in range(depth):
        kind, j = i % N_MIXERS, i // N_MIXERS
        if kind == 0:
            o = _nsa_mixer(h, norm_mix[i], nsa_w_in[j], nsa_cmp_pos_k[j], nsa_cmp_pos_v[j],
                           nsa_cmp_k_w1[j], nsa_cmp_k_w2[j], nsa_cmp_v_w1[j], nsa_cmp_v_w2[j],
                           batch, seq)
            wo = nsa_w_out[j]
        elif kind == 1:
            o = _moba_mixer(h, norm_mix[i], moba_w_in[j], batch, seq)
            wo = moba_w_out[j]
        else:
            o = _conv_mixer(h, norm_mix[i], conv_w_in[j], conv_w[j], batch, seq)
            wo = conv_w_out[j]
        h = _ffn(h, o, wo.astype(BF16), norm_ffn[i], ffn_w_gate[i].astype(BF16),
                 ffn_w_up[i].astype(BF16), ffn_w_down[i].astype(BF16), norm_final,
                 final_norm=(i == depth - 1))
    return h.reshape(batch, seq, d)
```

```python
import functools

import jax
import jax.numpy as jnp
from jax import lax
from jax.experimental import pallas as pl
from jax.experimental.pallas import tpu as pltpu

F32 = jnp.float32
BF16 = jnp.bfloat16

D_MODEL = 1024
HEAD_DIM = 64
RMS_EPS = 1e-6
N_MIXERS = 3

NSA_HEADS = 16
NSA_GROUPS = 4
NSA_REP = NSA_HEADS // NSA_GROUPS
NSA_KV = NSA_GROUPS * HEAD_DIM
CMP_BLOCK = 32
CMP_STRIDE = 16
CMP_HIDDEN = 256
SLC_BLOCK = 64
SLC_TOPK = 16
WINDOW = 512

MOBA_HEADS = 16
MOBA_BLOCK = 256
MOBA_TOPK = 3

CONV_WIDTH = 3

LANES = 128
VMEM_LIMIT = 56 * 1024 * 1024
MASKED = -1e30
M_FLOOR = -1e29

ATT_TQ = 256
ATT_TK = 256


def _dot(a, b):
    return jnp.dot(a, b, preferred_element_type=F32)


def _dot_tn(a, b):
    return lax.dot_general(a, b, (((0,), (0,)), ((), ())), preferred_element_type=F32)


def _rms(x, g):
    y = x * lax.rsqrt(jnp.mean(x * x, axis=-1, keepdims=True) + RMS_EPS)
    return y * g


def _cparams(sem, flags=None):
    return pltpu.CompilerParams(dimension_semantics=sem, vmem_limit_bytes=VMEM_LIMIT, flags=flags)


def _proj_kernel(*refs, n_out, chunk):
    x_ref, g_ref = refs[0], refs[1]
    w_refs = refs[2:2 + n_out]
    o_refs = refs[2 + n_out:2 + 2 * n_out]
    a = _rms(x_ref[...], g_ref[...]).astype(BF16)
    for w_ref, o_ref in zip(w_refs, o_refs):
        n = w_ref.shape[1]
        for n0 in range(0, n, chunk):
            n1 = min(n0 + chunk, n)
            o_ref[:, n0:n1] = _dot(a, w_ref[:, n0:n1]).astype(o_ref.dtype)


def _proj(h, gain, weights, out_dtypes, tm=512):
    t = h.shape[0]
    n_out = len(weights)
    in_specs = [pl.BlockSpec((tm, D_MODEL), lambda i: (i, 0)),
                pl.BlockSpec((1, D_MODEL), lambda i: (0, 0))]
    in_specs += [pl.BlockSpec(w.shape, lambda i: (0, 0)) for w in weights]
    out_specs = [pl.BlockSpec((tm, w.shape[1]), lambda i: (i, 0)) for w in weights]
    out_shape = [jax.ShapeDtypeStruct((t, w.shape[1]), dt) for w, dt in zip(weights, out_dtypes)]
    return pl.pallas_call(
        functools.partial(_proj_kernel, n_out=n_out, chunk=512),
        grid=(t // tm,),
        in_specs=in_specs, out_specs=out_specs, out_shape=out_shape,
        compiler_params=_cparams(("parallel",)),
        name="proj",
    )(h, gain.reshape(1, D_MODEL), *weights)


def _ffn_kernel(h_ref, o_ref, wo_ref, g_ref, wg_ref, wu_ref, wd_ref, gf_ref, out_ref,
                h1_sc, a_sc, acc_sc, *, final_norm):
    j = pl.program_id(1)

    @pl.when(j == 0)
    def _():
        h1 = h_ref[...] + _dot(o_ref[...], wo_ref[...])
        h1_sc[...] = h1
        a_sc[...] = _rms(h1, g_ref[...]).astype(BF16)
        acc_sc[...] = jnp.zeros_like(acc_sc)

    a = a_sc[...]
    gate = _dot(a, wg_ref[...])
    up = _dot(a, wu_ref[...])
    t = (jax.nn.silu(gate) * up).astype(BF16)
    acc_sc[...] += _dot(t, wd_ref[...])

    @pl.when(j == pl.num_programs(1) - 1)
    def _():
        y = h1_sc[...] + acc_sc[...]
        if final_norm:
            y = _rms(y, gf_ref[...])
        out_ref[...] = y


def _ffn(h, o, wo, gain, wg, wu, wd, final_gain, final_norm, tm=512, n_ff=2):
    t = h.shape[0]
    d_ff = wg.shape[1]
    tf = d_ff // n_ff
    assert tf * n_ff == d_ff and tf % LANES == 0
    return pl.pallas_call(
        functools.partial(_ffn_kernel, final_norm=final_norm),
        grid=(t // tm, n_ff),
        in_specs=[
            pl.BlockSpec((tm, D_MODEL), lambda i, j: (i, 0)),
            pl.BlockSpec((tm, o.shape[1]), lambda i, j: (i, 0)),
            pl.BlockSpec(wo.shape, lambda i, j: (0, 0)),
            pl.BlockSpec((1, D_MODEL), lambda i, j: (0, 0)),
            pl.BlockSpec((D_MODEL, tf), lambda i, j: (0, j)),
            pl.BlockSpec((D_MODEL, tf), lambda i, j: (0, j)),
            pl.BlockSpec((tf, D_MODEL), lambda i, j: (j, 0)),
            pl.BlockSpec((1, D_MODEL), lambda i, j: (0, 0)),
        ],
        out_specs=pl.BlockSpec((tm, D_MODEL), lambda i, j: (i, 0)),
        out_shape=jax.ShapeDtypeStruct((t, D_MODEL), F32),
        scratch_shapes=[pltpu.VMEM((tm, D_MODEL), F32),
                        pltpu.VMEM((tm, D_MODEL), BF16),
                        pltpu.VMEM((tm, D_MODEL), F32)],
        compiler_params=_cparams(("parallel", "arbitrary")),
        name="ffn",
    )(h, o, wo, gain.reshape(1, D_MODEL), wg, wu, wd, final_gain.reshape(1, D_MODEL))


def _flash_init(m_sc, l_sc, acc_sc):
    m_sc[...] = jnp.full(m_sc.shape, M_FLOOR, F32)
    l_sc[...] = jnp.zeros_like(l_sc)
    acc_sc[...] = jnp.zeros_like(acc_sc)


def _flash_step_multi(kvs, qps, biases, m_scs, l_scs, acc_scs):
    n = len(kvs)
    ss = []
    for c in range(n):
        s = _dot(kvs[c], qps[c])
        if callable(biases[c]):
            s = biases[c](s)
        elif biases[c] is not None:
            s = s + biases[c]
        ss.append(s)
    m_olds = [m_scs[c][...] for c in range(n)]
    l_olds = [l_scs[c][...] for c in range(n)]
    acc_olds = [acc_scs[c][...] for c in range(n)]
    m_news = [jnp.maximum(m_olds[c], jnp.max(ss[c], axis=0, keepdims=True)) for c in range(n)]
    alphas = [jnp.exp(m_olds[c] - m_news[c]) for c in range(n)]
    ps = [jnp.exp(ss[c] - m_news[c]) for c in range(n)]
    l_news = [alphas[c] * l_olds[c] + jnp.sum(ps[c], axis=0, keepdims=True) for c in range(n)]
    pvs = [_dot_tn(kvs[c], ps[c].astype(BF16)) for c in range(n)]
    acc_news = [alphas[c] * acc_olds[c] + pvs[c] for c in range(n)]
    for c in range(n):
        m_scs[c][...] = m_news[c]
        l_scs[c][...] = l_news[c]
        acc_scs[c][...] = acc_news[c]


def _flash_out(l_sc, acc_sc):
    l = l_sc[...]
    return acc_sc[HEAD_DIM:2 * HEAD_DIM, :] / jnp.where(l > 0, l, 1.0)


def _iota(shape, dim):
    return lax.broadcasted_iota(jnp.int32, shape, dim)


def _causal_bias(tk, tq, base=None):
    ok = _iota((tk, tq), 0) <= _iota((tk, tq), 1)
    return jnp.where(ok, 0.0 if base is None else base, MASKED)


def _rank_select(adj, n_rows, topk):
    j = _iota(adj.shape, 0)
    rank = jnp.zeros(adj.shape, jnp.int32)
    for k in range(n_rows):
        rk = adj[k:k + 1, :]
        beats = (rk > adj) | ((rk == adj) & (j > k))
        rank = rank + beats.astype(jnp.int32)
    return rank < topk


def _nsa_cmp_kernel(xk_ref, xv_ref, w1k_ref, w2k_ref, pk_ref, w1v_ref, w2v_ref, pv_ref, o_ref):
    half = CMP_STRIDE * HEAD_DIM
    for x_ref, w1_ref, w2_ref, p_ref, lane0 in ((xk_ref, w1k_ref, w2k_ref, pk_ref, 0),
                                                (xv_ref, w1v_ref, w2v_ref, pv_ref, HEAD_DIM)):
        posb = _dot(p_ref[...], w1_ref[...])[0:1]
        for g in range(NSA_GROUPS):
            x = x_ref[0, g]
            lo = _dot(x, w1_ref[0:half, :])
            hi = _dot(x, w1_ref[half:2 * half, :])
            n = hi.shape[0]
            hid = lo + pltpu.roll(hi, n - 1, 0) + posb
            y = _dot(jax.nn.gelu(hid).astype(BF16), w2_ref[...])
            o_ref[0, g, :, lane0:lane0 + HEAD_DIM] = y.astype(o_ref.dtype)


def _nsa_cmp(xk, xv, w1k, w2k, pk, w1v, w2v, pv):
    b, g, n, w = xk.shape
    full = lambda a: pl.BlockSpec(a.shape, lambda i: (0,) * a.ndim)
    xspec = pl.BlockSpec((1, g, n, w), lambda i: (i, 0, 0, 0))
    return pl.pallas_call(
        _nsa_cmp_kernel,
        grid=(b,),
        in_specs=[xspec, xspec, full(w1k), full(w2k), full(pk), full(w1v), full(w2v), full(pv)],
        out_specs=pl.BlockSpec((1, g, n, 2 * HEAD_DIM), lambda i: (i, 0, 0, 0)),
        out_shape=jax.ShapeDtypeStruct((b, g, n, 2 * HEAD_DIM), BF16),
        compiler_params=_cparams(("parallel",)),
        name="nsa_cmp",
    )(xk, xv, w1k, w2k, pk, w1v, w2v, pv)


def _nsa_attn_kernel(q_ref, kvs_ref, kvw_ref, kvc_ref, gt_ref, o_ref,
                     m_sc, l_sc, acc_sc, selb_sc, out_sc, *, seq, ng):
    tq, tk = ATT_TQ, ATT_TK
    i = pl.program_id(2)
    q0 = i * tq
    n_slc = seq // SLC_BLOCK
    n_cmp_pad = seq // CMP_STRIDE
    rep = NSA_REP
    sub = tk // SLC_BLOCK
    win_rows = WINDOW + tq
    groups = range(ng)
    lanes = [slice(g * LANES, (g + 1) * LANES) for g in groups]

    def heads(x):
        return jnp.concatenate([x] * rep, axis=1)

    qt = q_ref[...].astype(F32).T
    zpad = jnp.zeros((HEAD_DIM, tq), F32)
    qps = [jnp.concatenate(
        [jnp.concatenate([qt[(g * rep + r) * HEAD_DIM:(g * rep + r + 1) * HEAD_DIM], zpad], axis=0)
         for r in range(rep)], axis=1).astype(BF16) for g in groups]
    gates = jax.nn.sigmoid(gt_ref[...]).T

    def emit(g, branch, o_t, first):
        for r in range(rep):
            row = g * LANES + branch * rep + r
            val = gates[row:row + 1, :] * o_t[:, r * tq:(r + 1) * tq]
            sl = slice((g * rep + r) * HEAD_DIM, (g * rep + r + 1) * HEAD_DIM)
            if first:
                out_sc[sl, :] = val
            else:
                out_sc[sl, :] += val

    w0 = pl.multiple_of(jnp.maximum(q0 - WINDOW, 0), tk)
    d0 = pl.multiple_of(q0, tk)
    kvcs = [kvc_ref[0, g] for g in groups]
    kvws = [kvw_ref[pl.ds(w0, win_rows), lanes[g]] for g in groups]
    kvds = [kvs_ref[pl.ds(d0, tk), lanes[g]] for g in groups]
    s_cs = [_dot(kvcs[g], qps[g]) for g in groups]
    s_ws = [_dot(kvws[g], qps[g]) for g in groups]
    s_ds = [_dot(kvds[g], qps[g]) for g in groups]

    c_end = _iota((n_cmp_pad, tq), 0) * CMP_STRIDE + (CMP_BLOCK - 1)
    c_ok = (c_end <= q0 + _iota((n_cmp_pad, tq), 1)) & (c_end < seq)
    c_bias = heads(jnp.where(c_ok, 0.0, MASKED))
    k_pos = w0 + _iota((win_rows, tq), 0)
    t_pos = q0 + _iota((win_rows, tq), 1)
    w_bias = heads(jnp.where((k_pos <= t_pos) & (k_pos > t_pos - WINDOW), 0.0, MASKED))
    causal = _causal_bias(tk, tq)

    jj = _iota((n_slc, n_cmp_pad), 0) * SLC_BLOCK
    cc = _iota((n_slc, n_cmp_pad), 1) * CMP_STRIDE
    overlap_t = jnp.where((cc < jj + SLC_BLOCK) & (cc + CMP_BLOCK > jj) & (cc + CMP_BLOCK <= seq),
                          1.0, 0.0).astype(BF16)
    j_blk = _iota((n_slc, tq), 0)
    cur = (q0 + _iota((n_slc, tq), 1)) // SLC_BLOCK
    forced = (j_blk == 0) | (j_blk == cur) | (j_blk == cur - 1)
    valid = j_blk <= cur

    for g in groups:
        s = s_cs[g] + c_bias
        m = jnp.maximum(jnp.max(s, axis=0, keepdims=True), M_FLOOR)
        e = jnp.exp(s - m)
        d = jnp.sum(e, axis=0, keepdims=True)
        p = e / jnp.where(d > 0, d, 1.0)
        emit(g, 0, _dot_tn(kvcs[g], p.astype(BF16))[HEAD_DIM:2 * HEAD_DIM, :], True)

        p_sum = p[:, 0:tq]
        for r in range(1, rep):
            p_sum = p_sum + p[:, r * tq:(r + 1) * tq]
        p_hi = p_sum.astype(BF16)
        p_lo = (p_sum - p_hi.astype(F32)).astype(BF16)
        imp = _dot(overlap_t, p_hi) + _dot(overlap_t, p_lo)
        adj = jnp.where(forced, jnp.inf, jnp.where(valid, imp, -jnp.inf))
        sel = valid & _rank_select(adj, n_slc, min(SLC_TOPK, n_slc))
        selb_sc[g] = jnp.where(sel, 0.0, MASKED)

    for g in groups:
        s = s_ws[g] + w_bias
        m = jnp.max(s, axis=0, keepdims=True)
        e = jnp.exp(s - m)
        d = jnp.sum(e, axis=0, keepdims=True)
        emit(g, 2, _dot_tn(kvws[g], e.astype(BF16))[HEAD_DIM:2 * HEAD_DIM, :] / d, False)

    def slc_bias(g, blk0, extra=None):
        def apply(s):
            parts = []
            for a in range(sub):
                row = heads(selb_sc[g, pl.ds(blk0 + a, 1), :])
                part = s[a * SLC_BLOCK:(a + 1) * SLC_BLOCK, :] + row
                if extra is not None:
                    part = part + extra[a * SLC_BLOCK:(a + 1) * SLC_BLOCK, :]
                parts.append(part)
            return jnp.concatenate(parts, axis=0)
        return apply

    m_scs = [m_sc.at[g] for g in groups]
    l_scs = [l_sc.at[g] for g in groups]
    acc_scs = [acc_sc.at[g] for g in groups]

    causal_h = heads(causal)
    for g in groups:
        s = slc_bias(g, i * sub, causal_h)(s_ds[g])
        m = jnp.maximum(jnp.max(s, axis=0, keepdims=True), M_FLOOR)
        p = jnp.exp(s - m)
        m_scs[g][...] = m
        l_scs[g][...] = jnp.sum(p, axis=0, keepdims=True)
        acc_scs[g][...] = _dot_tn(kvds[g], p.astype(BF16))

    def slc_body(kt, carry):
        k0 = pl.multiple_of(kt * tk, tk)
        _flash_step_multi([kvs_ref[pl.ds(k0, tk), lanes[g]] for g in groups], qps,
                          [slc_bias(g, kt * sub) for g in groups], m_scs, l_scs, acc_scs)
        return carry

    lax.fori_loop(0, i, slc_body, 0)
    for g in groups:
        emit(g, 1, _flash_out(l_scs[g], acc_scs[g]), False)

    o_ref[...] = out_sc[...].T.astype(o_ref.dtype)


def _nsa_attn(main, kvc, gates, batch, seq, ng=2):
    tq = ATT_TQ
    assert ATT_TK == tq and WINDOW % tq == 0 and seq % tq == 0 and seq >= WINDOW + tq
    assert NSA_GROUPS % ng == 0 and tq % SLC_BLOCK == 0
    nq = seq // tq
    t = batch * seq
    gw = ng * NSA_REP * HEAD_DIM
    kvw = ng * LANES
    kv0 = NSA_HEADS * HEAD_DIM // kvw
    m = NSA_REP * tq
    n_cmp_pad = seq // CMP_STRIDE
    return pl.pallas_call(
        functools.partial(_nsa_attn_kernel, seq=seq, ng=ng),
        grid=(batch, NSA_GROUPS // ng, nq),
        in_specs=[
            pl.BlockSpec((tq, gw), lambda b, g, i: (b * nq + i, g)),
            pl.BlockSpec((seq, kvw), lambda b, g, i: (b, kv0 + g)),
            pl.BlockSpec((seq, kvw), lambda b, g, i: (b, kv0 + NSA_GROUPS // ng + g)),
            pl.BlockSpec((1, ng, n_cmp_pad, LANES), lambda b, g, i: (b, g, 0, 0)),
            pl.BlockSpec((tq, kvw), lambda b, g, i: (b * nq + i, g)),
        ],
        out_specs=pl.BlockSpec((tq, gw), lambda b, g, i: (b * nq + i, g)),
        out_shape=jax.ShapeDtypeStruct((t, NSA_HEADS * HEAD_DIM), BF16),
        scratch_shapes=[pltpu.VMEM((ng, 1, m), F32), pltpu.VMEM((ng, 1, m), F32),
                        pltpu.VMEM((ng, LANES, m), F32),
                        pltpu.VMEM((ng, seq // SLC_BLOCK, tq), F32),
                        pltpu.VMEM((gw, tq), F32)],
        compiler_params=_cparams(("parallel", "parallel", "arbitrary")),
        name="nsa_attn",
    )(main, main, main, kvc, gates)


def _interleave_heads(wa, wb, n):
    d = wa.shape[0]
    return jnp.stack([wa.reshape(d, n, HEAD_DIM), wb.reshape(d, n, HEAD_DIM)],
                     axis=2).reshape(d, 2 * n * HEAD_DIM)


def _nsa_mixer(h, gain, w_in, pos_k, pos_v, k_w1, k_w2, v_w1, v_w2, batch, seq):
    nh = NSA_HEADS * HEAD_DIM
    c = [nh + i * NSA_KV for i in range(7)]
    wq = w_in[:, :c[0]] * (HEAD_DIM ** -0.5)
    wkc, wvc, wks, wvs, wkw, wvw = (w_in[:, c[i]:c[i + 1]] for i in range(6))
    w_gate = w_in[:, c[6]:]
    w_main = jnp.concatenate(
        [wq, _interleave_heads(wks, wvs, NSA_GROUPS), _interleave_heads(wkw, wvw, NSA_GROUPS),
         wkc, wvc], axis=1).astype(BF16)
    wg = w_gate.reshape(D_MODEL, 3, NSA_GROUPS, NSA_REP).transpose(0, 2, 1, 3)
    wg = wg.reshape(D_MODEL, NSA_GROUPS, 3 * NSA_REP)
    wg = jnp.pad(wg, ((0, 0), (0, 0), (0, LANES - 3 * NSA_REP)))
    wg = wg.reshape(D_MODEL, NSA_GROUPS * LANES).astype(BF16)

    main, gates = _proj(h, gain, [w_main, wg], [BF16, F32])

    n_seg = seq // CMP_STRIDE
    base = nh + 4 * NSA_KV

    def segs(cols):
        x = cols.reshape(batch, n_seg, CMP_STRIDE, NSA_GROUPS, HEAD_DIM)
        return x.transpose(0, 3, 1, 2, 4).reshape(batch, NSA_GROUPS, n_seg, CMP_STRIDE * HEAD_DIM)

    xk = segs(main[:, base:base + NSA_KV])
    xv = segs(main[:, base + NSA_KV:base + 2 * NSA_KV])

    def pos8(p):
        return jnp.pad(p.reshape(1, CMP_BLOCK * HEAD_DIM), ((0, 7), (0, 0))).astype(BF16)

    kvc = _nsa_cmp(xk, xv, k_w1.astype(BF16), k_w2.astype(BF16), pos8(pos_k),
                   v_w1.astype(BF16), v_w2.astype(BF16), pos8(pos_v))
    return _nsa_attn(main, kvc, gates, batch, seq)


def _moba_attn_kernel(q_ref, kv_ref, o_ref, m_sc, l_sc, acc_sc, selb_sc, kmean_sc, *, seq, nh):
    tq, tk = ATT_TQ, ATT_TK
    i = pl.program_id(2)
    q0 = i * tq
    n_blk = seq // MOBA_BLOCK
    heads = range(nh)
    lanes = [slice(h * LANES, (h + 1) * LANES) for h in heads]

    @pl.when(i == 0)
    def _():
        for h in heads:
            for j in range(n_blk):
                blk = kv_ref[j * MOBA_BLOCK:(j + 1) * MOBA_BLOCK, lanes[h]].astype(F32)
                kmean_sc[h, j:j + 1, :] = jnp.sum(blk, axis=0, keepdims=True) * (1.0 / MOBA_BLOCK)

    qt = q_ref[...].astype(F32).T
    zpad = jnp.zeros((HEAD_DIM, tq), F32)
    qps = [jnp.concatenate([qt[h * HEAD_DIM:(h + 1) * HEAD_DIM], zpad], axis=0).astype(BF16)
           for h in heads]
    past = _iota((n_blk, tq), 0) < i
    for h in heads:
        score = _dot(kmean_sc[h].astype(BF16), qps[h])
        adj = jnp.where(past, score, -jnp.inf)
        sel = past & _rank_select(adj, n_blk, MOBA_TOPK)
        selb_sc[h] = jnp.where(sel, 0.0, MASKED)
        _flash_init(m_sc.at[h], l_sc.at[h], acc_sc.at[h])

    m_scs = [m_sc.at[h] for h in heads]
    l_scs = [l_sc.at[h] for h in heads]
    acc_scs = [acc_sc.at[h] for h in heads]

    def body(kt, carry):
        k0 = pl.multiple_of(kt * tk, tk)
        _flash_step_multi([kv_ref[pl.ds(k0, tk), lanes[h]] for h in heads], qps,
                          [selb_sc[h, pl.ds(kt, 1), :] for h in heads], m_scs, l_scs, acc_scs)
        return carry

    lax.fori_loop(0, i, body, 0)
    k0 = pl.multiple_of(q0, tk)
    causal = _causal_bias(tk, tq)
    _flash_step_multi([kv_ref[pl.ds(k0, tk), lanes[h]] for h in heads], qps,
                      [causal] * nh, m_scs, l_scs, acc_scs)
    outs = [_flash_out(l_scs[h], acc_scs[h]) for h in heads]
    o_ref[...] = jnp.concatenate(outs, axis=0).T.astype(o_ref.dtype)


def _moba_attn(main, batch, seq, nh=4):
    tq = ATT_TQ
    assert ATT_TK == tq and MOBA_BLOCK == tq and seq % tq == 0 and MOBA_HEADS % nh == 0
    nq = seq // tq
    t = batch * seq
    d_heads = MOBA_HEADS * HEAD_DIM
    n_blk = seq // MOBA_BLOCK
    qw = nh * HEAD_DIM
    kvw = nh * LANES
    return pl.pallas_call(
        functools.partial(_moba_attn_kernel, seq=seq, nh=nh),
        grid=(batch, MOBA_HEADS // nh, nq),
        in_specs=[
            pl.BlockSpec((tq, qw), lambda b, hg, i: (b * nq + i, hg)),
            pl.BlockSpec((seq, kvw), lambda b, hg, i: (b, d_heads // kvw + hg)),
        ],
        out_specs=pl.BlockSpec((tq, qw), lambda b, hg, i: (b * nq + i, hg)),
        out_shape=jax.ShapeDtypeStruct((t, d_heads), BF16),
        scratch_shapes=[pltpu.VMEM((nh, 1, tq), F32), pltpu.VMEM((nh, 1, tq), F32),
                        pltpu.VMEM((nh, LANES, tq), F32),
                        pltpu.VMEM((nh, n_blk, tq), F32),
                        pltpu.VMEM((nh, n_blk, LANES), F32)],
        compiler_params=_cparams(("parallel", "parallel", "arbitrary")),
        name="moba_attn",
    )(main, main)


def _moba_mixer(h, gain, w_in, batch, seq):
    nh = MOBA_HEADS * HEAD_DIM
    wq = w_in[:, :nh] * (HEAD_DIM ** -0.5)
    wk = w_in[:, nh:2 * nh]
    wv = w_in[:, 2 * nh:]
    w_main = jnp.concatenate([wq, _interleave_heads(wk, wv, MOBA_HEADS)], axis=1).astype(BF16)
    (main,) = _proj(h, gain, [w_main], [BF16])
    return _moba_attn(main, batch, seq)


def _conv_kernel(h_ref, hp_ref, g_ref, win_ref, cw_ref, o_ref):
    i = pl.program_id(1)
    d = D_MODEL
    g = g_ref[...]
    a = _rms(h_ref[...], g).astype(BF16)
    ap = _rms(hp_ref[...], g).astype(BF16)
    b_gate = _dot(a, win_ref[:, 0:d])
    z = _dot(a, win_ref[:, d:2 * d]) * _dot(a, win_ref[:, 2 * d:3 * d])
    zp = _dot(ap, win_ref[:, d:2 * d]) * _dot(ap, win_ref[:, 2 * d:3 * d])
    zp = jnp.where(i == 0, 0.0, zp)
    row = _iota(z.shape, 0)
    z1 = jnp.where(row == 0, zp[7:8], pltpu.roll(z, 1, 0))
    z2 = jnp.where(row == 0, zp[6:7], jnp.where(row == 1, zp[7:8], pltpu.roll(z, 2, 0)))
    y = cw_ref[0:1, :] * z2 + cw_ref[1:2, :] * z1 + cw_ref[2:3, :] * z
    o_ref[...] = (b_gate * y).astype(o_ref.dtype)


def _conv_mixer(h, gain, w_in, conv_w, batch, seq, ts=512):
    t = batch * seq
    ns = seq // ts
    halo = 8
    cw = jnp.pad(conv_w, ((0, 8 - CONV_WIDTH), (0, 0)))
    return pl.pallas_call(
        _conv_kernel,
        grid=(batch, ns),
        in_specs=[
            pl.BlockSpec((ts, D_MODEL), lambda b, i: (b * ns + i, 0)),
            pl.BlockSpec((halo, D_MODEL),
                         lambda b, i: (jnp.maximum((b * ns + i) * (ts // halo) - 1, 0), 0)),
            pl.BlockSpec((1, D_MODEL), lambda b, i: (0, 0)),
            pl.BlockSpec(w_in.shape, lambda b, i: (0, 0)),
            pl.BlockSpec(cw.shape, lambda b, i: (0, 0)),
        ],
        out_specs=pl.BlockSpec((ts, D_MODEL), lambda b, i: (b * ns + i, 0)),
        out_shape=jax.ShapeDtypeStruct((t, D_MODEL), BF16),
        compiler_params=_cparams(("parallel", "arbitrary")),
        name="conv_mix",
    )(h, h, gain.reshape(1, D_MODEL), w_in.astype(BF16), cw)


def kernel(x, norm_mix, norm_ffn, norm_final, ffn_w_gate, ffn_w_up, ffn_w_down, nsa_w_in, nsa_w_out, nsa_cmp_pos_k, nsa_cmp_pos_v, nsa_cmp_k_w1, nsa_cmp_k_w2, nsa_cmp_v_w1, nsa_cmp_v_w2, moba_w_in, moba_w_out, conv_w_in, conv_w, conv_w_out):
    batch, seq, d = x.shape
    depth = norm_mix.shape[0]
    h = x.reshape(batch * seq, d)
    for i in range(depth):
        kind, j = i % N_MIXERS, i // N_MIXERS
        if kind == 0:
            o = _nsa_mixer(h, norm_mix[i], nsa_w_in[j], nsa_cmp_pos_k[j], nsa_cmp_pos_v[j],
                           nsa_cmp_k_w1[j], nsa_cmp_k_w2[j], nsa_cmp_v_w1[j], nsa_cmp_v_w2[j],
                           batch, seq)
            wo = nsa_w_out[j]
        elif kind == 1:
            o = _moba_mixer(h, norm_mix[i], moba_w_in[j], batch, seq)
            wo = moba_w_out[j]
        else:
            o = _conv_mixer(h, norm_mix[i], conv_w_in[j], conv_w[j], batch, seq)
            wo = conv_w_out[j]
        h = _ffn(h, o, wo.astype(BF16), norm_ffn[i], ffn_w_gate[i].astype(BF16),
                 ffn_w_up[i].astype(BF16), ffn_w_down[i].astype(BF16), norm_final,
                 final_norm=(i == depth - 1))
    return h.reshape(batch, seq, d)
```

```python
import functools

import jax
import jax.numpy as jnp
from jax import lax
from jax.experimental import pallas as pl
from jax.experimental.pallas import tpu as pltpu

F32 = jnp.float32
BF16 = jnp.bfloat16

D_MODEL = 1024
HEAD_DIM = 64
RMS_EPS = 1e-6
N_MIXERS = 3

NSA_HEADS = 16
NSA_GROUPS = 4
NSA_REP = NSA_HEADS // NSA_GROUPS
NSA_KV = NSA_GROUPS * HEAD_DIM
CMP_BLOCK = 32
CMP_STRIDE = 16
CMP_HIDDEN = 256
SLC_BLOCK = 64
SLC_TOPK = 16
WINDOW = 512

MOBA_HEADS = 16
MOBA_BLOCK = 256
MOBA_TOPK = 3

CONV_WIDTH = 3

LANES = 128
VMEM_LIMIT = 56 * 1024 * 1024
MASKED = -1e30
M_FLOOR = -1e29
LOG2E = 1.4426950408889634

ATT_TQ = 256
ATT_TK = 256


def _dot(a, b):
    return jnp.dot(a, b, preferred_element_type=F32)


def _dot_tn(a, b):
    return lax.dot_general(a, b, (((0,), (0,)), ((), ())), preferred_element_type=F32)


def _rms(x, g):
    y = x * lax.rsqrt(jnp.mean(x * x, axis=-1, keepdims=True) + RMS_EPS)
    return y * g


def _cparams(sem, flags=None):
    return pltpu.CompilerParams(dimension_semantics=sem, vmem_limit_bytes=VMEM_LIMIT, flags=flags)


def _proj_kernel(*refs, n_out, chunk):
    x_ref, g_ref, sc_ref = refs[0], refs[1], refs[2]
    w_refs = refs[3:3 + n_out]
    o_refs = refs[3 + n_out:3 + 2 * n_out]
    a = _rms(x_ref[...], g_ref[...]).astype(BF16)
    for k, (w_ref, o_ref) in enumerate(zip(w_refs, o_refs)):
        n = w_ref.shape[1]
        for n0 in range(0, n, chunk):
            n1 = min(n0 + chunk, n)
            y = _dot(a, w_ref[:, n0:n1])
            if k == 0:
                y = y * sc_ref[:, n0:n1]
            o_ref[:, n0:n1] = y.astype(o_ref.dtype)


def _proj(h, gain, scale0, weights, out_dtypes, tm=512):
    t = h.shape[0]
    n_out = len(weights)
    const = lambda a: pl.BlockSpec(a.shape, lambda i: (0, 0))
    gain = gain.reshape(1, D_MODEL)
    in_specs = [pl.BlockSpec((tm, D_MODEL), lambda i: (i, 0)), const(gain), const(scale0)]
    in_specs += [const(w) for w in weights]
    out_specs = [pl.BlockSpec((tm, w.shape[1]), lambda i: (i, 0)) for w in weights]
    out_shape = [jax.ShapeDtypeStruct((t, w.shape[1]), dt) for w, dt in zip(weights, out_dtypes)]
    return pl.pallas_call(
        functools.partial(_proj_kernel, n_out=n_out, chunk=512),
        grid=(t // tm,),
        in_specs=in_specs, out_specs=out_specs, out_shape=out_shape,
        compiler_params=_cparams(("parallel",)),
        name="proj",
    )(h, gain, scale0, *weights)


def _q_scale_row(n_q, n_total):
    q = jnp.full((1, n_q), (HEAD_DIM ** -0.5) * LOG2E, F32)
    return jnp.concatenate([q, jnp.ones((1, n_total - n_q), F32)], axis=1)


def _ffn_kernel(h_ref, o_ref, wo_ref, g_ref, wg_ref, wu_ref, wd_ref, gf_ref, out_ref,
                h1_sc, a_sc, acc_sc, *, final_norm):
    j = pl.program_id(1)

    @pl.when(j == 0)
    def _():
        h1 = h_ref[...] + _dot(o_ref[...], wo_ref[...])
        h1_sc[...] = h1
        a_sc[...] = _rms(h1, g_ref[...]).astype(BF16)
        acc_sc[...] = jnp.zeros_like(acc_sc)

    a = a_sc[...]
    gate = _dot(a, wg_ref[...])
    up = _dot(a, wu_ref[...])
    t = (jax.nn.silu(gate) * up).astype(BF16)
    acc_sc[...] += _dot(t, wd_ref[...])

    @pl.when(j == pl.num_programs(1) - 1)
    def _():
        y = h1_sc[...] + acc_sc[...]
        if final_norm:
            y = _rms(y, gf_ref[...])
        out_ref[...] = y


def _ffn(h, o, wo, gain, wg, wu, wd, final_gain, final_norm, tm=512, n_ff=2):
    t = h.shape[0]
    d_ff = wg.shape[1]
    tf = d_ff // n_ff
    assert tf * n_ff == d_ff and tf % LANES == 0
    return pl.pallas_call(
        functools.partial(_ffn_kernel, final_norm=final_norm),
        grid=(t // tm, n_ff),
        in_specs=[
            pl.BlockSpec((tm, D_MODEL), lambda i, j: (i, 0)),
            pl.BlockSpec((tm, o.shape[1]), lambda i, j: (i, 0)),
            pl.BlockSpec(wo.shape, lambda i, j: (0, 0)),
            pl.BlockSpec((1, D_MODEL), lambda i, j: (0, 0)),
            pl.BlockSpec((D_MODEL, tf), lambda i, j: (0, j)),
            pl.BlockSpec((D_MODEL, tf), lambda i, j: (0, j)),
            pl.BlockSpec((tf, D_MODEL), lambda i, j: (j, 0)),
            pl.BlockSpec((1, D_MODEL), lambda i, j: (0, 0)),
        ],
        out_specs=pl.BlockSpec((tm, D_MODEL), lambda i, j: (i, 0)),
        out_shape=jax.ShapeDtypeStruct((t, D_MODEL), F32),
        scratch_shapes=[pltpu.VMEM((tm, D_MODEL), F32),
                        pltpu.VMEM((tm, D_MODEL), BF16),
                        pltpu.VMEM((tm, D_MODEL), F32)],
        compiler_params=_cparams(("parallel", "arbitrary")),
        name="ffn",
    )(h, o, wo, gain.reshape(1, D_MODEL), wg, wu, wd, final_gain.reshape(1, D_MODEL))


def _block_max(s, rows):
    if rows is None:
        return jnp.max(s, axis=0, keepdims=True)
    h = s.shape[0] // len(rows)
    ms = [jnp.max(s[a * h:(a + 1) * h], axis=0, keepdims=True) + row for a, row in enumerate(rows)]
    return functools.reduce(jnp.maximum, ms)


def _block_exp2(s, rows, m):
    if rows is None:
        return jnp.exp2(s - m)
    h = s.shape[0] // len(rows)
    return jnp.concatenate([jnp.exp2(s[a * h:(a + 1) * h] - (m - row))
                            for a, row in enumerate(rows)], axis=0)


def _flash_step_multi(kvs, qps, rows, extras, m_scs, l_scs, acc_scs):
    n = len(kvs)
    ss = []
    for c in range(n):
        s = _dot(kvs[c], qps[c])
        if extras[c] is not None:
            s = s + extras[c]
        ss.append(s)
    m_olds = [m_scs[c][...] for c in range(n)]
    l_olds = [l_scs[c][...] for c in range(n)]
    acc_olds = [acc_scs[c][...] for c in range(n)]
    m_news = [jnp.maximum(m_olds[c], _block_max(ss[c], rows[c])) for c in range(n)]
    alphas = [jnp.exp2(m_olds[c] - m_news[c]) for c in range(n)]
    ps = [_block_exp2(ss[c], rows[c], m_news[c]) for c in range(n)]
    l_news = [alphas[c] * l_olds[c] + jnp.sum(ps[c], axis=0, keepdims=True) for c in range(n)]
    pvs = [_dot_tn(kvs[c], ps[c].astype(BF16)) for c in range(n)]
    acc_news = [alphas[c] * acc_olds[c] + pvs[c] for c in range(n)]
    for c in range(n):
        m_scs[c][...] = m_news[c]
        l_scs[c][...] = l_news[c]
        acc_scs[c][...] = acc_news[c]


def _flash_first_multi(kvs, ss, rows, m_scs, l_scs, acc_scs):
    n = len(kvs)
    ms = [jnp.maximum(_block_max(ss[c], rows[c]), M_FLOOR) for c in range(n)]
    ps = [_block_exp2(ss[c], rows[c], ms[c]) for c in range(n)]
    ls = [jnp.sum(ps[c], axis=0, keepdims=True) for c in range(n)]
    accs = [_dot_tn(kvs[c], ps[c].astype(BF16)) for c in range(n)]
    for c in range(n):
        m_scs[c][...] = ms[c]
        l_scs[c][...] = ls[c]
        acc_scs[c][...] = accs[c]


def _flash_out(l_sc, acc_sc):
    l = l_sc[...]
    return acc_sc[HEAD_DIM:2 * HEAD_DIM, :] / jnp.where(l > 0, l, 1.0)


def _iota(shape, dim):
    return lax.broadcasted_iota(jnp.int32, shape, dim)


def _causal_bias(tk, tq, base=None):
    ok = _iota((tk, tq), 0) <= _iota((tk, tq), 1)
    return jnp.where(ok, 0.0 if base is None else base, MASKED)


def _rank_select(adj, n_rows, topk):
    j = _iota(adj.shape, 0)
    rank = jnp.zeros(adj.shape, jnp.int32)
    for k in range(n_rows):
        rk = adj[k:k + 1, :]
        beats = (rk > adj) | ((rk == adj) & (j > k))
        rank = rank + beats.astype(jnp.int32)
    return rank < topk


def _nsa_cmp_kernel(xk_ref, xv_ref, w1k_ref, w2k_ref, pk_ref, w1v_ref, w2v_ref, pv_ref, o_ref):
    half = CMP_STRIDE * HEAD_DIM
    for x_ref, w1_ref, w2_ref, p_ref, lane0 in ((xk_ref, w1k_ref, w2k_ref, pk_ref, 0),
                                                (xv_ref, w1v_ref, w2v_ref, pv_ref, HEAD_DIM)):
        posb = _dot(p_ref[...], w1_ref[...])[0:1]
        for g in range(NSA_GROUPS):
            x = x_ref[0, g]
            lo = _dot(x, w1_ref[0:half, :])
            hi = _dot(x, w1_ref[half:2 * half, :])
            n = hi.shape[0]
            hid = lo + pltpu.roll(hi, n - 1, 0) + posb
            y = _dot(jax.nn.gelu(hid).astype(BF16), w2_ref[...])
            o_ref[0, g, :, lane0:lane0 + HEAD_DIM] = y.astype(o_ref.dtype)


def _nsa_cmp(xk, xv, w1k, w2k, pk, w1v, w2v, pv):
    b, g, n, w = xk.shape
    full = lambda a: pl.BlockSpec(a.shape, lambda i: (0,) * a.ndim)
    xspec = pl.BlockSpec((1, g, n, w), lambda i: (i, 0, 0, 0))
    return pl.pallas_call(
        _nsa_cmp_kernel,
        grid=(b,),
        in_specs=[xspec, xspec, full(w1k), full(w2k), full(pk), full(w1v), full(w2v), full(pv)],
        out_specs=pl.BlockSpec((1, g, n, 2 * HEAD_DIM), lambda i: (i, 0, 0, 0)),
        out_shape=jax.ShapeDtypeStruct((b, g, n, 2 * HEAD_DIM), BF16),
        compiler_params=_cparams(("parallel",)),
        name="nsa_cmp",
    )(xk, xv, w1k, w2k, pk, w1v, w2v, pv)


def _nsa_attn_kernel(q_ref, kvs_ref, kvw_ref, kvc_ref, gt_ref, o_ref,
                     m_sc, l_sc, acc_sc, selb_sc, out_sc, *, seq, ng):
    tq, tk = ATT_TQ, ATT_TK
    i = pl.program_id(2)
    q0 = i * tq
    n_slc = seq // SLC_BLOCK
    n_cmp_pad = seq // CMP_STRIDE
    rep = NSA_REP
    sub = tk // SLC_BLOCK
    win_rows = WINDOW + tq
    groups = range(ng)
    lanes = [slice(g * LANES, (g + 1) * LANES) for g in groups]

    def heads(x):
        return jnp.concatenate([x] * rep, axis=1)

    qt = q_ref[...].astype(F32).T
    zpad = jnp.zeros((HEAD_DIM, tq), F32)
    qps = [jnp.concatenate(
        [jnp.concatenate([qt[(g * rep + r) * HEAD_DIM:(g * rep + r + 1) * HEAD_DIM], zpad], axis=0)
         for r in range(rep)], axis=1).astype(BF16) for g in groups]
    gates = jax.nn.sigmoid(gt_ref[...]).T

    def emit(g, branch, o_t, first):
        for r in range(rep):
            row = g * LANES + branch * rep + r
            val = gates[row:row + 1, :] * o_t[:, r * tq:(r + 1) * tq]
            sl = slice((g * rep + r) * HEAD_DIM, (g * rep + r + 1) * HEAD_DIM)
            if first:
                out_sc[sl, :] = val
            else:
                out_sc[sl, :] += val

    w0 = pl.multiple_of(jnp.maximum(q0 - WINDOW, 0), tk)
    d0 = pl.multiple_of(q0, tk)
    kvcs = [kvc_ref[0, g] for g in groups]
    kvws = [kvw_ref[pl.ds(w0, win_rows), lanes[g]] for g in groups]
    kvds = [kvs_ref[pl.ds(d0, tk), lanes[g]] for g in groups]
    s_cs = [_dot(kvcs[g], qps[g]) for g in groups]
    s_ws = [_dot(kvws[g], qps[g]) for g in groups]
    s_ds = [_dot(kvds[g], qps[g]) for g in groups]

    c_end = _iota((n_cmp_pad, tq), 0) * CMP_STRIDE + (CMP_BLOCK - 1)
    c_ok = (c_end <= q0 + _iota((n_cmp_pad, tq), 1)) & (c_end < seq)
    c_bias = heads(jnp.where(c_ok, 0.0, MASKED))
    k_pos = w0 + _iota((win_rows, tq), 0)
    t_pos = q0 + _iota((win_rows, tq), 1)
    w_bias = heads(jnp.where((k_pos <= t_pos) & (k_pos > t_pos - WINDOW), 0.0, MASKED))
    causal = _causal_bias(tk, tq)

    jj = _iota((n_slc, n_cmp_pad), 0) * SLC_BLOCK
    cc = _iota((n_slc, n_cmp_pad), 1) * CMP_STRIDE
    overlap_t = jnp.where((cc < jj + SLC_BLOCK) & (cc + CMP_BLOCK > jj) & (cc + CMP_BLOCK <= seq),
                          1.0, 0.0).astype(BF16)
    j_blk = _iota((n_slc, tq), 0)
    cur = (q0 + _iota((n_slc, tq), 1)) // SLC_BLOCK
    forced = (j_blk == 0) | (j_blk == cur) | (j_blk == cur - 1)
    valid = j_blk <= cur

    ss = [s_cs[g] + c_bias for g in groups]
    ms = [jnp.maximum(jnp.max(s, axis=0, keepdims=True), M_FLOOR) for s in ss]
    es = [jnp.exp2(ss[g] - ms[g]) for g in groups]
    ds = [jnp.sum(e, axis=0, keepdims=True) for e in es]
    p_cs = [es[g] / jnp.where(ds[g] > 0, ds[g], 1.0) for g in groups]
    o_cs = [_dot_tn(kvcs[g], p_cs[g].astype(BF16))[HEAD_DIM:2 * HEAD_DIM, :] for g in groups]

    ss = [s_ws[g] + w_bias for g in groups]
    ms = [jnp.max(s, axis=0, keepdims=True) for s in ss]
    es = [jnp.exp2(ss[g] - ms[g]) for g in groups]
    ds = [jnp.sum(e, axis=0, keepdims=True) for e in es]
    o_ws = [_dot_tn(kvws[g], es[g].astype(BF16))[HEAD_DIM:2 * HEAD_DIM, :] / ds[g] for g in groups]

    selbs = []
    for g in groups:
        p_sum = p_cs[g][:, 0:tq]
        for r in range(1, rep):
            p_sum = p_sum + p_cs[g][:, r * tq:(r + 1) * tq]
        p_hi = p_sum.astype(BF16)
        p_lo = (p_sum - p_hi.astype(F32)).astype(BF16)
        imp = _dot(overlap_t, p_hi) + _dot(overlap_t, p_lo)
        adj = jnp.where(forced, jnp.inf, jnp.where(valid, imp, -jnp.inf))
        sel = valid & _rank_select(adj, n_slc, min(SLC_TOPK, n_slc))
        selbs.append(jnp.where(sel, 0.0, MASKED))

    for g in groups:
        selb_sc[g] = selbs[g]
        emit(g, 0, o_cs[g], True)
        emit(g, 2, o_ws[g], False)

    def sel_rows(g, blk0):
        return [heads(selb_sc[g, pl.ds(blk0 + a, 1), :]) for a in range(sub)]

    m_scs = [m_sc.at[g] for g in groups]
    l_scs = [l_sc.at[g] for g in groups]
    acc_scs = [acc_sc.at[g] for g in groups]

    causal_h = heads(causal)
    _flash_first_multi(kvds, [s_ds[g] + causal_h for g in groups],
                       [sel_rows(g, i * sub) for g in groups], m_scs, l_scs, acc_scs)

    def slc_body(kt, carry):
        k0 = pl.multiple_of(kt * tk, tk)
        _flash_step_multi([kvs_ref[pl.ds(k0, tk), lanes[g]] for g in groups], qps,
                          [sel_rows(g, kt * sub) for g in groups], [None] * ng,
                          m_scs, l_scs, acc_scs)
        return carry

    lax.fori_loop(0, i, slc_body, 0)
    for g in groups:
        emit(g, 1, _flash_out(l_scs[g], acc_scs[g]), False)

    o_ref[...] = out_sc[...].T.astype(o_ref.dtype)


def _nsa_attn(main, kvc, gates, batch, seq, ng=4):
    tq = ATT_TQ
    assert ATT_TK == tq and WINDOW % tq == 0 and seq % tq == 0 and seq >= WINDOW + tq
    assert NSA_GROUPS % ng == 0 and tq % SLC_BLOCK == 0
    nq = seq // tq
    t = batch * seq
    gw = ng * NSA_REP * HEAD_DIM
    kvw = ng * LANES
    kv0 = NSA_HEADS * HEAD_DIM // kvw
    m = NSA_REP * tq
    n_cmp_pad = seq // CMP_STRIDE
    return pl.pallas_call(
        functools.partial(_nsa_attn_kernel, seq=seq, ng=ng),
        grid=(batch, NSA_GROUPS // ng, nq),
        in_specs=[
            pl.BlockSpec((tq, gw), lambda b, g, i: (b * nq + i, g)),
            pl.BlockSpec((seq, kvw), lambda b, g, i: (b, kv0 + g)),
            pl.BlockSpec((seq, kvw), lambda b, g, i: (b, kv0 + NSA_GROUPS // ng + g)),
            pl.BlockSpec((1, ng, n_cmp_pad, LANES), lambda b, g, i: (b, g, 0, 0)),
            pl.BlockSpec((tq, kvw), lambda b, g, i: (b * nq + i, g)),
        ],
        out_specs=pl.BlockSpec((tq, gw), lambda b, g, i: (b * nq + i, g)),
        out_shape=jax.ShapeDtypeStruct((t, NSA_HEADS * HEAD_DIM), BF16),
        scratch_shapes=[pltpu.VMEM((ng, 1, m), F32), pltpu.VMEM((ng, 1, m), F32),
                        pltpu.VMEM((ng, LANES, m), F32),
                        pltpu.VMEM((ng, seq // SLC_BLOCK, tq), F32),
                        pltpu.VMEM((gw, tq), F32)],
        compiler_params=_cparams(("parallel", "parallel", "arbitrary")),
        name="nsa_attn",
    )(main, main, main, kvc, gates)


def _interleave_heads(wa, wb, n):
    d = wa.shape[0]
    return jnp.stack([wa.reshape(d, n, HEAD_DIM), wb.reshape(d, n, HEAD_DIM)],
                     axis=2).reshape(d, 2 * n * HEAD_DIM)


def _nsa_mixer(h, gain, w_in, pos_k, pos_v, k_w1, k_w2, v_w1, v_w2, batch, seq):
    nh = NSA_HEADS * HEAD_DIM
    c = [nh + i * NSA_KV for i in range(7)]
    wq = w_in[:, :c[0]]
    wkc, wvc, wks, wvs, wkw, wvw = (w_in[:, c[i]:c[i + 1]] for i in range(6))
    w_gate = w_in[:, c[6]:]
    w_main = jnp.concatenate(
        [wq, _interleave_heads(wks, wvs, NSA_GROUPS), _interleave_heads(wkw, wvw, NSA_GROUPS),
         wkc, wvc], axis=1).astype(BF16)
    wg = w_gate.reshape(D_MODEL, 3, NSA_GROUPS, NSA_REP).transpose(0, 2, 1, 3)
    wg = wg.reshape(D_MODEL, NSA_GROUPS, 3 * NSA_REP)
    wg = jnp.pad(wg, ((0, 0), (0, 0), (0, LANES - 3 * NSA_REP)))
    wg = wg.reshape(D_MODEL, NSA_GROUPS * LANES).astype(BF16)

    main, gates = _proj(h, gain, _q_scale_row(nh, w_main.shape[1]), [w_main, wg], [BF16, F32])

    n_seg = seq // CMP_STRIDE
    base = nh + 4 * NSA_KV

    def segs(cols):
        x = cols.reshape(batch, n_seg, CMP_STRIDE, NSA_GROUPS, HEAD_DIM)
        return x.transpose(0, 3, 1, 2, 4).reshape(batch, NSA_GROUPS, n_seg, CMP_STRIDE * HEAD_DIM)

    xk = segs(main[:, base:base + NSA_KV])
    xv = segs(main[:, base + NSA_KV:base + 2 * NSA_KV])

    def pos8(p):
        return jnp.pad(p.reshape(1, CMP_BLOCK * HEAD_DIM), ((0, 7), (0, 0))).astype(BF16)

    kvc = _nsa_cmp(xk, xv, k_w1.astype(BF16), k_w2.astype(BF16), pos8(pos_k),
                   v_w1.astype(BF16), v_w2.astype(BF16), pos8(pos_v))
    return _nsa_attn(main, kvc, gates, batch, seq)


def _moba_attn_kernel(q_ref, kv_ref, o_ref, m_sc, l_sc, acc_sc, selb_sc, kmean_sc, *, seq, nh):
    tq, tk = ATT_TQ, ATT_TK
    i = pl.program_id(2)
    q0 = i * tq
    n_blk = seq // MOBA_BLOCK
    heads = range(nh)
    lanes = [slice(h * LANES, (h + 1) * LANES) for h in heads]

    @pl.when(i == 0)
    def _():
        for h in heads:
            for j in range(n_blk):
                blk = kv_ref[j * MOBA_BLOCK:(j + 1) * MOBA_BLOCK, lanes[h]].astype(F32)
                kmean_sc[h, j:j + 1, :] = jnp.sum(blk, axis=0, keepdims=True) * (1.0 / MOBA_BLOCK)

    qt = q_ref[...].astype(F32).T
    zpad = jnp.zeros((HEAD_DIM, tq), F32)
    qps = [jnp.concatenate([qt[h * HEAD_DIM:(h + 1) * HEAD_DIM], zpad], axis=0).astype(BF16)
           for h in heads]
    d0 = pl.multiple_of(q0, tk)
    kvds = [kv_ref[pl.ds(d0, tk), lanes[h]] for h in heads]
    s_ds = [_dot(kvds[h], qps[h]) for h in heads]
    past = _iota((n_blk, tq), 0) < i
    for h in heads:
        score = _dot(kmean_sc[h].astype(BF16), qps[h])
        adj = jnp.where(past, score, -jnp.inf)
        sel = past & _rank_select(adj, n_blk, MOBA_TOPK)
        selb_sc[h] = jnp.where(sel, 0.0, MASKED)

    m_scs = [m_sc.at[h] for h in heads]
    l_scs = [l_sc.at[h] for h in heads]
    acc_scs = [acc_sc.at[h] for h in heads]

    causal = _causal_bias(tk, tq)
    _flash_first_multi(kvds, [s_ds[h] + causal for h in heads], [None] * nh, m_scs, l_scs, acc_scs)

    def body(kt, carry):
        k0 = pl.multiple_of(kt * tk, tk)
        _flash_step_multi([kv_ref[pl.ds(k0, tk), lanes[h]] for h in heads], qps,
                          [[selb_sc[h, pl.ds(kt, 1), :]] for h in heads], [None] * nh,
                          m_scs, l_scs, acc_scs)
        return carry

    lax.fori_loop(0, i, body, 0)
    outs = [_flash_out(l_scs[h], acc_scs[h]) for h in heads]
    o_ref[...] = jnp.concatenate(outs, axis=0).T.astype(o_ref.dtype)


def _moba_attn(main, batch, seq, nh=8):
    tq = ATT_TQ
    assert ATT_TK == tq and MOBA_BLOCK == tq and seq % tq == 0 and MOBA_HEADS % nh == 0
    nq = seq // tq
    t = batch * seq
    d_heads = MOBA_HEADS * HEAD_DIM
    n_blk = seq // MOBA_BLOCK
    qw = nh * HEAD_DIM
    kvw = nh * LANES
    return pl.pallas_call(
        functools.partial(_moba_attn_kernel, seq=seq, nh=nh),
        grid=(batch, MOBA_HEADS // nh, nq),
        in_specs=[
            pl.BlockSpec((tq, qw), lambda b, hg, i: (b * nq + i, hg)),
            pl.BlockSpec((seq, kvw), lambda b, hg, i: (b, d_heads // kvw + hg)),
        ],
        out_specs=pl.BlockSpec((tq, qw), lambda b, hg, i: (b * nq + i, hg)),
        out_shape=jax.ShapeDtypeStruct((t, d_heads), BF16),
        scratch_shapes=[pltpu.VMEM((nh, 1, tq), F32), pltpu.VMEM((nh, 1, tq), F32),
                        pltpu.VMEM((nh, LANES, tq), F32),
                        pltpu.VMEM((nh, n_blk, tq), F32),
                        pltpu.VMEM((nh, n_blk, LANES), F32)],
        compiler_params=_cparams(("parallel", "parallel", "arbitrary")),
        name="moba_attn",
    )(main, main)


def _moba_mixer(h, gain, w_in, batch, seq):
    nh = MOBA_HEADS * HEAD_DIM
    wq = w_in[:, :nh]
    wk = w_in[:, nh:2 * nh]
    wv = w_in[:, 2 * nh:]
    w_main = jnp.concatenate([wq, _interleave_heads(wk, wv, MOBA_HEADS)], axis=1).astype(BF16)
    (main,) = _proj(h, gain, _q_scale_row(nh, w_main.shape[1]), [w_main], [BF16])
    return _moba_attn(main, batch, seq)


def _conv_kernel(h_ref, hp_ref, g_ref, win_ref, cw_ref, o_ref):
    i = pl.program_id(1)
    d = D_MODEL
    g = g_ref[...]
    a = _rms(h_ref[...], g).astype(BF16)
    ap = _rms(hp_ref[...], g).astype(BF16)
    b_gate = _dot(a, win_ref[:, 0:d])
    z = _dot(a, win_ref[:, d:2 * d]) * _dot(a, win_ref[:, 2 * d:3 * d])
    zp = _dot(ap, win_ref[:, d:2 * d]) * _dot(ap, win_ref[:, 2 * d:3 * d])
    zp = jnp.where(i == 0, 0.0, zp)
    row = _iota(z.shape, 0)
    z1 = jnp.where(row == 0, zp[7:8], pltpu.roll(z, 1, 0))
    z2 = jnp.where(row == 0, zp[6:7], jnp.where(row == 1, zp[7:8], pltpu.roll(z, 2, 0)))
    y = cw_ref[0:1, :] * z2 + cw_ref[1:2, :] * z1 + cw_ref[2:3, :] * z
    o_ref[...] = (b_gate * y).astype(o_ref.dtype)


def _conv_mixer(h, gain, w_in, conv_w, batch, seq, ts=512):
    t = batch * seq
    ns = seq // ts
    halo = 8
    cw = jnp.pad(conv_w, ((0, 8 - CONV_WIDTH), (0, 0)))
    return pl.pallas_call(
        _conv_kernel,
        grid=(batch, ns),
        in_specs=[
            pl.BlockSpec((ts, D_MODEL), lambda b, i: (b * ns + i, 0)),
            pl.BlockSpec((halo, D_MODEL),
                         lambda b, i: (jnp.maximum((b * ns + i) * (ts // halo) - 1, 0), 0)),
            pl.BlockSpec((1, D_MODEL), lambda b, i: (0, 0)),
            pl.BlockSpec(w_in.shape, lambda b, i: (0, 0)),
            pl.BlockSpec(cw.shape, lambda b, i: (0, 0)),
        ],
        out_specs=pl.BlockSpec((ts, D_MODEL), lambda b, i: (b * ns + i, 0)),
        out_shape=jax.ShapeDtypeStruct((t, D_MODEL), BF16),
        compiler_params=_cparams(("parallel", "arbitrary")),
        name="conv_mix",
    )(h, h, gain.reshape(1, D_MODEL), w_in.astype(BF16), cw)


def kernel(x, norm_mix, norm_ffn, norm_final, ffn_w_gate, ffn_w_up, ffn_w_down, nsa_w_in, nsa_w_out, nsa_cmp_pos_k, nsa_cmp_pos_v, nsa_cmp_k_w1, nsa_cmp_k_w2, nsa_cmp_v_w1, nsa_cmp_v_w2, moba_w_in, moba_w_out, conv_w_in, conv_w, conv_w_out):
    batch, seq, d = x.shape
    depth = norm_mix.shape[0]
    h = x.reshape(batch * seq, d)
    for i in range(depth):
        kind, j = i % N_MIXERS, i // N_MIXERS
        if kind == 0:
            o = _nsa_mixer(h, norm_mix[i], nsa_w_in[j], nsa_cmp_pos_k[j], nsa_cmp_pos_v[j],
                           nsa_cmp_k_w1[j], nsa_cmp_k_w2[j], nsa_cmp_v_w1[j], nsa_cmp_v_w2[j],
                           batch, seq)
            wo = nsa_w_out[j]
        elif kind == 1:
            o = _moba_mixer(h, norm_mix[i], moba_w_in[j], batch, seq)
            wo = moba_w_out[j]
        else:
            o = _conv_mixer(h, norm_mix[i], conv_w_in[j], conv_w[j], batch, seq)
            wo = conv_w_out[j]
        h = _ffn(h, o, wo.astype(BF16), norm_ffn[i], ffn_w_gate[i].astype(BF16),
                 ffn_w_up[i].astype(BF16), ffn_w_down[i].astype(BF16), norm_final,
                 final_norm=(i == depth - 1))
    return h.reshape(batch, seq, d)
```

```python
import functools

import jax
import jax.numpy as jnp
from jax import lax
from jax.experimental import pallas as pl
from jax.experimental.pallas import tpu as pltpu

F32 = jnp.float32
BF16 = jnp.bfloat16

D_MODEL = 1024
HEAD_DIM = 64
RMS_EPS = 1e-6
N_MIXERS = 3

NSA_HEADS = 16
NSA_GROUPS = 4
NSA_REP = NSA_HEADS // NSA_GROUPS
NSA_KV = NSA_GROUPS * HEAD_DIM
CMP_BLOCK = 32
CMP_STRIDE = 16
CMP_HIDDEN = 256
SLC_BLOCK = 64
SLC_TOPK = 16
WINDOW = 512

MOBA_HEADS = 16
MOBA_BLOCK = 256
MOBA_TOPK = 3

CONV_WIDTH = 3

LANES = 128
MXU_TILE = 256
VMEM_LIMIT = 56 * 1024 * 1024
MASKED = -1e30
M_FLOOR = -1e29
LOG2E = 1.4426950408889634

ONES_ROWS = 16
ACC_ROWS = ONES_ROWS + HEAD_DIM

ATT_TQ = 256
ATT_TK = 256


def _dot(a, b):
    return jnp.dot(a, b, preferred_element_type=F32)


def _rms(x, g):
    y = x * lax.rsqrt(jnp.mean(x * x, axis=-1, keepdims=True) + RMS_EPS)
    return y * g


def _cparams(sem, flags=None):
    return pltpu.CompilerParams(dimension_semantics=sem, vmem_limit_bytes=VMEM_LIMIT, flags=flags)


def _proj_kernel(*refs, n_out, chunk):
    x_ref, g_ref, sc_ref = refs[0], refs[1], refs[2]
    w_refs = refs[3:3 + n_out]
    o_refs = refs[3 + n_out:3 + 2 * n_out]
    a = _rms(x_ref[...], g_ref[...]).astype(BF16)
    for k, (w_ref, o_ref) in enumerate(zip(w_refs, o_refs)):
        n = w_ref.shape[1]
        for n0 in range(0, n, chunk):
            n1 = min(n0 + chunk, n)
            y = _dot(a, w_ref[:, n0:n1])
            if k == 0:
                y = y * sc_ref[:, n0:n1]
            o_ref[:, n0:n1] = y.astype(o_ref.dtype)


def _proj(h, gain, scale0, weights, out_dtypes, tm=512):
    t = h.shape[0]
    n_out = len(weights)
    const = lambda a: pl.BlockSpec(a.shape, lambda i: (0, 0))
    gain = gain.reshape(1, D_MODEL)
    in_specs = [pl.BlockSpec((tm, D_MODEL), lambda i: (i, 0)), const(gain), const(scale0)]
    in_specs += [const(w) for w in weights]
    out_specs = [pl.BlockSpec((tm, w.shape[1]), lambda i: (i, 0)) for w in weights]
    out_shape = [jax.ShapeDtypeStruct((t, w.shape[1]), dt) for w, dt in zip(weights, out_dtypes)]
    return pl.pallas_call(
        functools.partial(_proj_kernel, n_out=n_out, chunk=512),
        grid=(t // tm,),
        in_specs=in_specs, out_specs=out_specs, out_shape=out_shape,
        compiler_params=_cparams(("parallel",)),
        name="proj",
    )(h, gain, scale0, *weights)


def _q_scale_row(n_q, n_total):
    q = jnp.full((1, n_q), (HEAD_DIM ** -0.5) * LOG2E, F32)
    return jnp.concatenate([q, jnp.ones((1, n_total - n_q), F32)], axis=1)


def _ffn_kernel(h_ref, o_ref, wo_ref, g_ref, wg_ref, wu_ref, wd_ref, gf_ref, out_ref,
                *, final_norm, chunks):
    h1 = h_ref[...] + _dot(o_ref[...], wo_ref[...])
    a = _rms(h1, g_ref[...]).astype(BF16)
    out_ref[...] = h1
    for c0, c1 in chunks:
        gate = _dot(a, wg_ref[:, c0:c1])
        up = _dot(a, wu_ref[:, c0:c1])
        t = (jax.nn.silu(gate) * up).astype(BF16)
        out_ref[...] += _dot(t, wd_ref[c0:c1, :])
    if final_norm:
        out_ref[...] = _rms(out_ref[...], gf_ref[...])


def _ffn(h, o, wo, gain, wg, wu, wd, final_gain, final_norm, tm=512):
    t = h.shape[0]
    d_ff = wg.shape[1]
    step = 3 * MXU_TILE
    chunks = tuple((c0, min(c0 + step, d_ff)) for c0 in range(0, d_ff, step))
    assert d_ff % MXU_TILE == 0
    gain = gain.reshape(1, D_MODEL)
    final_gain = final_gain.reshape(1, D_MODEL)
    const = lambda a: pl.BlockSpec(a.shape, lambda i: (0, 0), pipeline_mode=pl.Buffered(1))
    return pl.pallas_call(
        functools.partial(_ffn_kernel, final_norm=final_norm, chunks=chunks),
        grid=(t // tm,),
        in_specs=[
            pl.BlockSpec((tm, D_MODEL), lambda i: (i, 0)),
            pl.BlockSpec((tm, o.shape[1]), lambda i: (i, 0)),
            const(wo), const(gain), const(wg), const(wu), const(wd), const(final_gain),
        ],
        out_specs=pl.BlockSpec((tm, D_MODEL), lambda i: (i, 0)),
        out_shape=jax.ShapeDtypeStruct((t, D_MODEL), F32),
        compiler_params=_cparams(("parallel",)),
        name="ffn",
    )(h, o, wo, gain, wg, wu, wd, final_gain)


def _block_max(s, rows):
    if rows is None:
        return jnp.max(s, axis=0, keepdims=True)
    h = s.shape[0] // len(rows)
    ms = [jnp.max(s[a * h:(a + 1) * h], axis=0, keepdims=True) + row for a, row in enumerate(rows)]
    return functools.reduce(jnp.maximum, ms)


def _block_exp2(s, rows, m):
    if rows is None:
        return jnp.exp2(s - m)
    h = s.shape[0] // len(rows)
    return jnp.concatenate([jnp.exp2(s[a * h:(a + 1) * h] - (m - row))
                            for a, row in enumerate(rows)], axis=0)


def _flash_step_multi(kvs, qps, rows, extras, m_scs, acc_scs):
    n = len(kvs)
    ss = []
    for c in range(n):
        s = _dot(kvs[c], qps[c])
        if extras[c] is not None:
            s = s + extras[c]
        ss.append(s)
    m_olds = [m_scs[c][...] for c in range(n)]
    acc_olds = [acc_scs[c][...] for c in range(n)]
    m_news = [jnp.maximum(m_olds[c], _block_max(ss[c], rows[c])) for c in range(n)]
    alphas = [jnp.exp2(m_olds[c] - m_news[c]) for c in range(n)]
    ps = [_block_exp2(ss[c], rows[c], m_news[c]) for c in range(n)]
    pvs = [_pv(kvs[c], ps[c]) for c in range(n)]
    acc_news = [alphas[c] * acc_olds[c] + pvs[c] for c in range(n)]
    for c in range(n):
        m_scs[c][...] = m_news[c]
        acc_scs[c][...] = acc_news[c]


def _flash_first_multi(kvs, ss, rows, m_scs, acc_scs):
    n = len(kvs)
    ms = [jnp.maximum(_block_max(ss[c], rows[c]), M_FLOOR) for c in range(n)]
    ps = [_block_exp2(ss[c], rows[c], ms[c]) for c in range(n)]
    accs = [_pv(kvs[c], ps[c]) for c in range(n)]
    for c in range(n):
        m_scs[c][...] = ms[c]
        acc_scs[c][...] = accs[c]


def _pv(kv, p):
    tk = kv.shape[0]
    lhs = jnp.concatenate([jnp.ones((ONES_ROWS, tk), BF16), kv.T[HEAD_DIM:2 * HEAD_DIM]], axis=0)
    return _dot(lhs, p.astype(BF16))


def _flash_out(acc):
    l = acc[0:1, :]
    return acc[ONES_ROWS:ACC_ROWS, :] / jnp.where(l > 0, l, 1.0)


def _iota(shape, dim):
    return lax.broadcasted_iota(jnp.int32, shape, dim)


def _causal_bias(tk, tq, base=None):
    ok = _iota((tk, tq), 0) <= _iota((tk, tq), 1)
    return jnp.where(ok, 0.0 if base is None else base, MASKED)


def _rank_select(adj, n_rows, topk):
    j = _iota(adj.shape, 0)
    rank = jnp.zeros(adj.shape, jnp.int32)
    for k in range(n_rows):
        rk = adj[k:k + 1, :]
        beats = (rk > adj) | ((rk == adj) & (j > k))
        rank = rank + beats.astype(jnp.int32)
    return rank < topk


def _nsa_cmp_kernel(*refs):
    n_slab = 2 * NSA_KV // LANES
    x_refs = refs[:n_slab]
    w1k_ref, w2k_ref, pk_ref, w1v_ref, w2v_ref, pv_ref, o_ref = refs[n_slab:]
    per_slab = LANES // HEAD_DIM
    nb, _, n_seg, _ = o_ref.shape
    seq = n_seg * CMP_STRIDE
    for t, x_ref in enumerate(x_refs):
        is_v = t >= n_slab // 2
        w1_ref, w2_ref, p_ref = (w1v_ref, w2v_ref, pv_ref) if is_v else (w1k_ref, w2k_ref, pk_ref)
        lane0 = HEAD_DIM if is_v else 0
        x = jnp.concatenate(
            [jnp.concatenate([x_ref[pl.ds(bb * seq + l, n_seg, stride=CMP_STRIDE), :]
                              for l in range(CMP_STRIDE)], axis=1) for bb in range(nb)],
            axis=0).astype(BF16)
        for e in range(per_slab):
            g = (t % (n_slab // 2)) * per_slab + e
            acc = _dot(x, w1_ref[e])
            pp = _dot(p_ref[...], w1_ref[e])
            posb = pp[0:1, :CMP_HIDDEN] + pp[1:2, CMP_HIDDEN:]
            hid = acc[:, :CMP_HIDDEN] + pltpu.roll(acc[:, CMP_HIDDEN:], nb * n_seg - 1, 0) + posb
            y = _dot(jax.nn.gelu(hid).astype(BF16), w2_ref[...])
            for bb in range(nb):
                o_ref[bb, g, :, lane0:lane0 + HEAD_DIM] = (
                    y[bb * n_seg:(bb + 1) * n_seg].astype(o_ref.dtype))


def _nsa_cmp(cmp_in, batch, seq, w1k, w2k, pk, w1v, w2v, pv, nb=4):
    full = lambda a: pl.BlockSpec(a.shape, lambda i: (0,) * a.ndim)
    n_seg = seq // CMP_STRIDE
    n_slab = 2 * NSA_KV // LANES
    assert batch % nb == 0
    slabs = [pl.BlockSpec((nb * seq, LANES), functools.partial(lambda i, t: (i, t), t=t))
             for t in range(n_slab)]
    return pl.pallas_call(
        _nsa_cmp_kernel,
        grid=(batch // nb,),
        in_specs=slabs + [full(w1k), full(w2k), full(pk), full(w1v), full(w2v), full(pv)],
        out_specs=pl.BlockSpec((nb, NSA_GROUPS, n_seg, 2 * HEAD_DIM), lambda i: (i, 0, 0, 0)),
        out_shape=jax.ShapeDtypeStruct((batch, NSA_GROUPS, n_seg, 2 * HEAD_DIM), BF16),
        compiler_params=_cparams(("parallel",)),
        name="nsa_cmp",
    )(*([cmp_in] * n_slab), w1k, w2k, pk, w1v, w2v, pv)


def _nsa_attn_kernel(q_ref, kvs_ref, kvw_ref, kvc_ref, gt_ref, o_ref,
                     m_sc, acc_sc, selb_sc, out_sc, *, seq, ng):
    tq, tk = ATT_TQ, ATT_TK
    i = pl.program_id(2)
    q0 = i * tq
    n_slc = seq // SLC_BLOCK
    n_cmp_pad = seq // CMP_STRIDE
    rep = NSA_REP
    sub = tk // SLC_BLOCK
    win_rows = WINDOW + tq
    groups = range(ng)
    lanes = [slice(g * LANES, (g + 1) * LANES) for g in groups]

    def heads(x):
        return jnp.concatenate([x] * rep, axis=1)

    qt = q_ref[...].astype(F32).T
    zpad = jnp.zeros((HEAD_DIM, tq), F32)
    qps = [jnp.concatenate(
        [jnp.concatenate([qt[(g * rep + r) * HEAD_DIM:(g * rep + r + 1) * HEAD_DIM], zpad], axis=0)
         for r in range(rep)], axis=1).astype(BF16) for g in groups]
    gates = jax.nn.sigmoid(gt_ref[...]).T

    def emit(g, branch, o_t, first):
        for r in range(rep):
            row = g * LANES + branch * rep + r
            val = gates[row:row + 1, :] * o_t[:, r * tq:(r + 1) * tq]
            sl = slice((g * rep + r) * HEAD_DIM, (g * rep + r + 1) * HEAD_DIM)
            if first:
                out_sc[sl, :] = val
            else:
                out_sc[sl, :] += val

    w0 = pl.multiple_of(jnp.maximum(q0 - WINDOW, 0), tk)
    d0 = pl.multiple_of(q0, tk)
    kvcs = [kvc_ref[0, g] for g in groups]
    kvws = [kvw_ref[pl.ds(w0, win_rows), lanes[g]] for g in groups]
    kvds = [kvs_ref[pl.ds(d0, tk), lanes[g]] for g in groups]
    s_cs = [_dot(kvcs[g], qps[g]) for g in groups]
    s_ws = [_dot(kvws[g], qps[g]) for g in groups]
    s_ds = [_dot(kvds[g], qps[g]) for g in groups]

    c_end = _iota((n_cmp_pad, tq), 0) * CMP_STRIDE + (CMP_BLOCK - 1)
    c_ok = (c_end <= q0 + _iota((n_cmp_pad, tq), 1)) & (c_end < seq)
    c_bias = heads(jnp.where(c_ok, 0.0, MASKED))
    k_pos = w0 + _iota((win_rows, tq), 0)
    t_pos = q0 + _iota((win_rows, tq), 1)
    w_bias = heads(jnp.where((k_pos <= t_pos) & (k_pos > t_pos - WINDOW), 0.0, MASKED))
    causal = _causal_bias(tk, tq)

    jj = _iota((n_slc, n_cmp_pad), 0) * SLC_BLOCK
    cc = _iota((n_slc, n_cmp_pad), 1) * CMP_STRIDE
    overlap_t = jnp.where((cc < jj + SLC_BLOCK) & (cc + CMP_BLOCK > jj) & (cc + CMP_BLOCK <= seq),
                          1.0, 0.0).astype(BF16)
    j_blk = _iota((n_slc, tq), 0)
    cur = (q0 + _iota((n_slc, tq), 1)) // SLC_BLOCK
    forced = (j_blk == 0) | (j_blk == cur) | (j_blk == cur - 1)
    valid = j_blk <= cur

    ss = [s_cs[g] + c_bias for g in groups]
    ms = [jnp.maximum(jnp.max(s, axis=0, keepdims=True), M_FLOOR) for s in ss]
    es = [jnp.exp2(ss[g] - ms[g]) for g in groups]
    ds = [jnp.sum(e, axis=0, keepdims=True) for e in es]
    p_cs = [es[g] / jnp.where(ds[g] > 0, ds[g], 1.0) for g in groups]
    o_cs = [_pv(kvcs[g], p_cs[g])[ONES_ROWS:ACC_ROWS, :] for g in groups]

    ss = [s_ws[g] + w_bias for g in groups]
    ms = [jnp.max(s, axis=0, keepdims=True) for s in ss]
    o_ws = [_flash_out(_pv(kvws[g], jnp.exp2(ss[g] - ms[g]))) for g in groups]

    selbs = []
    for g in groups:
        p_sum = p_cs[g][:, 0:tq]
        for r in range(1, rep):
            p_sum = p_sum + p_cs[g][:, r * tq:(r + 1) * tq]
        p_hi = p_sum.astype(BF16)
        p_lo = (p_sum - p_hi.astype(F32)).astype(BF16)
        imp = _dot(overlap_t, p_hi) + _dot(overlap_t, p_lo)
        adj = jnp.where(forced, jnp.inf, jnp.where(valid, imp, -jnp.inf))
        sel = valid & _rank_select(adj, n_slc, min(SLC_TOPK, n_slc))
        selbs.append(jnp.where(sel, 0.0, MASKED))

    for g in groups:
        selb_sc[g] = selbs[g]
        emit(g, 0, o_cs[g], True)
        emit(g, 2, o_ws[g], False)

    def sel_rows(g, blk0):
        return [heads(selb_sc[g, pl.ds(blk0 + a, 1), :]) for a in range(sub)]

    m_scs = [m_sc.at[g] for g in groups]
    acc_scs = [acc_sc.at[g] for g in groups]

    causal_h = heads(causal)
    _flash_first_multi(kvds, [s_ds[g] + causal_h for g in groups],
                       [sel_rows(g, i * sub) for g in groups], m_scs, acc_scs)

    def slc_body(kt, carry):
        k0 = pl.multiple_of(kt * tk, tk)
        _flash_step_multi([kvs_ref[pl.ds(k0, tk), lanes[g]] for g in groups], qps,
                          [sel_rows(g, kt * sub) for g in groups], [None] * ng, m_scs, acc_scs)
        return carry

    lax.fori_loop(0, i, slc_body, 0)
    for g in groups:
        emit(g, 1, _flash_out(acc_scs[g][...]), False)

    o_ref[...] = out_sc[...].T.astype(o_ref.dtype)


def _nsa_attn(main, kvc, gates, batch, seq, ng=4):
    tq = ATT_TQ
    assert ATT_TK == tq and WINDOW % tq == 0 and seq % tq == 0 and seq >= WINDOW + tq
    assert NSA_GROUPS % ng == 0 and tq % SLC_BLOCK == 0
    nq = seq // tq
    t = batch * seq
    gw = ng * NSA_REP * HEAD_DIM
    kvw = ng * LANES
    kv0 = NSA_HEADS * HEAD_DIM // kvw
    m = NSA_REP * tq
    n_cmp_pad = seq // CMP_STRIDE
    return pl.pallas_call(
        functools.partial(_nsa_attn_kernel, seq=seq, ng=ng),
        grid=(batch, NSA_GROUPS // ng, nq),
        in_specs=[
            pl.BlockSpec((tq, gw), lambda b, g, i: (b * nq + i, g)),
            pl.BlockSpec((seq, kvw), lambda b, g, i: (b, kv0 + g)),
            pl.BlockSpec((seq, kvw), lambda b, g, i: (b, kv0 + NSA_GROUPS // ng + g)),
            pl.BlockSpec((1, ng, n_cmp_pad, LANES), lambda b, g, i: (b, g, 0, 0)),
            pl.BlockSpec((tq, kvw), lambda b, g, i: (b * nq + i, g)),
        ],
        out_specs=pl.BlockSpec((tq, gw), lambda b, g, i: (b * nq + i, g)),
        out_shape=jax.ShapeDtypeStruct((t, NSA_HEADS * HEAD_DIM), BF16),
        scratch_shapes=[pltpu.VMEM((ng, 1, m), F32),
                        pltpu.VMEM((ng, ACC_ROWS, m), F32),
                        pltpu.VMEM((ng, seq // SLC_BLOCK, tq), F32),
                        pltpu.VMEM((gw, tq), F32)],
        compiler_params=_cparams(("parallel", "parallel", "arbitrary")),
        name="nsa_attn",
    )(main, main, main, kvc, gates)


def _interleave_heads(wa, wb, n):
    d = wa.shape[0]
    return jnp.stack([wa.reshape(d, n, HEAD_DIM), wb.reshape(d, n, HEAD_DIM)],
                     axis=2).reshape(d, 2 * n * HEAD_DIM)


def _nsa_mixer(h, gain, w_in, pos_k, pos_v, k_w1, k_w2, v_w1, v_w2, batch, seq):
    nh = NSA_HEADS * HEAD_DIM
    c = [nh + i * NSA_KV for i in range(7)]
    wq = w_in[:, :c[0]]
    wkc, wvc, wks, wvs, wkw, wvw = (w_in[:, c[i]:c[i + 1]] for i in range(6))
    w_gate = w_in[:, c[6]:]
    w_main = jnp.concatenate(
        [wq, _interleave_heads(wks, wvs, NSA_GROUPS), _interleave_heads(wkw, wvw, NSA_GROUPS)],
        axis=1).astype(BF16)
    w_cmp = jnp.concatenate([wkc, wvc], axis=1).astype(BF16)
    wg = w_gate.reshape(D_MODEL, 3, NSA_GROUPS, NSA_REP).transpose(0, 2, 1, 3)
    wg = wg.reshape(D_MODEL, NSA_GROUPS, 3 * NSA_REP)
    wg = jnp.pad(wg, ((0, 0), (0, 0), (0, LANES - 3 * NSA_REP)))
    wg = wg.reshape(D_MODEL, NSA_GROUPS * LANES).astype(BF16)

    main, gates, cmp_in = _proj(h, gain, _q_scale_row(nh, w_main.shape[1]),
                                [w_main, wg, w_cmp], [BF16, F32, F32])

    half = CMP_STRIDE * HEAD_DIM
    per_slab = LANES // HEAD_DIM

    def seg_weights(w1):
        w = jnp.concatenate([w1[:half], w1[half:]], axis=1).reshape(CMP_STRIDE, HEAD_DIM, -1)
        z = jnp.zeros_like(w)
        per_group = [jnp.concatenate([w if e == k else z for k in range(per_slab)], axis=1)
                     for e in range(per_slab)]
        return jnp.stack(per_group).reshape(per_slab, CMP_STRIDE * LANES, -1).astype(BF16)

    def pos_rows(p):
        rows = [jnp.tile(p[k * CMP_STRIDE:(k + 1) * CMP_STRIDE], (1, per_slab)).reshape(1, -1)
                for k in range(2)]
        return jnp.pad(jnp.concatenate(rows, axis=0), ((0, 6), (0, 0))).astype(BF16)

    kvc = _nsa_cmp(cmp_in, batch, seq, seg_weights(k_w1), k_w2.astype(BF16), pos_rows(pos_k),
                   seg_weights(v_w1), v_w2.astype(BF16), pos_rows(pos_v))
    return _nsa_attn(main, kvc, gates, batch, seq)


def _moba_attn_kernel(q_ref, kv_ref, o_ref, m_sc, acc_sc, selb_sc, kmean_sc, *, seq, nh):
    tq, tk = ATT_TQ, ATT_TK
    i = pl.program_id(2)
    q0 = i * tq
    n_blk = seq // MOBA_BLOCK
    heads = range(nh)
    lanes = [slice(h * LANES, (h + 1) * LANES) for h in heads]

    @pl.when(i == 0)
    def _():
        for h in heads:
            for j in range(n_blk):
                blk = kv_ref[j * MOBA_BLOCK:(j + 1) * MOBA_BLOCK, lanes[h]].astype(F32)
                kmean_sc[h, j:j + 1, :] = jnp.sum(blk, axis=0, keepdims=True) * (1.0 / MOBA_BLOCK)

    qt = q_ref[...].astype(F32).T
    zpad = jnp.zeros((HEAD_DIM, tq), F32)
    qps = [jnp.concatenate([qt[h * HEAD_DIM:(h + 1) * HEAD_DIM], zpad], axis=0).astype(BF16)
           for h in heads]
    d0 = pl.multiple_of(q0, tk)
    kvds = [kv_ref[pl.ds(d0, tk), lanes[h]] for h in heads]
    s_ds = [_dot(kvds[h], qps[h]) for h in heads]
    past = _iota((n_blk, tq), 0) < i
    for h in heads:
        score = _dot(kmean_sc[h].astype(BF16), qps[h])
        adj = jnp.where(past, score, -jnp.inf)
        sel = past & _rank_select(adj, n_blk, MOBA_TOPK)
        selb_sc[h] = jnp.where(sel, 0.0, MASKED)

    m_scs = [m_sc.at[h] for h in heads]
    acc_scs = [acc_sc.at[h] for h in heads]

    causal = _causal_bias(tk, tq)
    _flash_first_multi(kvds, [s_ds[h] + causal for h in heads], [None] * nh, m_scs, acc_scs)

    def body(kt, carry):
        k0 = pl.multiple_of(kt * tk, tk)
        _flash_step_multi([kv_ref[pl.ds(k0, tk), lanes[h]] for h in heads], qps,
                          [[selb_sc[h, pl.ds(kt, 1), :]] for h in heads], [None] * nh,
                          m_scs, acc_scs)
        return carry

    lax.fori_loop(0, i, body, 0)
    outs = [_flash_out(acc_scs[h][...]) for h in heads]
    o_ref[...] = jnp.concatenate(outs, axis=0).T.astype(o_ref.dtype)


def _moba_attn(main, batch, seq, nh=8):
    tq = ATT_TQ
    assert ATT_TK == tq and MOBA_BLOCK == tq and seq % tq == 0 and MOBA_HEADS % nh == 0
    nq = seq // tq
    t = batch * seq
    d_heads = MOBA_HEADS * HEAD_DIM
    n_blk = seq // MOBA_BLOCK
    qw = nh * HEAD_DIM
    kvw = nh * LANES
    return pl.pallas_call(
        functools.partial(_moba_attn_kernel, seq=seq, nh=nh),
        grid=(batch, MOBA_HEADS // nh, nq),
        in_specs=[
            pl.BlockSpec((tq, qw), lambda b, hg, i: (b * nq + i, hg)),
            pl.BlockSpec((seq, kvw), lambda b, hg, i: (b, d_heads // kvw + hg)),
        ],
        out_specs=pl.BlockSpec((tq, qw), lambda b, hg, i: (b * nq + i, hg)),
        out_shape=jax.ShapeDtypeStruct((t, d_heads), BF16),
        scratch_shapes=[pltpu.VMEM((nh, 1, tq), F32),
                        pltpu.VMEM((nh, ACC_ROWS, tq), F32),
                        pltpu.VMEM((nh, n_blk, tq), F32),
                        pltpu.VMEM((nh, n_blk, LANES), F32)],
        compiler_params=_cparams(("parallel", "parallel", "arbitrary")),
        name="moba_attn",
    )(main, main)


def _moba_mixer(h, gain, w_in, batch, seq):
    nh = MOBA_HEADS * HEAD_DIM
    wq = w_in[:, :nh]
    wk = w_in[:, nh:2 * nh]
    wv = w_in[:, 2 * nh:]
    w_main = jnp.concatenate([wq, _interleave_heads(wk, wv, MOBA_HEADS)], axis=1).astype(BF16)
    (main,) = _proj(h, gain, _q_scale_row(nh, w_main.shape[1]), [w_main], [BF16])
    return _moba_attn(main, batch, seq)


def _conv_kernel(h_ref, hp_ref, g_ref, win_ref, cw_ref, o_ref):
    i = pl.program_id(1)
    d = D_MODEL
    g = g_ref[...]
    a = _rms(h_ref[...], g).astype(BF16)
    ap = _rms(hp_ref[...], g).astype(BF16)
    b_gate = _dot(a, win_ref[:, 0:d])
    z = _dot(a, win_ref[:, d:2 * d]) * _dot(a, win_ref[:, 2 * d:3 * d])
    zp = _dot(ap, win_ref[:, d:2 * d]) * _dot(ap, win_ref[:, 2 * d:3 * d])
    zp = jnp.where(i == 0, 0.0, zp)
    row = _iota(z.shape, 0)
    z1 = jnp.where(row == 0, zp[7:8], pltpu.roll(z, 1, 0))
    z2 = jnp.where(row == 0, zp[6:7], jnp.where(row == 1, zp[7:8], pltpu.roll(z, 2, 0)))
    y = cw_ref[0:1, :] * z2 + cw_ref[1:2, :] * z1 + cw_ref[2:3, :] * z
    o_ref[...] = (b_gate * y).astype(o_ref.dtype)


def _conv_mixer(h, gain, w_in, conv_w, batch, seq, ts=512):
    t = batch * seq
    ns = seq // ts
    halo = 8
    cw = jnp.pad(conv_w, ((0, 8 - CONV_WIDTH), (0, 0)))
    return pl.pallas_call(
        _conv_kernel,
        grid=(batch, ns),
        in_specs=[
            pl.BlockSpec((ts, D_MODEL), lambda b, i: (b * ns + i, 0)),
            pl.BlockSpec((halo, D_MODEL),
                         lambda b, i: (jnp.maximum((b * ns + i) * (ts // halo) - 1, 0), 0)),
            pl.BlockSpec((1, D_MODEL), lambda b, i: (0, 0)),
            pl.BlockSpec(w_in.shape, lambda b, i: (0, 0)),
            pl.BlockSpec(cw.shape, lambda b, i: (0, 0)),
        ],
        out_specs=pl.BlockSpec((ts, D_MODEL), lambda b, i: (b * ns + i, 0)),
        out_shape=jax.ShapeDtypeStruct((t, D_MODEL), BF16),
        compiler_params=_cparams(("parallel", "arbitrary")),
        name="conv_mix",
    )(h, h, gain.reshape(1, D_MODEL), w_in.astype(BF16), cw)


def kernel(x, norm_mix, norm_ffn, norm_final, ffn_w_gate, ffn_w_up, ffn_w_down, nsa_w_in, nsa_w_out, nsa_cmp_pos_k, nsa_cmp_pos_v, nsa_cmp_k_w1, nsa_cmp_k_w2, nsa_cmp_v_w1, nsa_cmp_v_w2, moba_w_in, moba_w_out, conv_w_in, conv_w, conv_w_out):
    batch, seq, d = x.shape
    depth = norm_mix.shape[0]
    h = x.reshape(batch * seq, d)
    for i in range(depth):
        kind, j = i % N_MIXERS, i // N_MIXERS
        if kind == 0:
            o = _nsa_mixer(h, norm_mix[i], nsa_w_in[j], nsa_cmp_pos_k[j], nsa_cmp_pos_v[j],
                           nsa_cmp_k_w1[j], nsa_cmp_k_w2[j], nsa_cmp_v_w1[j], nsa_cmp_v_w2[j],
                           batch, seq)
            wo = nsa_w_out[j]
        elif kind == 1:
            o = _moba_mixer(h, norm_mix[i], moba_w_in[j], batch, seq)
            wo = moba_w_out[j]
        else:
            o = _conv_mixer(h, norm_mix[i], conv_w_in[j], conv_w[j], batch, seq)
            wo = conv_w_out[j]
        h = _ffn(h, o, wo.astype(BF16), norm_ffn[i], ffn_w_gate[i].astype(BF16),
                 ffn_w_up[i].astype(BF16), ffn_w_down[i].astype(BF16), norm_final,
                 final_norm=(i == depth - 1))
    return h.reshape(batch, seq, d)
```

```python
import functools

import jax
import jax.numpy as jnp
from jax import lax
from jax.experimental import pallas as pl
from jax.experimental.pallas import tpu as pltpu

F32 = jnp.float32
BF16 = jnp.bfloat16

D_MODEL = 1024
HEAD_DIM = 64
RMS_EPS = 1e-6
N_MIXERS = 3

NSA_HEADS = 16
NSA_GROUPS = 4
NSA_REP = NSA_HEADS // NSA_GROUPS
NSA_KV = NSA_GROUPS * HEAD_DIM
CMP_BLOCK = 32
CMP_STRIDE = 16
CMP_HIDDEN = 256
SLC_BLOCK = 64
SLC_TOPK = 16
WINDOW = 512

MOBA_HEADS = 16
MOBA_BLOCK = 256
MOBA_TOPK = 3

CONV_WIDTH = 3

LANES = 128
SUBLANES = 8
MXU_TILE = 256
VMEM_LIMIT = 56 * 1024 * 1024
MASKED = -1e30
M_FLOOR = -1e29
LOG2E = 1.4426950408889634

ONES_ROWS = 16
ACC_ROWS = ONES_ROWS + HEAD_DIM

ATT_TQ = 256
ATT_TK = 256


def _dot(a, b):
    return jnp.dot(a, b, preferred_element_type=F32)


def _rms(x, g):
    y = x * lax.rsqrt(jnp.mean(x * x, axis=-1, keepdims=True) + RMS_EPS)
    return y * g


def _cparams(sem, flags=None):
    return pltpu.CompilerParams(dimension_semantics=sem, vmem_limit_bytes=VMEM_LIMIT, flags=flags)


def _proj_kernel(*refs, n_out, chunk):
    x_ref, g_ref, sc_ref = refs[0], refs[1], refs[2]
    w_refs = refs[3:3 + n_out]
    o_refs = refs[3 + n_out:3 + 2 * n_out]
    a = _rms(x_ref[...], g_ref[...]).astype(BF16)
    for k, (w_ref, o_ref) in enumerate(zip(w_refs, o_refs)):
        n = w_ref.shape[1]
        for n0 in range(0, n, chunk):
            n1 = min(n0 + chunk, n)
            y = _dot(a, w_ref[:, n0:n1])
            if k == 0:
                y = y * sc_ref[:, n0:n1]
            o_ref[:, n0:n1] = y.astype(o_ref.dtype)


def _proj(h, gain, scale0, weights, out_dtypes, tm=512):
    t = h.shape[0]
    n_out = len(weights)
    const = lambda a: pl.BlockSpec(a.shape, lambda i: (0, 0))
    gain = gain.reshape(1, D_MODEL)
    in_specs = [pl.BlockSpec((tm, D_MODEL), lambda i: (i, 0)), const(gain), const(scale0)]
    in_specs += [const(w) for w in weights]
    out_specs = [pl.BlockSpec((tm, w.shape[1]), lambda i: (i, 0)) for w in weights]
    out_shape = [jax.ShapeDtypeStruct((t, w.shape[1]), dt) for w, dt in zip(weights, out_dtypes)]
    return pl.pallas_call(
        functools.partial(_proj_kernel, n_out=n_out, chunk=512),
        grid=(t // tm,),
        in_specs=in_specs, out_specs=out_specs, out_shape=out_shape,
        compiler_params=_cparams(("parallel",)),
        name="proj",
    )(h, gain, scale0, *weights)


def _q_scale_row(n_q, n_total):
    q = jnp.full((1, n_q), (HEAD_DIM ** -0.5) * LOG2E, F32)
    return jnp.concatenate([q, jnp.ones((1, n_total - n_q), F32)], axis=1)


def _ffn_kernel(h_ref, o_ref, wo_ref, g_ref, wg_ref, wu_ref, wd_ref, gf_ref, out_ref,
                *, final_norm, chunks):
    h1 = h_ref[...] + _dot(o_ref[...], wo_ref[...])
    a = _rms(h1, g_ref[...]).astype(BF16)
    out_ref[...] = h1
    for c0, c1 in chunks:
        gate = _dot(a, wg_ref[:, c0:c1])
        up = _dot(a, wu_ref[:, c0:c1])
        t = (jax.nn.silu(gate) * up).astype(BF16)
        out_ref[...] += _dot(t, wd_ref[c0:c1, :])
    if final_norm:
        out_ref[...] = _rms(out_ref[...], gf_ref[...])


def _ffn(h, o, wo, gain, wg, wu, wd, final_gain, final_norm, tm=512):
    t = h.shape[0]
    d_ff = wg.shape[1]
    step = 3 * MXU_TILE
    chunks = tuple((c0, min(c0 + step, d_ff)) for c0 in range(0, d_ff, step))
    assert d_ff % MXU_TILE == 0
    gain = gain.reshape(1, D_MODEL)
    final_gain = final_gain.reshape(1, D_MODEL)
    const = lambda a: pl.BlockSpec(a.shape, lambda i: (0, 0), pipeline_mode=pl.Buffered(1))
    return pl.pallas_call(
        functools.partial(_ffn_kernel, final_norm=final_norm, chunks=chunks),
        grid=(t // tm,),
        in_specs=[
            pl.BlockSpec((tm, D_MODEL), lambda i: (i, 0)),
            pl.BlockSpec((tm, o.shape[1]), lambda i: (i, 0)),
            const(wo), const(gain), const(wg), const(wu), const(wd), const(final_gain),
        ],
        out_specs=pl.BlockSpec((tm, D_MODEL), lambda i: (i, 0)),
        out_shape=jax.ShapeDtypeStruct((t, D_MODEL), F32),
        compiler_params=_cparams(("parallel",)),
        name="ffn",
    )(h, o, wo, gain, wg, wu, wd, final_gain)


def _block_max(s, rows):
    if rows is None:
        return jnp.max(s, axis=0, keepdims=True)
    h = s.shape[0] // len(rows)
    ms = [jnp.max(s[a * h:(a + 1) * h].reshape(h // SUBLANES, SUBLANES, s.shape[1]), axis=0) + row
          for a, row in enumerate(rows)]
    return jnp.max(functools.reduce(jnp.maximum, ms), axis=0, keepdims=True)


def _block_exp2(s, rows, m):
    if rows is None:
        return jnp.exp2(s - m)
    h = s.shape[0] // len(rows)
    return jnp.concatenate([jnp.exp2(s[a * h:(a + 1) * h] - (m - row))
                            for a, row in enumerate(rows)], axis=0)


def _flash_step_multi(kvs, qps, rows, m_scs, acc_scs):
    n = len(kvs)
    ss = [_dot(kvs[c], qps[c]) for c in range(n)]
    m_olds = [m_scs[c][...] for c in range(n)]
    acc_olds = [acc_scs[c][...] for c in range(n)]
    m_news = [jnp.maximum(m_olds[c], _block_max(ss[c], rows[c])) for c in range(n)]
    alphas = [jnp.exp2(m_olds[c] - m_news[c]) for c in range(n)]
    ps = [_block_exp2(ss[c], rows[c], m_news[c]) for c in range(n)]
    pvs = [_pv(kvs[c], ps[c]) for c in range(n)]
    acc_news = [alphas[c] * acc_olds[c] + pvs[c] for c in range(n)]
    for c in range(n):
        m_scs[c][...] = m_news[c]
        acc_scs[c][...] = acc_news[c]


def _flash_first_multi(kvs, ss, rows, m_scs, acc_scs):
    n = len(kvs)
    ms = [jnp.maximum(_block_max(ss[c], rows[c]), M_FLOOR) for c in range(n)]
    ps = [_block_exp2(ss[c], rows[c], ms[c]) for c in range(n)]
    accs = [_pv(kvs[c], ps[c]) for c in range(n)]
    for c in range(n):
        m_scs[c][...] = ms[c]
        acc_scs[c][...] = accs[c]


def _pv(kv, p):
    tk = kv.shape[0]
    lhs = jnp.concatenate([jnp.ones((ONES_ROWS, tk), BF16), kv.T[HEAD_DIM:2 * HEAD_DIM]], axis=0)
    return _dot(lhs, p.astype(BF16))


def _flash_out(acc):
    l = acc[0:1, :]
    return acc[ONES_ROWS:ACC_ROWS, :] / jnp.where(l > 0, l, 1.0)


def _iota(shape, dim):
    return lax.broadcasted_iota(jnp.int32, shape, dim)


def _causal_bias(tk, tq, base=None):
    ok = _iota((tk, tq), 0) <= _iota((tk, tq), 1)
    return jnp.where(ok, 0.0 if base is None else base, MASKED)


def _rank_select(adj, n_rows, topk):
    j = _iota(adj.shape, 0)
    rank = jnp.zeros(adj.shape, jnp.int32)
    for k in range(n_rows):
        rk = adj[k:k + 1, :]
        beats = (rk > adj) | ((rk == adj) & (j > k))
        rank = rank + beats.astype(jnp.int32)
    return rank < topk


def _nsa_cmp_kernel(*refs):
    n_slab = 2 * NSA_KV // LANES
    x_refs = refs[:n_slab]
    w1k_ref, w2k_ref, pk_ref, w1v_ref, w2v_ref, pv_ref, o_ref = refs[n_slab:]
    per_slab = LANES // HEAD_DIM
    nb, _, n_seg, _ = o_ref.shape
    seq = n_seg * CMP_STRIDE
    for t, x_ref in enumerate(x_refs):
        is_v = t >= n_slab // 2
        w1_ref, w2_ref, p_ref = (w1v_ref, w2v_ref, pv_ref) if is_v else (w1k_ref, w2k_ref, pk_ref)
        lane0 = HEAD_DIM if is_v else 0
        x = jnp.concatenate(
            [jnp.concatenate([x_ref[pl.ds(bb * seq + l, n_seg, stride=CMP_STRIDE), :]
                              for l in range(CMP_STRIDE)], axis=1) for bb in range(nb)],
            axis=0).astype(BF16)
        for e in range(per_slab):
            g = (t % (n_slab // 2)) * per_slab + e
            acc = _dot(x, w1_ref[e])
            pp = _dot(p_ref[...], w1_ref[e])
            posb = pp[0:1, :CMP_HIDDEN] + pp[1:2, CMP_HIDDEN:]
            hid = acc[:, :CMP_HIDDEN] + pltpu.roll(acc[:, CMP_HIDDEN:], nb * n_seg - 1, 0) + posb
            y = _dot(jax.nn.gelu(hid).astype(BF16), w2_ref[...])
            for bb in range(nb):
                o_ref[bb, g, :, lane0:lane0 + HEAD_DIM] = (
                    y[bb * n_seg:(bb + 1) * n_seg].astype(o_ref.dtype))


def _nsa_cmp(cmp_in, batch, seq, w1k, w2k, pk, w1v, w2v, pv):
    full = lambda a: pl.BlockSpec(a.shape, lambda i: (0,) * a.ndim)
    n_seg = seq // CMP_STRIDE
    n_slab = 2 * NSA_KV // LANES
    nb = max(d for d in (1, 2, 4) if batch % d == 0)
    slabs = [pl.BlockSpec((nb * seq, LANES), functools.partial(lambda i, t: (i, t), t=t))
             for t in range(n_slab)]
    return pl.pallas_call(
        _nsa_cmp_kernel,
        grid=(batch // nb,),
        in_specs=slabs + [full(w1k), full(w2k), full(pk), full(w1v), full(w2v), full(pv)],
        out_specs=pl.BlockSpec((nb, NSA_GROUPS, n_seg, 2 * HEAD_DIM), lambda i: (i, 0, 0, 0)),
        out_shape=jax.ShapeDtypeStruct((batch, NSA_GROUPS, n_seg, 2 * HEAD_DIM), BF16),
        compiler_params=_cparams(("parallel",)),
        name="nsa_cmp",
    )(*([cmp_in] * n_slab), w1k, w2k, pk, w1v, w2v, pv)


def _nsa_attn_kernel(q_ref, kvs_ref, kvw_ref, kvc_ref, gt_ref, o_ref,
                     m_sc, acc_sc, selb_sc, out_sc, *, seq, ng):
    tq, tk = ATT_TQ, ATT_TK
    i = pl.program_id(2)
    q0 = i * tq
    n_slc = seq // SLC_BLOCK
    n_cmp_pad = seq // CMP_STRIDE
    rep = NSA_REP
    sub = tk // SLC_BLOCK
    win_rows = WINDOW + tq
    groups = range(ng)
    lanes = [slice(g * LANES, (g + 1) * LANES) for g in groups]

    def heads(x):
        return jnp.concatenate([x] * rep, axis=1)

    qt = q_ref[...].astype(F32).T
    zpad = jnp.zeros((HEAD_DIM, tq), F32)
    qps = [jnp.concatenate(
        [jnp.concatenate([qt[(g * rep + r) * HEAD_DIM:(g * rep + r + 1) * HEAD_DIM], zpad], axis=0)
         for r in range(rep)], axis=1).astype(BF16) for g in groups]
    gates = jax.nn.sigmoid(gt_ref[...]).T

    def emit(g, branch, o_t, first):
        for r in range(rep):
            row = g * LANES + branch * rep + r
            val = gates[row:row + 1, :] * o_t[:, r * tq:(r + 1) * tq]
            sl = slice((g * rep + r) * HEAD_DIM, (g * rep + r + 1) * HEAD_DIM)
            if first:
                out_sc[sl, :] = val
            else:
                out_sc[sl, :] += val

    w0 = pl.multiple_of(jnp.maximum(q0 - WINDOW, 0), tk)
    d0 = pl.multiple_of(q0, tk)
    kvcs = [kvc_ref[0, g] for g in groups]
    kvws = [kvw_ref[pl.ds(w0, win_rows), lanes[g]] for g in groups]
    kvds = [kvs_ref[pl.ds(d0, tk), lanes[g]] for g in groups]
    s_cs = [_dot(kvcs[g], qps[g]) for g in groups]
    s_ws = [_dot(kvws[g], qps[g]) for g in groups]
    s_ds = [_dot(kvds[g], qps[g]) for g in groups]

    c_end = _iota((n_cmp_pad, tq), 0) * CMP_STRIDE + (CMP_BLOCK - 1)
    c_ok = (c_end <= q0 + _iota((n_cmp_pad, tq), 1)) & (c_end < seq)
    c_bias = heads(jnp.where(c_ok, 0.0, MASKED))
    k_pos = w0 + _iota((win_rows, tq), 0)
    t_pos = q0 + _iota((win_rows, tq), 1)
    w_bias = heads(jnp.where((k_pos <= t_pos) & (k_pos > t_pos - WINDOW), 0.0, MASKED))
    causal = _causal_bias(tk, tq)

    jj = _iota((n_slc, n_cmp_pad), 0) * SLC_BLOCK
    cc = _iota((n_slc, n_cmp_pad), 1) * CMP_STRIDE
    overlap_t = jnp.where((cc < jj + SLC_BLOCK) & (cc + CMP_BLOCK > jj) & (cc + CMP_BLOCK <= seq),
                          1.0, 0.0).astype(BF16)
    j_blk = _iota((n_slc, tq), 0)
    cur = (q0 + _iota((n_slc, tq), 1)) // SLC_BLOCK
    forced = (j_blk == 0) | (j_blk == cur) | (j_blk == cur - 1)
    valid = j_blk <= cur

    ss = [s_cs[g] + c_bias for g in groups]
    ms = [jnp.maximum(jnp.max(s, axis=0, keepdims=True), M_FLOOR) for s in ss]
    es = [jnp.exp2(ss[g] - ms[g]) for g in groups]
    ds = [jnp.sum(e, axis=0, keepdims=True) for e in es]
    p_cs = [es[g] / jnp.where(ds[g] > 0, ds[g], 1.0) for g in groups]
    o_cs = [_pv(kvcs[g], p_cs[g])[ONES_ROWS:ACC_ROWS, :] for g in groups]

    ss = [s_ws[g] + w_bias for g in groups]
    ms = [jnp.max(s, axis=0, keepdims=True) for s in ss]
    o_ws = [_flash_out(_pv(kvws[g], jnp.exp2(ss[g] - ms[g]))) for g in groups]

    selbs = []
    for g in groups:
        p_sum = p_cs[g][:, 0:tq]
        for r in range(1, rep):
            p_sum = p_sum + p_cs[g][:, r * tq:(r + 1) * tq]
        p_hi = p_sum.astype(BF16)
        p_lo = (p_sum - p_hi.astype(F32)).astype(BF16)
        imp = _dot(overlap_t, p_hi) + _dot(overlap_t, p_lo)
        adj = jnp.where(forced, jnp.inf, jnp.where(valid, imp, -jnp.inf))
        sel = valid & _rank_select(adj, n_slc, min(SLC_TOPK, n_slc))
        selbs.append(jnp.where(sel, 0.0, MASKED))

    for g in groups:
        selb_sc[g] = selbs[g]
        emit(g, 0, o_cs[g], True)
        emit(g, 2, o_ws[g], False)

    def sel_rows(g, blk0):
        return [heads(selb_sc[g, pl.ds(blk0 + a, 1), :]) for a in range(sub)]

    m_scs = [m_sc.at[g] for g in groups]
    acc_scs = [acc_sc.at[g] for g in groups]

    causal_h = heads(causal)
    _flash_first_multi(kvds, [s_ds[g] + causal_h for g in groups],
                       [sel_rows(g, i * sub) for g in groups], m_scs, acc_scs)

    def slc_body(kt, carry):
        k0 = pl.multiple_of(kt * tk, tk)
        _flash_step_multi([kvs_ref[pl.ds(k0, tk), lanes[g]] for g in groups], qps,
                          [sel_rows(g, kt * sub) for g in groups], m_scs, acc_scs)
        return carry

    lax.fori_loop(0, i, slc_body, 0)
    for g in groups:
        emit(g, 1, _flash_out(acc_scs[g][...]), False)

    o_ref[...] = out_sc[...].T.astype(o_ref.dtype)


def _nsa_attn(main, kvc, gates, batch, seq, ng=4):
    tq = ATT_TQ
    assert ATT_TK == tq and WINDOW % tq == 0 and seq % tq == 0 and seq >= WINDOW + tq
    assert NSA_GROUPS % ng == 0 and tq % SLC_BLOCK == 0
    nq = seq // tq
    t = batch * seq
    gw = ng * NSA_REP * HEAD_DIM
    kvw = ng * LANES
    assert (NSA_HEADS * HEAD_DIM) % kvw == 0
    kv0 = NSA_HEADS * HEAD_DIM // kvw
    m = NSA_REP * tq
    n_cmp_pad = seq // CMP_STRIDE
    return pl.pallas_call(
        functools.partial(_nsa_attn_kernel, seq=seq, ng=ng),
        grid=(batch, NSA_GROUPS // ng, nq),
        in_specs=[
            pl.BlockSpec((tq, gw), lambda b, g, i: (b * nq + i, g)),
            pl.BlockSpec((seq, kvw), lambda b, g, i: (b, kv0 + g)),
            pl.BlockSpec((seq, kvw), lambda b, g, i: (b, kv0 + NSA_GROUPS // ng + g)),
            pl.BlockSpec((1, ng, n_cmp_pad, LANES), lambda b, g, i: (b, g, 0, 0)),
            pl.BlockSpec((tq, kvw), lambda b, g, i: (b * nq + i, g)),
        ],
        out_specs=pl.BlockSpec((tq, gw), lambda b, g, i: (b * nq + i, g)),
        out_shape=jax.ShapeDtypeStruct((t, NSA_HEADS * HEAD_DIM), BF16),
        scratch_shapes=[pltpu.VMEM((ng, 1, m), F32),
                        pltpu.VMEM((ng, ACC_ROWS, m), F32),
                        pltpu.VMEM((ng, seq // SLC_BLOCK, tq), F32),
                        pltpu.VMEM((gw, tq), F32)],
        compiler_params=_cparams(("parallel", "parallel", "arbitrary")),
        name="nsa_attn",
    )(main, main, main, kvc, gates)


def _interleave_heads(wa, wb, n):
    d = wa.shape[0]
    return jnp.stack([wa.reshape(d, n, HEAD_DIM), wb.reshape(d, n, HEAD_DIM)],
                     axis=2).reshape(d, 2 * n * HEAD_DIM)


def _nsa_mixer(h, gain, w_in, pos_k, pos_v, k_w1, k_w2, v_w1, v_w2, batch, seq):
    nh = NSA_HEADS * HEAD_DIM
    c = [nh + i * NSA_KV for i in range(7)]
    wq = w_in[:, :c[0]]
    wkc, wvc, wks, wvs, wkw, wvw = (w_in[:, c[i]:c[i + 1]] for i in range(6))
    w_gate = w_in[:, c[6]:]
    w_main = jnp.concatenate(
        [wq, _interleave_heads(wks, wvs, NSA_GROUPS), _interleave_heads(wkw, wvw, NSA_GROUPS)],
        axis=1).astype(BF16)
    w_cmp = jnp.concatenate([wkc, wvc], axis=1).astype(BF16)
    wg = w_gate.reshape(D_MODEL, 3, NSA_GROUPS, NSA_REP).transpose(0, 2, 1, 3)
    wg = wg.reshape(D_MODEL, NSA_GROUPS, 3 * NSA_REP)
    wg = jnp.pad(wg, ((0, 0), (0, 0), (0, LANES - 3 * NSA_REP)))
    wg = wg.reshape(D_MODEL, NSA_GROUPS * LANES).astype(BF16)

    main, gates, cmp_in = _proj(h, gain, _q_scale_row(nh, w_main.shape[1]),
                                [w_main, wg, w_cmp], [BF16, F32, F32])

    half = CMP_STRIDE * HEAD_DIM
    per_slab = LANES // HEAD_DIM

    def seg_weights(w1):
        w = jnp.concatenate([w1[:half], w1[half:]], axis=1).reshape(CMP_STRIDE, HEAD_DIM, -1)
        z = jnp.zeros_like(w)
        per_group = [jnp.concatenate([w if e == k else z for k in range(per_slab)], axis=1)
                     for e in range(per_slab)]
        return jnp.stack(per_group).reshape(per_slab, CMP_STRIDE * LANES, -1).astype(BF16)

    def pos_rows(p):
        rows = [jnp.tile(p[k * CMP_STRIDE:(k + 1) * CMP_STRIDE], (1, per_slab)).reshape(1, -1)
                for k in range(2)]
        return jnp.pad(jnp.concatenate(rows, axis=0), ((0, 6), (0, 0))).astype(BF16)

    kvc = _nsa_cmp(cmp_in, batch, seq, seg_weights(k_w1), k_w2.astype(BF16), pos_rows(pos_k),
                   seg_weights(v_w1), v_w2.astype(BF16), pos_rows(pos_v))
    return _nsa_attn(main, kvc, gates, batch, seq)


def _moba_attn_kernel(q_ref, kv_ref, o_ref, m_sc, acc_sc, selb_sc, kmean_sc, *, seq, nh):
    tq, tk = ATT_TQ, ATT_TK
    i = pl.program_id(2)
    q0 = i * tq
    n_blk = seq // MOBA_BLOCK
    heads = range(nh)
    lanes = [slice(h * LANES, (h + 1) * LANES) for h in heads]

    @pl.when(i == 0)
    def _():
        for h in heads:
            for j in range(n_blk):
                blk = kv_ref[j * MOBA_BLOCK:(j + 1) * MOBA_BLOCK, lanes[h]].astype(F32)
                kmean_sc[h, j:j + 1, :] = jnp.sum(blk, axis=0, keepdims=True) * (1.0 / MOBA_BLOCK)

    qt = q_ref[...].astype(F32).T
    zpad = jnp.zeros((HEAD_DIM, tq), F32)
    qps = [jnp.concatenate([qt[h * HEAD_DIM:(h + 1) * HEAD_DIM], zpad], axis=0).astype(BF16)
           for h in heads]
    d0 = pl.multiple_of(q0, tk)
    kvds = [kv_ref[pl.ds(d0, tk), lanes[h]] for h in heads]
    s_ds = [_dot(kvds[h], qps[h]) for h in heads]
    past = _iota((n_blk, tq), 0) < i
    for h in heads:
        score = _dot(kmean_sc[h].astype(BF16), qps[h])
        adj = jnp.where(past, score, -jnp.inf)
        sel = past & _rank_select(adj, n_blk, MOBA_TOPK)
        selb_sc[h] = jnp.where(sel, 0.0, MASKED)

    m_scs = [m_sc.at[h] for h in heads]
    acc_scs = [acc_sc.at[h] for h in heads]

    causal = _causal_bias(tk, tq)
    _flash_first_multi(kvds, [s_ds[h] + causal for h in heads], [None] * nh, m_scs, acc_scs)

    def body(kt, carry):
        k0 = pl.multiple_of(kt * tk, tk)
        _flash_step_multi([kv_ref[pl.ds(k0, tk), lanes[h]] for h in heads], qps,
                          [[selb_sc[h, pl.ds(kt, 1), :]] for h in heads], m_scs, acc_scs)
        return carry

    lax.fori_loop(0, i, body, 0)
    outs =[_flash_out(acc_scs[h][...]) for h in heads]
    o_ref[...] = jnp.concatenate(outs, axis=0).T.astype(o_ref.dtype)


def _moba_attn(q, kv, batch, seq, nh=16):
    tq = ATT_TQ
    assert ATT_TK == tq and MOBA_BLOCK == tq and seq % tq == 0 and MOBA_HEADS % nh == 0
    nq = seq // tq
    t = batch * seq
    d_heads = MOBA_HEADS * HEAD_DIM
    n_blk = seq // MOBA_BLOCK
    qw = nh * HEAD_DIM
    kvw = nh * LANES
    return pl.pallas_call(
        functools.partial(_moba_attn_kernel, seq=seq, nh=nh),
        grid=(batch, MOBA_HEADS // nh, nq),
        in_specs=[
            pl.BlockSpec((tq, qw), lambda b, hg, i: (b * nq + i, hg)),
            pl.BlockSpec((seq, kvw), lambda b, hg, i: (b, hg)),
        ],
        out_specs=pl.BlockSpec((tq, qw), lambda b, hg, i: (b * nq + i, hg)),
        out_shape=jax.ShapeDtypeStruct((t, d_heads), BF16),
        scratch_shapes=[pltpu.VMEM((nh, 1, tq), F32),
                        pltpu.VMEM((nh, ACC_ROWS, tq), F32),
                        pltpu.VMEM((nh, n_blk, tq), F32),
                        pltpu.VMEM((nh, n_blk, LANES), F32)],
        compiler_params=_cparams(("parallel", "parallel", "arbitrary")),
        name="moba_attn",
    )(q, kv)


def _moba_mixer(h, gain, w_in, batch, seq):
    nh = MOBA_HEADS * HEAD_DIM
    wq = w_in[:, :nh]
    wk = w_in[:, nh:2 * nh]
    wv = w_in[:, 2 * nh:]
    w_kv = _interleave_heads(wk, wv, MOBA_HEADS).astype(BF16)
    q, kv = _proj(h, gain, _q_scale_row(nh, nh), [wq.astype(BF16), w_kv], [BF16, BF16])
    return _moba_attn(q, kv, batch, seq)


def _conv_kernel(h_ref, g_ref, win_ref, cw_ref, o_ref, ztail_sc):
    i = pl.program_id(1)
    d = D_MODEL
    a = _rms(h_ref[...], g_ref[...]).astype(BF16)
    b_gate = _dot(a, win_ref[:, 0:d])
    z = _dot(a, win_ref[:, d:2 * d]) * _dot(a, win_ref[:, 2 * d:3 * d])
    ts = z.shape[0]

    @pl.when(i == 0)
    def _():
        ztail_sc[...] = jnp.zeros_like(ztail_sc)

    zp = ztail_sc[...]
    ztail_sc[...] = z[ts - SUBLANES:ts]
    row = _iota(z.shape, 0)
    z1 = jnp.where(row == 0, zp[7:8], pltpu.roll(z, 1, 0))
    z2 = jnp.where(row == 0, zp[6:7], jnp.where(row == 1, zp[7:8], pltpu.roll(z, 2, 0)))
    y = cw_ref[0:1, :] * z2 + cw_ref[1:2, :] * z1 + cw_ref[2:3, :] * z
    o_ref[...] = (b_gate * y).astype(o_ref.dtype)


def _conv_mixer(h, gain, w_in, conv_w, batch, seq, ts=512):
    t = batch * seq
    ns = seq // ts
    assert CONV_WIDTH - 1 <= SUBLANES
    cw = jnp.pad(conv_w, ((0, SUBLANES - CONV_WIDTH), (0, 0)))
    return pl.pallas_call(
        _conv_kernel,
        grid=(batch, ns),
        in_specs=[
            pl.BlockSpec((ts, D_MODEL), lambda b, i: (b * ns + i, 0)),
            pl.BlockSpec((1, D_MODEL), lambda b, i: (0, 0)),
            pl.BlockSpec(w_in.shape, lambda b, i: (0, 0)),
            pl.BlockSpec(cw.shape, lambda b, i: (0, 0)),
        ],
        out_specs=pl.BlockSpec((ts, D_MODEL), lambda b, i: (b * ns + i, 0)),
        out_shape=jax.ShapeDtypeStruct((t, D_MODEL), BF16),
        scratch_shapes=[pltpu.VMEM((SUBLANES, D_MODEL), F32)],
        compiler_params=_cparams(("parallel", "arbitrary")),
        name="conv_mix",
    )(h, gain.reshape(1, D_MODEL), w_in.astype(BF16), cw)


def kernel(x, norm_mix, norm_ffn, norm_final, ffn_w_gate, ffn_w_up, ffn_w_down, nsa_w_in, nsa_w_out, nsa_cmp_pos_k, nsa_cmp_pos_v, nsa_cmp_k_w1, nsa_cmp_k_w2, nsa_cmp_v_w1, nsa_cmp_v_w2, moba_w_in, moba_w_out, conv_w_in, conv_w, conv_w_out):
    batch, seq, d = x.shape
    depth = norm_mix.shape[0]
    h = x.reshape(batch * seq, d)
    for i in range(depth):
        kind, j = i % N_MIXERS, i // N_MIXERS
        if kind == 0:
            o = _nsa_mixer(h, norm_mix[i], nsa_w_in[j], nsa_cmp_pos_k[j], nsa_cmp_pos_v[j],
                           nsa_cmp_k_w1[j], nsa_cmp_k_w2[j], nsa_cmp_v_w1[j], nsa_cmp_v_w2[j],
                           batch, seq)
            wo = nsa_w_out[j]
        elif kind == 1:
            o = _moba_mixer(h, norm_mix[i], moba_w_in[j], batch, seq)
            wo = moba_w_out[j]
        else:
            o = _conv_mixer(h, norm_mix[i], conv_w_in[j], conv_w[j], batch, seq)
            wo = conv_w_out[j]
        h = _ffn(h, o, wo.astype(BF16), norm_ffn[i], ffn_w_gate[i].astype(BF16),
                 ffn_w_up[i].astype(BF16), ffn_w_down[i].astype(BF16), norm_final,
                 final_norm=(i == depth - 1))
    return h.reshape(batch, seq, d)
```

```python
import functools

import jax
import jax.numpy as jnp
from jax import lax
from jax.experimental import pallas as pl
from jax.experimental.pallas import tpu as pltpu

F32 = jnp.float32
BF16 = jnp.bfloat16

D_MODEL = 1024
HEAD_DIM = 64
RMS_EPS = 1e-6
N_MIXERS = 3

NSA_HEADS = 16
NSA_GROUPS = 4
NSA_REP = NSA_HEADS // NSA_GROUPS
NSA_KV = NSA_GROUPS * HEAD_DIM
CMP_BLOCK = 32
CMP_STRIDE = 16
CMP_HIDDEN = 256
SLC_BLOCK = 64
SLC_TOPK = 16
WINDOW = 512

MOBA_HEADS = 16
MOBA_BLOCK = 256
MOBA_TOPK = 3

CONV_WIDTH = 3

LANES = 128
SUBLANES = 8
BF16_SUBLANES = 16
MXU_TILE = 256
VMEM_LIMIT = 56 * 1024 * 1024
MASKED = -1e30
M_FLOOR = -1e29
LOG2E = 1.4426950408889634

ONES_ROWS = 16
ACC_ROWS = ONES_ROWS + HEAD_DIM

ATT_TQ = 256
ATT_TK = 256


def _dot(a, b):
    return jnp.dot(a, b, preferred_element_type=F32)


def _rms(x, g):
    y = x * lax.rsqrt(jnp.mean(x * x, axis=-1, keepdims=True) + RMS_EPS)
    return y * g


def _cparams(sem, flags=None):
    return pltpu.CompilerParams(dimension_semantics=sem, vmem_limit_bytes=VMEM_LIMIT, flags=flags)


def _cast_rider(stacked, layer, n_steps, step_of):
    in_specs, out_specs, out_shapes = [], [], []
    for w in stacked:
        _, r, c = w.shape
        rb = next(b for b in range(BF16_SUBLANES, r + 1, BF16_SUBLANES)
                  if r % b == 0 and r // b <= n_steps)
        last = r // rb - 1
        out_idx = lambda *g, last=last: (jnp.minimum(step_of(*g), last), 0)
        in_idx = lambda *g, last=last: (layer, jnp.minimum(step_of(*g), last), 0)
        in_specs.append(pl.BlockSpec((None, rb, c), in_idx))
        out_specs.append(pl.BlockSpec((rb, c), out_idx))
        out_shapes.append(jax.ShapeDtypeStruct((r, c), BF16))
    return in_specs, out_specs, out_shapes


def _cast_blocks(w_refs, o_refs):
    for w_ref, o_ref in zip(w_refs, o_refs):
        o_ref[...] = w_ref[...].astype(o_ref.dtype)


def _proj_kernel(*refs, n_out, n_cast, chunk):
    x_ref, g_ref, sc_ref = refs[0], refs[1], refs[2]
    w_refs = refs[3:3 + n_out]
    c_refs = refs[3 + n_out:3 + n_out + n_cast]
    o_refs = refs[3 + n_out + n_cast:3 + 2 * n_out + n_cast]
    _cast_blocks(c_refs, refs[3 + 2 * n_out + n_cast:])
    a = _rms(x_ref[...], g_ref[...]).astype(BF16)
    for k, (w_ref, o_ref) in enumerate(zip(w_refs, o_refs)):
        n = w_ref.shape[1]
        for n0 in range(0, n, chunk):
            n1 = min(n0 + chunk, n)
            y = _dot(a, w_ref[:, n0:n1])
            if k == 0:
                y = y * sc_ref[:, n0:n1]
            o_ref[:, n0:n1] = y.astype(o_ref.dtype)


def _proj(h, gain, scale0, weights, out_dtypes, cast, layer, tm=512):
    t = h.shape[0]
    n_out = len(weights)
    n_steps = t // tm
    const = lambda a: pl.BlockSpec(a.shape, lambda i: (0, 0))
    gain = gain.reshape(1, D_MODEL)
    c_in, c_out, c_shapes = _cast_rider(cast, layer, n_steps, lambda i: i)
    in_specs = [pl.BlockSpec((tm, D_MODEL), lambda i: (i, 0)), const(gain), const(scale0)]
    in_specs += [const(w) for w in weights] + c_in
    out_specs = [pl.BlockSpec((tm, w.shape[1]), lambda i: (i, 0)) for w in weights] + c_out
    out_shape = [jax.ShapeDtypeStruct((t, w.shape[1]), dt) for w, dt in zip(weights, out_dtypes)]
    outs = pl.pallas_call(
        functools.partial(_proj_kernel, n_out=n_out, n_cast=len(cast), chunk=512),
        grid=(n_steps,),
        in_specs=in_specs, out_specs=out_specs, out_shape=out_shape + c_shapes,
        compiler_params=_cparams(("arbitrary",)),
        name="proj",
    )(h, gain, scale0, *weights, *cast)
    return outs[:n_out], outs[n_out:]


def _q_scale_row(n_q, n_total):
    q = jnp.full((1, n_q), (HEAD_DIM ** -0.5) * LOG2E, F32)
    return jnp.concatenate([q, jnp.ones((1, n_total - n_q), F32)], axis=1)


def _ffn_kernel(h_ref, o_ref, wo_ref, g_ref, wg_ref, wu_ref, wd_ref, gf_ref, out_ref,
                *, final_norm, chunks):
    h1 = h_ref[...] + _dot(o_ref[...], wo_ref[...])
    a = _rms(h1, g_ref[...]).astype(BF16)
    out_ref[...] = h1
    for c0, c1 in chunks:
        gate = _dot(a, wg_ref[:, c0:c1])
        up = _dot(a, wu_ref[:, c0:c1])
        t = (jax.nn.silu(gate) * up).astype(BF16)
        out_ref[...] += _dot(t, wd_ref[c0:c1, :])
    if final_norm:
        out_ref[...] = _rms(out_ref[...], gf_ref[...])


def _ffn(h, o, wo, gain, wg, wu, wd, final_gain, final_norm, tm=512):
    t = h.shape[0]
    d_ff = wg.shape[1]
    step = 3 * MXU_TILE
    chunks = tuple((c0, min(c0 + step, d_ff)) for c0 in range(0, d_ff, step))
    assert d_ff % MXU_TILE == 0
    gain = gain.reshape(1, D_MODEL)
    final_gain = final_gain.reshape(1, D_MODEL)
    const = lambda a: pl.BlockSpec(a.shape, lambda i: (0, 0), pipeline_mode=pl.Buffered(1))
    return pl.pallas_call(
        functools.partial(_ffn_kernel, final_norm=final_norm, chunks=chunks),
        grid=(t // tm,),
        in_specs=[
            pl.BlockSpec((tm, D_MODEL), lambda i: (i, 0)),
            pl.BlockSpec((tm, o.shape[1]), lambda i: (i, 0)),
            const(wo), const(gain), const(wg), const(wu), const(wd), const(final_gain),
        ],
        out_specs=pl.BlockSpec((tm, D_MODEL), lambda i: (i, 0)),
        out_shape=jax.ShapeDtypeStruct((t, D_MODEL), F32),
        compiler_params=_cparams(("parallel",)),
        name="ffn",
    )(h, o, wo, gain, wg, wu, wd, final_gain)


def _block_max(s, rows):
    if rows is None:
        return jnp.max(s, axis=0, keepdims=True)
    h = s.shape[0] // len(rows)
    ms = [jnp.max(s[a * h:(a + 1) * h].reshape(h // SUBLANES, SUBLANES, s.shape[1]), axis=0) + row
          for a, row in enumerate(rows)]
    return jnp.max(functools.reduce(jnp.maximum, ms), axis=0, keepdims=True)


def _block_exp2(s, rows, m):
    if rows is None:
        return jnp.exp2(s - m)
    h = s.shape[0] // len(rows)
    return jnp.concatenate([jnp.exp2(s[a * h:(a + 1) * h] - (m - row))
                            for a, row in enumerate(rows)], axis=0)


def _flash_step_multi(kvs, qps, rows, m_scs, acc_scs):
    n = len(kvs)
    ss = [_dot(kvs[c], qps[c]) for c in range(n)]
    m_olds = [m_scs[c][...] for c in range(n)]
    acc_olds = [acc_scs[c][...] for c in range(n)]
    m_news = [jnp.maximum(m_olds[c], _block_max(ss[c], rows[c])) for c in range(n)]
    alphas = [jnp.exp2(m_olds[c] - m_news[c]) for c in range(n)]
    ps = [_block_exp2(ss[c], rows[c], m_news[c]) for c in range(n)]
    pvs = [_pv(kvs[c], ps[c]) for c in range(n)]
    acc_news = [alphas[c] * acc_olds[c] + pvs[c] for c in range(n)]
    for c in range(n):
        m_scs[c][...] = m_news[c]
        acc_scs[c][...] = acc_news[c]


def _flash_first_multi(kvs, ss, rows, m_scs, acc_scs):
    n = len(kvs)
    ms = [jnp.maximum(_block_max(ss[c], rows[c]), M_FLOOR) for c in range(n)]
    ps = [_block_exp2(ss[c], rows[c], ms[c]) for c in range(n)]
    accs = [_pv(kvs[c], ps[c]) for c in range(n)]
    for c in range(n):
        m_scs[c][...] = ms[c]
        acc_scs[c][...] = accs[c]


def _pv(kv, p):
    tk = kv.shape[0]
    lhs = jnp.concatenate([jnp.ones((ONES_ROWS, tk), BF16), kv.T[HEAD_DIM:2 * HEAD_DIM]], axis=0)
    return _dot(lhs, p.astype(BF16))


def _flash_out(acc):
    l = acc[0:1, :]
    return acc[ONES_ROWS:ACC_ROWS, :] / jnp.where(l > 0, l, 1.0)


def _iota(shape, dim):
    return lax.broadcasted_iota(jnp.int32, shape, dim)


def _causal_bias(tk, tq, base=None):
    ok = _iota((tk, tq), 0) <= _iota((tk, tq), 1)
    return jnp.where(ok, 0.0 if base is None else base, MASKED)


def _rank_select(adj, n_rows, topk):
    j = _iota(adj.shape, 0)
    rank = jnp.zeros(adj.shape, jnp.int32)
    for k in range(n_rows):
        rk = adj[k:k + 1, :]
        beats = (rk > adj) | ((rk == adj) & (j > k))
        rank = rank + beats.astype(jnp.int32)
    return rank < topk


def _nsa_cmp_kernel(*refs):
    n_slab = 2 * NSA_KV // LANES
    x_refs = refs[:n_slab]
    w1k_ref, w2k_ref, pk_ref, w1v_ref, w2v_ref, pv_ref, o_ref = refs[n_slab:]
    per_slab = LANES // HEAD_DIM
    nb, _, n_seg, _ = o_ref.shape
    seq = n_seg * CMP_STRIDE
    for t, x_ref in enumerate(x_refs):
        is_v = t >= n_slab // 2
        w1_ref, w2_ref, p_ref = (w1v_ref, w2v_ref, pv_ref) if is_v else (w1k_ref, w2k_ref, pk_ref)
        lane0 = HEAD_DIM if is_v else 0
        tok = [jnp.concatenate([x_ref[pl.ds(bb * seq + l, n_seg, stride=CMP_STRIDE), :]
                                for bb in range(nb)], axis=0) for l in range(CMP_STRIDE)]
        pp = _dot(p_ref[...], w1_ref[...])
        posb = pp[0:1, :CMP_HIDDEN] + pp[1:2, CMP_HIDDEN:]
        for e in range(per_slab):
            g = (t % (n_slab // 2)) * per_slab + e
            x = jnp.concatenate([tk_[:, e * HEAD_DIM:(e + 1) * HEAD_DIM] for tk_ in tok],
                                axis=1).astype(BF16)
            acc = _dot(x, w1_ref[...])
            hid = acc[:, :CMP_HIDDEN] + pltpu.roll(acc[:, CMP_HIDDEN:], nb * n_seg - 1, 0) + posb
            y = _dot(jax.nn.gelu(hid).astype(BF16), w2_ref[...])
            for bb in range(nb):
                o_ref[bb, g, :, lane0:lane0 + HEAD_DIM] = (
                    y[bb * n_seg:(bb + 1) * n_seg].astype(o_ref.dtype))


def _nsa_cmp(cmp_in, batch, seq, w1k, w2k, pk, w1v, w2v, pv):
    full = lambda a: pl.BlockSpec(a.shape, lambda i: (0,) * a.ndim)
    n_seg = seq // CMP_STRIDE
    n_slab = 2 * NSA_KV // LANES
    nb = max(d for d in (1, 2, 4) if batch % d == 0)
    slabs = [pl.BlockSpec((nb * seq, LANES), functools.partial(lambda i, t: (i, t), t=t))
             for t in range(n_slab)]
    return pl.pallas_call(
        _nsa_cmp_kernel,
        grid=(batch // nb,),
        in_specs=slabs + [full(w1k), full(w2k), full(pk), full(w1v), full(w2v), full(pv)],
        out_specs=pl.BlockSpec((nb, NSA_GROUPS, n_seg, 2 * HEAD_DIM), lambda i: (i, 0, 0, 0)),
        out_shape=jax.ShapeDtypeStruct((batch, NSA_GROUPS, n_seg, 2 * HEAD_DIM), BF16),
        compiler_params=_cparams(("parallel",)),
        name="nsa_cmp",
    )(*([cmp_in] * n_slab), w1k, w2k, pk, w1v, w2v, pv)


def _nsa_attn_kernel(q_ref, kvs_ref, kvw_ref, kvc_ref, gt_ref, o_ref,
                     m_sc, acc_sc, selb_sc, out_sc, *, seq, ng):
    tq, tk = ATT_TQ, ATT_TK
    i = pl.program_id(2)
    q0 = i * tq
    n_slc = seq // SLC_BLOCK
    n_cmp_pad = seq // CMP_STRIDE
    rep = NSA_REP
    sub = tk // SLC_BLOCK
    win_rows = WINDOW + tq
    groups = range(ng)
    lanes = [slice(g * LANES, (g + 1) * LANES) for g in groups]

    def heads(x):
        return jnp.concatenate([x] * rep, axis=1)

    qt = q_ref[...].astype(F32).T
    zpad = jnp.zeros((HEAD_DIM, tq), F32)
    qps = [jnp.concatenate(
        [jnp.concatenate([qt[(g * rep + r) * HEAD_DIM:(g * rep + r + 1) * HEAD_DIM], zpad], axis=0)
         for r in range(rep)], axis=1).astype(BF16) for g in groups]
    gates = jax.nn.sigmoid(gt_ref[...]).T

    def emit(g, branch, o_t, first):
        for r in range(rep):
            row = branch * NSA_HEADS + g * rep + r
            val = gates[row:row + 1, :] * o_t[:, r * tq:(r + 1) * tq]
            sl = slice((g * rep + r) * HEAD_DIM, (g * rep + r + 1) * HEAD_DIM)
            if first:
                out_sc[sl, :] = val
            else:
                out_sc[sl, :] += val

    w0 = pl.multiple_of(jnp.maximum(q0 - WINDOW, 0), tk)
    d0 = pl.multiple_of(q0, tk)
    kvcs = [kvc_ref[0, g] for g in groups]
    kvws = [kvw_ref[pl.ds(w0, win_rows), lanes[g]] for g in groups]
    kvds = [kvs_ref[pl.ds(d0, tk), lanes[g]] for g in groups]
    s_cs = [_dot(kvcs[g], qps[g]) for g in groups]
    s_ws = [_dot(kvws[g], qps[g]) for g in groups]
    s_ds = [_dot(kvds[g], qps[g]) for g in groups]

    c_end = _iota((n_cmp_pad, tq), 0) * CMP_STRIDE + (CMP_BLOCK - 1)
    c_ok = (c_end <= q0 + _iota((n_cmp_pad, tq), 1)) & (c_end < seq)
    c_bias = heads(jnp.where(c_ok, 0.0, MASKED))
    k_pos = w0 + _iota((win_rows, tq), 0)
    t_pos = q0 + _iota((win_rows, tq), 1)
    w_bias = heads(jnp.where((k_pos <= t_pos) & (k_pos > t_pos - WINDOW), 0.0, MASKED))
    causal = _causal_bias(tk, tq)

    jj = _iota((n_slc, n_cmp_pad), 0) * SLC_BLOCK
    cc = _iota((n_slc, n_cmp_pad), 1) * CMP_STRIDE
    overlap_t = jnp.where((cc < jj + SLC_BLOCK) & (cc + CMP_BLOCK > jj) & (cc + CMP_BLOCK <= seq),
                          1.0, 0.0).astype(BF16)
    j_blk = _iota((n_slc, tq), 0)
    cur = (q0 + _iota((n_slc, tq), 1)) // SLC_BLOCK
    forced = (j_blk == 0) | (j_blk == cur) | (j_blk == cur - 1)
    valid = j_blk <= cur

    ss = [s_cs[g] + c_bias for g in groups]
    ms = [jnp.maximum(jnp.max(s, axis=0, keepdims=True), M_FLOOR) for s in ss]
    es = [jnp.exp2(ss[g] - ms[g]) for g in groups]
    ds = [jnp.sum(e, axis=0, keepdims=True) for e in es]
    p_cs = [es[g] / jnp.where(ds[g] > 0, ds[g], 1.0) for g in groups]
    o_cs = [_pv(kvcs[g], p_cs[g])[ONES_ROWS:ACC_ROWS, :] for g in groups]

    ss = [s_ws[g] + w_bias for g in groups]
    ms = [jnp.max(s, axis=0, keepdims=True) for s in ss]
    o_ws = [_flash_out(_pv(kvws[g], jnp.exp2(ss[g] - ms[g]))) for g in groups]

    selbs = []
    for g in groups:
        p_sum = p_cs[g][:, 0:tq]
        for r in range(1, rep):
            p_sum = p_sum + p_cs[g][:, r * tq:(r + 1) * tq]
        p_hi = p_sum.astype(BF16)
        p_lo = (p_sum - p_hi.astype(F32)).astype(BF16)
        imp = _dot(overlap_t, p_hi) + _dot(overlap_t, p_lo)
        adj = jnp.where(forced, jnp.inf, jnp.where(valid, imp, -jnp.inf))
        sel = valid & _rank_select(adj, n_slc, min(SLC_TOPK, n_slc))
        selbs.append(jnp.where(sel, 0.0, MASKED))

    for g in groups:
        selb_sc[g] = selbs[g]
        emit(g, 0, o_cs[g], True)
        emit(g, 2, o_ws[g], False)

    def sel_rows(g, blk0):
        return [heads(selb_sc[g, pl.ds(blk0 + a, 1), :]) for a in range(sub)]

    m_scs = [m_sc.at[g] for g in groups]
    acc_scs = [acc_sc.at[g] for g in groups]

    causal_h = heads(causal)
    _flash_first_multi(kvds, [s_ds[g] + causal_h for g in groups],
                       [sel_rows(g, i * sub) for g in groups], m_scs, acc_scs)

    def slc_body(kt, carry):
        k0 = pl.multiple_of(kt * tk, tk)
        _flash_step_multi([kvs_ref[pl.ds(k0, tk), lanes[g]] for g in groups], qps,
                          [sel_rows(g, kt * sub) for g in groups], m_scs, acc_scs)
        return carry

    lax.fori_loop(0, i, slc_body, 0)
    for g in groups:
        emit(g, 1, _flash_out(acc_scs[g][...]), False)

    o_ref[...] = out_sc[...].T.astype(o_ref.dtype)


def _nsa_attn(main, kvc, gates, batch, seq, ng=4):
    tq = ATT_TQ
    assert ATT_TK == tq and WINDOW % tq == 0 and seq % tq == 0 and seq >= WINDOW + tq
    assert ng == NSA_GROUPS and tq % SLC_BLOCK == 0
    nq = seq // tq
    t = batch * seq
    gw = ng * NSA_REP * HEAD_DIM
    kvw = ng * LANES
    assert (NSA_HEADS * HEAD_DIM) % kvw == 0
    kv0 = NSA_HEADS * HEAD_DIM // kvw
    m = NSA_REP * tq
    n_cmp_pad = seq // CMP_STRIDE
    return pl.pallas_call(
        functools.partial(_nsa_attn_kernel, seq=seq, ng=ng),
        grid=(batch, NSA_GROUPS // ng, nq),
        in_specs=[
            pl.BlockSpec((tq, gw), lambda b, g, i: (b * nq + i, g)),
            pl.BlockSpec((seq, kvw), lambda b, g, i: (b, kv0 + g)),
            pl.BlockSpec((seq, kvw), lambda b, g, i: (b, kv0 + NSA_GROUPS // ng + g)),
            pl.BlockSpec((1, ng, n_cmp_pad, LANES), lambda b, g, i: (b, g, 0, 0)),
            pl.BlockSpec((tq, LANES), lambda b, g, i: (b * nq + i, 0)),
        ],
        out_specs=pl.BlockSpec((tq, gw), lambda b, g, i: (b * nq + i, g)),
        out_shape=jax.ShapeDtypeStruct((t, NSA_HEADS * HEAD_DIM), BF16),
        scratch_shapes=[pltpu.VMEM((ng, 1, m), F32),
                        pltpu.VMEM((ng, ACC_ROWS, m), F32),
                        pltpu.VMEM((ng, seq // SLC_BLOCK, tq), F32),
                        pltpu.VMEM((gw, tq), F32)],
        compiler_params=_cparams(("parallel", "parallel", "arbitrary")),
        name="nsa_attn",
    )(main, main, main, kvc, gates)


def _interleave_heads(wa, wb, n):
    d = wa.shape[0]
    return jnp.stack([wa.reshape(d, n, HEAD_DIM), wb.reshape(d, n, HEAD_DIM)],
                     axis=2).reshape(d, 2 * n * HEAD_DIM)


def _nsa_mixer(h, gain, w_in, pos_k, pos_v, k_w1, k_w2, v_w1, v_w2, batch, seq, cast, layer):
    nh = NSA_HEADS * HEAD_DIM
    c = [nh + i * NSA_KV for i in range(7)]
    wq = w_in[:, :c[0]]
    wkc, wvc, wks, wvs, wkw, wvw = (w_in[:, c[i]:c[i + 1]] for i in range(6))
    w_gate = w_in[:, c[6]:]
    w_main = jnp.concatenate(
        [wq, _interleave_heads(wks, wvs, NSA_GROUPS), _interleave_heads(wkw, wvw, NSA_GROUPS)],
        axis=1).astype(BF16)
    w_cmp = jnp.concatenate([wkc, wvc], axis=1).astype(BF16)
    wg = jnp.pad(w_gate, ((0, 0), (0, LANES - 3 * NSA_HEADS))).astype(BF16)

    (main, gates, cmp_in), casted = _proj(h, gain, _q_scale_row(nh, w_main.shape[1]),
                                          [w_main, wg, w_cmp], [BF16, F32, F32], cast, layer)

    half = CMP_STRIDE * HEAD_DIM

    def seg_weights(w1):
        return jnp.concatenate([w1[:half], w1[half:]], axis=1).astype(BF16)

    def pos_rows(p):
        rows = p.reshape(2, half)
        return jnp.pad(rows, ((0, SUBLANES - 2), (0, 0))).astype(BF16)

    kvc = _nsa_cmp(cmp_in, batch, seq, seg_weights(k_w1), k_w2.astype(BF16), pos_rows(pos_k),
                   seg_weights(v_w1), v_w2.astype(BF16), pos_rows(pos_v))
    return _nsa_attn(main, kvc, gates, batch, seq), casted


def _moba_attn_kernel(q_ref, kv_ref, o_ref, m_sc, acc_sc, selb_sc, kmean_sc, *, seq, nh):
    tq, tk = ATT_TQ, ATT_TK
    i = pl.program_id(2)
    q0 = i * tq
    n_blk = seq // MOBA_BLOCK
    heads = range(nh)
    lanes = [slice(h * LANES, (h + 1) * LANES) for h in heads]

    @pl.when(i == 0)
    def _():
        for h in heads:
            for j in range(n_blk):
                blk = kv_ref[j * MOBA_BLOCK:(j + 1) * MOBA_BLOCK, lanes[h]].astype(F32)
                kmean_sc[h, j:j + 1, :] = jnp.sum(blk, axis=0, keepdims=True) * (1.0 / MOBA_BLOCK)

    qt = q_ref[...].astype(F32).T
    zpad = jnp.zeros((HEAD_DIM, tq), F32)
    qps = [jnp.concatenate([qt[h * HEAD_DIM:(h + 1) * HEAD_DIM], zpad], axis=0).astype(BF16)
           for h in heads]
    d0 = pl.multiple_of(q0, tk)
    kvds = [kv_ref[pl.ds(d0, tk), lanes[h]] for h in heads]
    s_ds = [_dot(kvds[h], qps[h]) for h in heads]
    past = _iota((n_blk, tq), 0) < i
    for h in heads:
        score = _dot(kmean_sc[h].astype(BF16), qps[h])
        adj = jnp.where(past, score, -jnp.inf)
        sel = past & _rank_select(adj, n_blk, MOBA_TOPK)
        selb_sc[h] = jnp.where(sel, 0.0, MASKED)

    m_scs = [m_sc.at[h] for h in heads]
    acc_scs = [acc_sc.at[h] for h in heads]

    causal = _causal_bias(tk, tq)
    _flash_first_multi(kvds, [s_ds[h] + causal for h in heads], [None] * nh, m_scs, acc_scs)

    def body(kt, carry):
        k0 = pl.multiple_of(kt * tk, tk)
        _flash_step_multi([kv_ref[pl.ds(k0, tk), lanes[h]] for h in heads], qps,
                          [[selb_sc[h, pl.ds(kt, 1), :]] for h in heads], m_scs, acc_scs)
        return carry

    lax.fori_loop(0, i, body, 0)
    outs =[_flash_out(acc_scs[h][...]) for h in heads]
    o_ref[...] = jnp.concatenate(outs, axis=0).T.astype(o_ref.dtype)


def _moba_attn(q, kv, batch, seq, nh=16):
    tq = ATT_TQ
    assert ATT_TK == tq and MOBA_BLOCK == tq and seq % tq == 0 and MOBA_HEADS % nh == 0
    nq = seq // tq
    t = batch * seq
    d_heads = MOBA_HEADS * HEAD_DIM
    n_blk = seq // MOBA_BLOCK
    qw = nh * HEAD_DIM
    kvw = nh * LANES
    return pl.pallas_call(
        functools.partial(_moba_attn_kernel, seq=seq, nh=nh),
        grid=(batch, MOBA_HEADS // nh, nq),
        in_specs=[
            pl.BlockSpec((tq, qw), lambda b, hg, i: (b * nq + i, hg)),
            pl.BlockSpec((seq, kvw), lambda b, hg, i: (b, hg)),
        ],
        out_specs=pl.BlockSpec((tq, qw), lambda b, hg, i: (b * nq + i, hg)),
        out_shape=jax.ShapeDtypeStruct((t, d_heads), BF16),
        scratch_shapes=[pltpu.VMEM((nh, 1, tq), F32),
                        pltpu.VMEM((nh, ACC_ROWS, tq), F32),
                        pltpu.VMEM((nh, n_blk, tq), F32),
                        pltpu.VMEM((nh, n_blk, LANES), F32)],
        compiler_params=_cparams(("parallel", "parallel", "arbitrary")),
        name="moba_attn",
    )(q, kv)


def _moba_mixer(h, gain, w_in, batch, seq, cast, layer):
    nh = MOBA_HEADS * HEAD_DIM
    wq = w_in[:, :nh]
    wk = w_in[:, nh:2 * nh]
    wv = w_in[:, 2 * nh:]
    w_kv = _interleave_heads(wk, wv, MOBA_HEADS).astype(BF16)
    (q, kv), casted = _proj(h, gain, _q_scale_row(nh, nh), [wq.astype(BF16), w_kv], [BF16, BF16],
                            cast, layer)
    return _moba_attn(q, kv, batch, seq), casted


def _conv_kernel(*refs, n_cast):
    h_ref, g_ref, win_ref, cw_ref = refs[:4]
    c_refs = refs[4:4 + n_cast]
    o_ref = refs[4 + n_cast]
    ztail_sc = refs[-1]
    _cast_blocks(c_refs, refs[5 + n_cast:5 + 2 * n_cast])
    i = pl.program_id(1)
    d = D_MODEL
    a = _rms(h_ref[...], g_ref[...]).astype(BF16)
    b_gate = _dot(a, win_ref[:, 0:d])
    z = _dot(a, win_ref[:, d:2 * d]) * _dot(a, win_ref[:, 2 * d:3 * d])
    ts = z.shape[0]

    @pl.when(i == 0)
    def _():
        ztail_sc[...] = jnp.zeros_like(ztail_sc)

    zp = ztail_sc[...]
    ztail_sc[...] = z[ts - SUBLANES:ts]
    row = _iota(z.shape, 0)
    z1 = jnp.where(row == 0, zp[7:8], pltpu.roll(z, 1, 0))
    z2 = jnp.where(row == 0, zp[6:7], jnp.where(row == 1, zp[7:8], pltpu.roll(z, 2, 0)))
    y = cw_ref[0:1, :] * z2 + cw_ref[1:2, :] * z1 + cw_ref[2:3, :] * z
    o_ref[...] = (b_gate * y).astype(o_ref.dtype)


def _conv_mixer(h, gain, w_in, conv_w, batch, seq, cast, layer, ts=512):
    t = batch * seq
    ns = seq // ts
    assert CONV_WIDTH - 1 <= SUBLANES
    cw = jnp.pad(conv_w, ((0, SUBLANES - CONV_WIDTH), (0, 0)))
    c_in, c_out, c_shapes = _cast_rider(cast, layer, batch * ns, lambda b, i: b * ns + i)
    outs = pl.pallas_call(
        functools.partial(_conv_kernel, n_cast=len(cast)),
        grid=(batch, ns),
        in_specs=[
            pl.BlockSpec((ts, D_MODEL), lambda b, i: (b * ns + i, 0)),
            pl.BlockSpec((1, D_MODEL), lambda b, i: (0, 0)),
            pl.BlockSpec(w_in.shape, lambda b, i: (0, 0)),
            pl.BlockSpec(cw.shape, lambda b, i: (0, 0)),
        ] + c_in,
        out_specs=[pl.BlockSpec((ts, D_MODEL), lambda b, i: (b * ns + i, 0))] + c_out,
        out_shape=[jax.ShapeDtypeStruct((t, D_MODEL), BF16)] + c_shapes,
        scratch_shapes=[pltpu.VMEM((SUBLANES, D_MODEL), F32)],
        compiler_params=_cparams(("arbitrary", "arbitrary")),
        name="conv_mix",
    )(h, gain.reshape(1, D_MODEL), w_in.astype(BF16), cw, *cast)
    return outs[0], outs[1:]


def kernel(x, norm_mix, norm_ffn, norm_final, ffn_w_gate, ffn_w_up, ffn_w_down, nsa_w_in, nsa_w_out, nsa_cmp_pos_k, nsa_cmp_pos_v, nsa_cmp_k_w1, nsa_cmp_k_w2, nsa_cmp_v_w1, nsa_cmp_v_w2, moba_w_in, moba_w_out, conv_w_in, conv_w, conv_w_out):
    batch, seq, d = x.shape
    depth = norm_mix.shape[0]
    h = x.reshape(batch * seq, d)
    ffn_f32 = [ffn_w_gate, ffn_w_up, ffn_w_down]
    for i in range(depth):
        kind, j = i % N_MIXERS, i // N_MIXERS
        if kind == 0:
            o, ffn_w = _nsa_mixer(h, norm_mix[i], nsa_w_in[j], nsa_cmp_pos_k[j], nsa_cmp_pos_v[j],
                                  nsa_cmp_k_w1[j], nsa_cmp_k_w2[j], nsa_cmp_v_w1[j], nsa_cmp_v_w2[j],
                                  batch, seq, ffn_f32, i)
            wo = nsa_w_out[j]
        elif kind == 1:
            o, ffn_w = _moba_mixer(h, norm_mix[i], moba_w_in[j], batch, seq, ffn_f32, i)
            wo = moba_w_out[j]
        else:
            o, ffn_w = _conv_mixer(h, norm_mix[i], conv_w_in[j], conv_w[j], batch, seq, ffn_f32, i)
            wo = conv_w_out[j]
        wg, wu, wd = ffn_w
        h = _ffn(h, o, wo.astype(BF16), norm_ffn[i], wg, wu, wd, norm_final,
                 final_norm=(i == depth - 1))
    return h.reshape(batch, seq, d)
```

```python
import functools

import jax
import jax.numpy as jnp
from jax import lax
from jax.experimental import pallas as pl
from jax.experimental.pallas import tpu as pltpu

F32 = jnp.float32
BF16 = jnp.bfloat16

D_MODEL = 1024
HEAD_DIM = 64
RMS_EPS = 1e-6
N_MIXERS = 3

NSA_HEADS = 16
NSA_GROUPS = 4
NSA_REP = NSA_HEADS // NSA_GROUPS
NSA_KV = NSA_GROUPS * HEAD_DIM
CMP_BLOCK = 32
CMP_STRIDE = 16
CMP_HIDDEN = 256
SLC_BLOCK = 64
SLC_TOPK = 16
WINDOW = 512

MOBA_HEADS = 16
MOBA_BLOCK = 256
MOBA_TOPK = 3

CONV_WIDTH = 3

LANES = 128
SUBLANES = 8
BF16_SUBLANES = 16
MXU_TILE = 256
VMEM_LIMIT = 56 * 1024 * 1024
MASKED = -1e30
M_FLOOR = -1e29
LOG2E = 1.4426950408889634

ONES_ROWS = 16
ACC_ROWS = ONES_ROWS + HEAD_DIM

ATT_TQ = 256
ATT_TK = 256


def _dot(a, b):
    return jnp.dot(a, b, preferred_element_type=F32)


def _rms(x, g):
    y = x * lax.rsqrt(jnp.mean(x * x, axis=-1, keepdims=True) + RMS_EPS)
    return y * g


def _cparams(sem, flags=None):
    return pltpu.CompilerParams(dimension_semantics=sem, vmem_limit_bytes=VMEM_LIMIT, flags=flags)


def _cast_rider(stacked, layer, n_steps, step_of):
    in_specs, out_specs, out_shapes = [], [], []
    for w in stacked:
        _, r, c = w.shape
        rb = next(b for b in range(BF16_SUBLANES, r + 1, BF16_SUBLANES)
                  if r % b == 0 and r // b <= n_steps)
        last = r // rb - 1
        out_idx = lambda *g, last=last: (jnp.minimum(step_of(*g), last), 0)
        in_idx = lambda *g, last=last: (layer, jnp.minimum(step_of(*g), last), 0)
        in_specs.append(pl.BlockSpec((None, rb, c), in_idx))
        out_specs.append(pl.BlockSpec((rb, c), out_idx))
        out_shapes.append(jax.ShapeDtypeStruct((r, c), BF16))
    return in_specs, out_specs, out_shapes


def _cast_blocks(w_refs, o_refs):
    for w_ref, o_ref in zip(w_refs, o_refs):
        o_ref[...] = w_ref[...].astype(o_ref.dtype)


def _proj_kernel(*refs, n_out, n_cast, chunk):
    x_ref, g_ref, sc_ref = refs[0], refs[1], refs[2]
    w_refs = refs[3:3 + n_out]
    c_refs = refs[3 + n_out:3 + n_out + n_cast]
    o_refs = refs[3 + n_out + n_cast:3 + 2 * n_out + n_cast]
    _cast_blocks(c_refs, refs[3 + 2 * n_out + n_cast:])
    a = _rms(x_ref[...], g_ref[...]).astype(BF16)
    for k, (w_ref, o_ref) in enumerate(zip(w_refs, o_refs)):
        n = w_ref.shape[1]
        for n0 in range(0, n, chunk):
            n1 = min(n0 + chunk, n)
            y = _dot(a, w_ref[:, n0:n1])
            if k == 0:
                y = y * sc_ref[:, n0:n1]
            o_ref[:, n0:n1] = y.astype(o_ref.dtype)


def _proj(h, gain, scale0, weights, out_dtypes, cast, layer, tm=512):
    t = h.shape[0]
    n_out = len(weights)
    n_steps = t // tm
    const = lambda a: pl.BlockSpec(a.shape, lambda i: (0, 0))
    gain = gain.reshape(1, D_MODEL)
    c_in, c_out, c_shapes = _cast_rider(cast, layer, n_steps, lambda i: i)
    in_specs = [pl.BlockSpec((tm, D_MODEL), lambda i: (i, 0)), const(gain), const(scale0)]
    in_specs += [const(w) for w in weights] + c_in
    out_specs = [pl.BlockSpec((tm, w.shape[1]), lambda i: (i, 0)) for w in weights] + c_out
    out_shape = [jax.ShapeDtypeStruct((t, w.shape[1]), dt) for w, dt in zip(weights, out_dtypes)]
    outs = pl.pallas_call(
        functools.partial(_proj_kernel, n_out=n_out, n_cast=len(cast), chunk=512),
        grid=(n_steps,),
        in_specs=in_specs, out_specs=out_specs, out_shape=out_shape + c_shapes,
        compiler_params=_cparams(("arbitrary",)),
        name="proj",
    )(h, gain, scale0, *weights, *cast)
    return outs[:n_out], outs[n_out:]


def _q_scale_row(n_q, n_total):
    q = jnp.full((1, n_q), (HEAD_DIM ** -0.5) * LOG2E, F32)
    return jnp.concatenate([q, jnp.ones((1, n_total - n_q), F32)], axis=1)


def _ffn_kernel(h_ref, o_ref, wo_ref, g_ref, wg_ref, wu_ref, wd_ref, gf_ref, out_ref,
                *, final_norm, chunks):
    h1 = h_ref[...] + _dot(o_ref[...], wo_ref[...])
    a = _rms(h1, g_ref[...]).astype(BF16)
    out_ref[...] = h1
    for c0, c1 in chunks:
        gate = _dot(a, wg_ref[:, c0:c1])
        up = _dot(a, wu_ref[:, c0:c1])
        t = (jax.nn.silu(gate) * up).astype(BF16)
        out_ref[...] += _dot(t, wd_ref[c0:c1, :])
    if final_norm:
        out_ref[...] = _rms(out_ref[...], gf_ref[...])


def _ffn(h, o, wo, gain, wg, wu, wd, final_gain, final_norm, tm=512):
    t = h.shape[0]
    d_ff = wg.shape[1]
    step = 3 * MXU_TILE
    chunks = tuple((c0, min(c0 + step, d_ff)) for c0 in range(0, d_ff, step))
    assert d_ff % MXU_TILE == 0
    gain = gain.reshape(1, D_MODEL)
    final_gain = final_gain.reshape(1, D_MODEL)
    const = lambda a: pl.BlockSpec(a.shape, lambda i: (0, 0), pipeline_mode=pl.Buffered(1))
    return pl.pallas_call(
        functools.partial(_ffn_kernel, final_norm=final_norm, chunks=chunks),
        grid=(t // tm,),
        in_specs=[
            pl.BlockSpec((tm, D_MODEL), lambda i: (i, 0)),
            pl.BlockSpec((tm, o.shape[1]), lambda i: (i, 0)),
            const(wo), const(gain), const(wg), const(wu), const(wd), const(final_gain),
        ],
        out_specs=pl.BlockSpec((tm, D_MODEL), lambda i: (i, 0)),
        out_shape=jax.ShapeDtypeStruct((t, D_MODEL), F32),
        compiler_params=_cparams(("parallel",)),
        name="ffn",
    )(h, o, wo, gain, wg, wu, wd, final_gain)


def _block_max(s, rows):
    if rows is None:
        return jnp.max(s, axis=0, keepdims=True)
    h = s.shape[0] // len(rows)
    ms = [jnp.max(s[a * h:(a + 1) * h].reshape(h // SUBLANES, SUBLANES, s.shape[1]), axis=0) + row
          for a, row in enumerate(rows)]
    return jnp.max(functools.reduce(jnp.maximum, ms), axis=0, keepdims=True)


def _block_exp2(s, rows, m):
    if rows is None:
        return jnp.exp2(s - m)
    h = s.shape[0] // len(rows)
    return jnp.concatenate([jnp.exp2(s[a * h:(a + 1) * h] - (m - row))
                            for a, row in enumerate(rows)], axis=0)


def _flash_step_multi(kvs, qps, rows, m_scs, acc_scs):
    n = len(kvs)
    ss = [_dot(kvs[c], qps[c]) for c in range(n)]
    m_olds = [m_scs[c][...] for c in range(n)]
    acc_olds = [acc_scs[c][...] for c in range(n)]
    m_news = [jnp.maximum(m_olds[c], _block_max(ss[c], rows[c])) for c in range(n)]
    alphas = [jnp.exp2(m_olds[c] - m_news[c]) for c in range(n)]
    ps = [_block_exp2(ss[c], rows[c], m_news[c]) for c in range(n)]
    pvs = [_pv(kvs[c], ps[c]) for c in range(n)]
    acc_news = [alphas[c] * acc_olds[c] + pvs[c] for c in range(n)]
    for c in range(n):
        m_scs[c][...] = m_news[c]
        acc_scs[c][...] = acc_news[c]


def _flash_first_multi(kvs, ss, rows, m_scs, acc_scs):
    n = len(kvs)
    ms = [jnp.maximum(_block_max(ss[c], rows[c]), M_FLOOR) for c in range(n)]
    ps = [_block_exp2(ss[c], rows[c], ms[c]) for c in range(n)]
    accs = [_pv(kvs[c], ps[c]) for c in range(n)]
    for c in range(n):
        m_scs[c][...] = ms[c]
        acc_scs[c][...] = accs[c]


def _pv(kv, p):
    tk = kv.shape[0]
    lhs = jnp.concatenate([jnp.ones((ONES_ROWS, tk), BF16), kv.T[HEAD_DIM:2 * HEAD_DIM]], axis=0)
    return _dot(lhs, p.astype(BF16))


def _flash_out(acc):
    l = acc[0:1, :]
    return acc[ONES_ROWS:ACC_ROWS, :] / jnp.where(l > 0, l, 1.0)


def _iota(shape, dim):
    return lax.broadcasted_iota(jnp.int32, shape, dim)


def _causal_bias(tk, tq, base=None):
    ok = _iota((tk, tq), 0) <= _iota((tk, tq), 1)
    return jnp.where(ok, 0.0 if base is None else base, MASKED)


def _rank_select(adj, n_rows, topk):
    j = _iota(adj.shape, 0)
    rank = jnp.zeros(adj.shape, jnp.int32)
    for k in range(n_rows):
        rk = adj[k:k + 1, :]
        beats = (rk > adj) | ((rk == adj) & (j > k))
        rank = rank + beats.astype(jnp.int32)
    return rank < topk


def _nsa_cmp_kernel(*refs):
    n_slab = 2 * NSA_KV // LANES
    x_refs = refs[:n_slab]
    w1k_ref, w2k_ref, pk_ref, w1v_ref, w2v_ref, pv_ref, o_ref = refs[n_slab:]
    per_slab = LANES // HEAD_DIM
    nb, _, n_seg, _ = o_ref.shape
    seq = n_seg * CMP_STRIDE
    for t, x_ref in enumerate(x_refs):
        is_v = t >= n_slab // 2
        w1_ref, w2_ref, p_ref = (w1v_ref, w2v_ref, pv_ref) if is_v else (w1k_ref, w2k_ref, pk_ref)
        lane0 = HEAD_DIM if is_v else 0
        tok = [jnp.concatenate([x_ref[pl.ds(bb * seq + l, n_seg, stride=CMP_STRIDE), :]
                                for bb in range(nb)], axis=0) for l in range(CMP_STRIDE)]
        pp = _dot(p_ref[...], w1_ref[...])
        posb = pp[0:1, :CMP_HIDDEN] + pp[1:2, CMP_HIDDEN:]
        for e in range(per_slab):
            g = (t % (n_slab // 2)) * per_slab + e
            x = jnp.concatenate([tk_[:, e * HEAD_DIM:(e + 1) * HEAD_DIM] for tk_ in tok],
                                axis=1).astype(BF16)
            acc = _dot(x, w1_ref[...])
            hid = acc[:, :CMP_HIDDEN] + pltpu.roll(acc[:, CMP_HIDDEN:], nb * n_seg - 1, 0) + posb
            y = _dot(jax.nn.gelu(hid).astype(BF16), w2_ref[...])
            for bb in range(nb):
                o_ref[bb, g, :, lane0:lane0 + HEAD_DIM] = (
                    y[bb * n_seg:(bb + 1) * n_seg].astype(o_ref.dtype))


def _nsa_cmp(cmp_in, batch, seq, w1k, w2k, pk, w1v, w2v, pv):
    full = lambda a: pl.BlockSpec(a.shape, lambda i: (0,) * a.ndim)
    n_seg = seq // CMP_STRIDE
    n_slab = 2 * NSA_KV // LANES
    nb = max(d for d in (1, 2, 4) if batch % d == 0)
    slabs = [pl.BlockSpec((nb * seq, LANES), functools.partial(lambda i, t: (i, t), t=t))
             for t in range(n_slab)]
    return pl.pallas_call(
        _nsa_cmp_kernel,
        grid=(batch // nb,),
        in_specs=slabs + [full(w1k), full(w2k), full(pk), full(w1v), full(w2v), full(pv)],
        out_specs=pl.BlockSpec((nb, NSA_GROUPS, n_seg, 2 * HEAD_DIM), lambda i: (i, 0, 0, 0)),
        out_shape=jax.ShapeDtypeStruct((batch, NSA_GROUPS, n_seg, 2 * HEAD_DIM), BF16),
        compiler_params=_cparams(("parallel",)),
        name="nsa_cmp",
    )(*([cmp_in] * n_slab), w1k, w2k, pk, w1v, w2v, pv)


def _nsa_attn_kernel(q_ref, kvs_ref, kvw_ref, kvc_ref, gt_ref, o_ref,
                     m_sc, acc_sc, selb_sc, out_sc, *, seq, ng):
    tq, tk = ATT_TQ, ATT_TK
    i = pl.program_id(2)
    q0 = i * tq
    n_slc = seq // SLC_BLOCK
    n_cmp_pad = seq // CMP_STRIDE
    rep = NSA_REP
    sub = tk // SLC_BLOCK
    win_rows = WINDOW + tq
    groups = range(ng)
    lanes = [slice(g * LANES, (g + 1) * LANES) for g in groups]

    def heads(x):
        return jnp.concatenate([x] * rep, axis=1)

    qt = q_ref[...].astype(F32).T
    zpad = jnp.zeros((HEAD_DIM, tq), F32)
    qps = [jnp.concatenate(
        [jnp.concatenate([qt[(g * rep + r) * HEAD_DIM:(g * rep + r + 1) * HEAD_DIM], zpad], axis=0)
         for r in range(rep)], axis=1).astype(BF16) for g in groups]
    gates = jax.nn.sigmoid(gt_ref[...]).T

    def emit(g, branch, o_t, first):
        for r in range(rep):
            row = branch * NSA_HEADS + g * rep + r
            val = gates[row:row + 1, :] * o_t[:, r * tq:(r + 1) * tq]
            sl = slice((g * rep + r) * HEAD_DIM, (g * rep + r + 1) * HEAD_DIM)
            if first:
                out_sc[sl, :] = val
            else:
                out_sc[sl, :] += val

    w0 = pl.multiple_of(jnp.maximum(q0 - WINDOW, 0), tk)
    d0 = pl.multiple_of(q0, tk)
    kvcs = [kvc_ref[0, g] for g in groups]
    kvws = [kvw_ref[pl.ds(w0, win_rows), lanes[g]] for g in groups]
    kvds = [kvs_ref[pl.ds(d0, tk), lanes[g]] for g in groups]
    s_cs = [_dot(kvcs[g], qps[g]) for g in groups]
    s_ws = [_dot(kvws[g], qps[g]) for g in groups]

    c_end = _iota((n_cmp_pad, tq), 0) * CMP_STRIDE + (CMP_BLOCK - 1)
    c_ok = (c_end <= q0 + _iota((n_cmp_pad, tq), 1)) & (c_end < seq)
    c_bias = heads(jnp.where(c_ok, 0.0, MASKED))
    k_pos = w0 + _iota((win_rows, tq), 0)
    t_pos = q0 + _iota((win_rows, tq), 1)
    w_bias = heads(jnp.where((k_pos <= t_pos) & (k_pos > t_pos - WINDOW), 0.0, MASKED))
    causal = _causal_bias(tk, tq)

    jj = _iota((n_slc, n_cmp_pad), 0) * SLC_BLOCK
    cc = _iota((n_slc, n_cmp_pad), 1) * CMP_STRIDE
    overlap_t = jnp.where((cc < jj + SLC_BLOCK) & (cc + CMP_BLOCK > jj) & (cc + CMP_BLOCK <= seq),
                          1.0, 0.0).astype(BF16)
    j_blk = _iota((n_slc, tq), 0)
    cur = (q0 + _iota((n_slc, tq), 1)) // SLC_BLOCK
    forced = (j_blk == 0) | (j_blk == cur) | (j_blk == cur - 1)
    valid = j_blk <= cur

    ss = [s_cs[g] + c_bias for g in groups]
    ms = [jnp.maximum(jnp.max(s, axis=0, keepdims=True), M_FLOOR) for s in ss]
    es = [jnp.exp2(ss[g] - ms[g]) for g in groups]
    ds = [jnp.sum(e, axis=0, keepdims=True) for e in es]
    p_cs = [es[g] / jnp.where(ds[g] > 0, ds[g], 1.0) for g in groups]
    o_cs = [_pv(kvcs[g], p_cs[g])[ONES_ROWS:ACC_ROWS, :] for g in groups]

    ss = [s_ws[g] + w_bias for g in groups]
    ms = [jnp.max(s, axis=0, keepdims=True) for s in ss]
    o_ws = [_flash_out(_pv(kvws[g], jnp.exp2(ss[g] - ms[g]))) for g in groups]

    selbs = []
    for g in groups:
        p_sum = p_cs[g][:, 0:tq]
        for r in range(1, rep):
            p_sum = p_sum + p_cs[g][:, r * tq:(r + 1) * tq]
        p_hi = p_sum.astype(BF16)
        p_lo = (p_sum - p_hi.astype(F32)).astype(BF16)
        imp = _dot(overlap_t, p_hi) + _dot(overlap_t, p_lo)
        adj = jnp.where(forced, jnp.inf, jnp.where(valid, imp, -jnp.inf))
        sel = valid & _rank_select(adj, n_slc, min(SLC_TOPK, n_slc))
        selbs.append(jnp.where(sel, 0.0, MASKED))

    for g in groups:
        selb_sc[g] = selbs[g]
        emit(g, 0, o_cs[g], True)
        emit(g, 2, o_ws[g], False)

    def sel_rows(g, blk0):
        return [heads(selb_sc[g, pl.ds(blk0 + a, 1), :]) for a in range(sub)]

    m_scs = [m_sc.at[g] for g in groups]
    acc_scs = [acc_sc.at[g] for g in groups]

    causal_h = heads(causal)
    s_ds = [_dot(kvds[g], qps[g]) for g in groups]
    _flash_first_multi(kvds, [s_ds[g] + causal_h for g in groups],
                       [sel_rows(g, i * sub) for g in groups], m_scs, acc_scs)

    def slc_body(kt, carry):
        k0 = pl.multiple_of(kt * tk, tk)
        _flash_step_multi([kvs_ref[pl.ds(k0, tk), lanes[g]] for g in groups], qps,
                          [sel_rows(g, kt * sub) for g in groups], m_scs, acc_scs)
        return carry

    lax.fori_loop(0, i, slc_body, 0)
    for g in groups:
        emit(g, 1, _flash_out(acc_scs[g][...]), False)

    o_ref[...] = out_sc[...].T.astype(o_ref.dtype)


def _nsa_attn(main, kvc, gates, batch, seq, ng=4):
    tq = ATT_TQ
    assert ATT_TK == tq and WINDOW % tq == 0 and seq % tq == 0 and seq >= WINDOW + tq
    assert ng == NSA_GROUPS and tq % SLC_BLOCK == 0
    nq = seq // tq
    t = batch * seq
    gw = ng * NSA_REP * HEAD_DIM
    kvw = ng * LANES
    assert (NSA_HEADS * HEAD_DIM) % kvw == 0
    kv0 = NSA_HEADS * HEAD_DIM // kvw
    m = NSA_REP * tq
    n_cmp_pad = seq // CMP_STRIDE
    return pl.pallas_call(
        functools.partial(_nsa_attn_kernel, seq=seq, ng=ng),
        grid=(batch, NSA_GROUPS // ng, nq),
        in_specs=[
            pl.BlockSpec((tq, gw), lambda b, g, i: (b * nq + i, g)),
            pl.BlockSpec((seq, kvw), lambda b, g, i: (b, kv0 + g)),
            pl.BlockSpec((seq, kvw), lambda b, g, i: (b, kv0 + NSA_GROUPS // ng + g)),
            pl.BlockSpec((1, ng, n_cmp_pad, LANES), lambda b, g, i: (b, g, 0, 0)),
            pl.BlockSpec((tq, LANES), lambda b, g, i: (b * nq + i, 0)),
        ],
        out_specs=pl.BlockSpec((tq, gw), lambda b, g, i: (b * nq + i, g)),
        out_shape=jax.ShapeDtypeStruct((t, NSA_HEADS * HEAD_DIM), BF16),
        scratch_shapes=[pltpu.VMEM((ng, 1, m), F32),
                        pltpu.VMEM((ng, ACC_ROWS, m), F32),
                        pltpu.VMEM((ng, seq // SLC_BLOCK, tq), F32),
                        pltpu.VMEM((gw, tq), F32)],
        compiler_params=_cparams(("parallel", "parallel", "arbitrary")),
        name="nsa_attn",
    )(main, main, main, kvc, gates)


def _interleave_heads(wa, wb, n):
    d = wa.shape[0]
    return jnp.stack([wa.reshape(d, n, HEAD_DIM), wb.reshape(d, n, HEAD_DIM)],
                     axis=2).reshape(d, 2 * n * HEAD_DIM)


def _nsa_mixer(h, gain, w_in, pos_k, pos_v, k_w1, k_w2, v_w1, v_w2, batch, seq, cast, layer):
    nh = NSA_HEADS * HEAD_DIM
    c = [nh + i * NSA_KV for i in range(7)]
    wq = w_in[:, :c[0]]
    wkc, wvc, wks, wvs, wkw, wvw = (w_in[:, c[i]:c[i + 1]] for i in range(6))
    w_gate = w_in[:, c[6]:]
    w_main = jnp.concatenate(
        [wq, _interleave_heads(wks, wvs, NSA_GROUPS), _interleave_heads(wkw, wvw, NSA_GROUPS)],
        axis=1).astype(BF16)
    w_cmp = jnp.concatenate([wkc, wvc], axis=1).astype(BF16)
    wg = jnp.pad(w_gate, ((0, 0), (0, LANES - 3 * NSA_HEADS))).astype(BF16)

    (main, gates, cmp_in), casted = _proj(h, gain, _q_scale_row(nh, w_main.shape[1]),
                                          [w_main, wg, w_cmp], [BF16, F32, F32], cast, layer)

    half = CMP_STRIDE * HEAD_DIM

    def seg_weights(w1):
        return jnp.concatenate([w1[:half], w1[half:]], axis=1).astype(BF16)

    def pos_rows(p):
        rows = p.reshape(2, half)
        return jnp.pad(rows, ((0, SUBLANES - 2), (0, 0))).astype(BF16)

    kvc = _nsa_cmp(cmp_in, batch, seq, seg_weights(k_w1), k_w2.astype(BF16), pos_rows(pos_k),
                   seg_weights(v_w1), v_w2.astype(BF16), pos_rows(pos_v))
    return _nsa_attn(main, kvc, gates, batch, seq), casted


def _moba_attn_kernel(q_ref, kv_ref, o_ref, m_sc, acc_sc, selb_sc, kmean_sc, *, seq, nh):
    tq, tk = ATT_TQ, ATT_TK
    i = pl.program_id(2)
    q0 = i * tq
    n_blk = seq // MOBA_BLOCK
    heads = range(nh)
    lanes = [slice(h * LANES, (h + 1) * LANES) for h in heads]

    @pl.when(i == 0)
    def _():
        for h in heads:
            for j in range(n_blk):
                blk = kv_ref[j * MOBA_BLOCK:(j + 1) * MOBA_BLOCK, lanes[h]].astype(F32)
                kmean_sc[h, j:j + 1, :] = jnp.sum(blk, axis=0, keepdims=True) * (1.0 / MOBA_BLOCK)

    qt = q_ref[...].astype(F32).T
    zpad = jnp.zeros((HEAD_DIM, tq), F32)
    qps = [jnp.concatenate([qt[h * HEAD_DIM:(h + 1) * HEAD_DIM], zpad], axis=0).astype(BF16)
           for h in heads]
    d0 = pl.multiple_of(q0, tk)
    kvds = [kv_ref[pl.ds(d0, tk), lanes[h]] for h in heads]
    s_ds = [_dot(kvds[h], qps[h]) for h in heads]
    past = _iota((n_blk, tq), 0) < i
    for h in heads:
        score = _dot(kmean_sc[h].astype(BF16), qps[h])
        adj = jnp.where(past, score, -jnp.inf)
        sel = past & _rank_select(adj, n_blk, MOBA_TOPK)
        selb_sc[h] = jnp.where(sel, 0.0, MASKED)

    m_scs = [m_sc.at[h] for h in heads]
    acc_scs = [acc_sc.at[h] for h in heads]

    causal = _causal_bias(tk, tq)
    _flash_first_multi(kvds, [s_ds[h] + causal for h in heads], [None] * nh, m_scs, acc_scs)

    def body(kt, carry):
        k0 = pl.multiple_of(kt * tk, tk)
        _flash_step_multi([kv_ref[pl.ds(k0, tk), lanes[h]] for h in heads], qps,
                          [[selb_sc[h, pl.ds(kt, 1), :]] for h in heads], m_scs, acc_scs)
        return carry

    lax.fori_loop(0, i, body, 0)
    outs =[_flash_out(acc_scs[h][...]) for h in heads]
    o_ref[...] = jnp.concatenate(outs, axis=0).T.astype(o_ref.dtype)


def _moba_attn(q, kv, batch, seq, nh=16):
    tq = ATT_TQ
    assert ATT_TK == tq and MOBA_BLOCK == tq and seq % tq == 0 and MOBA_HEADS % nh == 0
    nq = seq // tq
    t = batch * seq
    d_heads = MOBA_HEADS * HEAD_DIM
    n_blk = seq // MOBA_BLOCK
    qw = nh * HEAD_DIM
    kvw = nh * LANES
    return pl.pallas_call(
        functools.partial(_moba_attn_kernel, seq=seq, nh=nh),
        grid=(batch, MOBA_HEADS // nh, nq),
        in_specs=[
            pl.BlockSpec((tq, qw), lambda b, hg, i: (b * nq + i, hg)),
            pl.BlockSpec((seq, kvw), lambda b, hg, i: (b, hg)),
        ],
        out_specs=pl.BlockSpec((tq, qw), lambda b, hg, i: (b * nq + i, hg)),
        out_shape=jax.ShapeDtypeStruct((t, d_heads), BF16),
        scratch_shapes=[pltpu.VMEM((nh, 1, tq), F32),
                        pltpu.VMEM((nh, ACC_ROWS, tq), F32),
                        pltpu.VMEM((nh, n_blk, tq), F32),
                        pltpu.VMEM((nh, n_blk, LANES), F32)],
        compiler_params=_cparams(("parallel", "parallel", "arbitrary")),
        name="moba_attn",
    )(q, kv)


def _moba_mixer(h, gain, w_in, batch, seq, cast, layer):
    nh = MOBA_HEADS * HEAD_DIM
    wq = w_in[:, :nh]
    wk = w_in[:, nh:2 * nh]
    wv = w_in[:, 2 * nh:]
    w_kv = _interleave_heads(wk, wv, MOBA_HEADS).astype(BF16)
    (q, kv), casted = _proj(h, gain, _q_scale_row(nh, nh), [wq.astype(BF16), w_kv], [BF16, BF16],
                            cast, layer)
    return _moba_attn(q, kv, batch, seq), casted


def _conv_kernel(*refs, n_cast):
    h_ref, g_ref, win_ref, cw_ref = refs[:4]
    c_refs = refs[4:4 + n_cast]
    o_ref = refs[4 + n_cast]
    ztail_sc = refs[-1]
    _cast_blocks(c_refs, refs[5 + n_cast:5 + 2 * n_cast])
    i = pl.program_id(1)
    d = D_MODEL
    a = _rms(h_ref[...], g_ref[...]).astype(BF16)
    ts = a.shape[0]

    @pl.when(i == 0)
    def _():
        ztail_sc[...] = jnp.zeros_like(ztail_sc)

    row = _iota((ts, MXU_TILE), 0)
    for c0 in range(0, d, MXU_TILE):
        cols = slice(c0, c0 + MXU_TILE)
        b_gate = _dot(a, win_ref[:, c0:c0 + MXU_TILE])
        z = (_dot(a, win_ref[:, d + c0:d + c0 + MXU_TILE])
             * _dot(a, win_ref[:, 2 * d + c0:2 * d + c0 + MXU_TILE]))
        zp = ztail_sc[:, cols]
        ztail_sc[:, cols] = z[ts - SUBLANES:ts]
        z1 = jnp.where(row == 0, zp[7:8], pltpu.roll(z, 1, 0))
        z2 = jnp.where(row == 0, zp[6:7], jnp.where(row == 1, zp[7:8], pltpu.roll(z, 2, 0)))
        y = cw_ref[0:1, cols] * z2 + cw_ref[1:2, cols] * z1 + cw_ref[2:3, cols] * z
        o_ref[:, cols] = (b_gate * y).astype(o_ref.dtype)


def _conv_mixer(h, gain, w_in, conv_w, batch, seq, cast, layer, ts=512):
    t = batch * seq
    ns = seq // ts
    assert CONV_WIDTH - 1 <= SUBLANES
    cw = jnp.pad(conv_w, ((0, SUBLANES - CONV_WIDTH), (0, 0)))
    c_in, c_out, c_shapes = _cast_rider(cast, layer, batch * ns, lambda b, i: b * ns + i)
    outs = pl.pallas_call(
        functools.partial(_conv_kernel, n_cast=len(cast)),
        grid=(batch, ns),
        in_specs=[
            pl.BlockSpec((ts, D_MODEL), lambda b, i: (b * ns + i, 0)),
            pl.BlockSpec((1, D_MODEL), lambda b, i: (0, 0)),
            pl.BlockSpec(w_in.shape, lambda b, i: (0, 0)),
            pl.BlockSpec(cw.shape, lambda b, i: (0, 0)),
        ] + c_in,
        out_specs=[pl.BlockSpec((ts, D_MODEL), lambda b, i: (b * ns + i, 0))] + c_out,
        out_shape=[jax.ShapeDtypeStruct((t, D_MODEL), BF16)] + c_shapes,
        scratch_shapes=[pltpu.VMEM((SUBLANES, D_MODEL), F32)],
        compiler_params=_cparams(("arbitrary", "arbitrary")),
        name="conv_mix",
    )(h, gain.reshape(1, D_MODEL), w_in.astype(BF16), cw, *cast)
    return outs[0], outs[1:]


def kernel(x, norm_mix, norm_ffn, norm_final, ffn_w_gate, ffn_w_up, ffn_w_down, nsa_w_in, nsa_w_out, nsa_cmp_pos_k, nsa_cmp_pos_v, nsa_cmp_k_w1, nsa_cmp_k_w2, nsa_cmp_v_w1, nsa_cmp_v_w2, moba_w_in, moba_w_out, conv_w_in, conv_w, conv_w_out):
    batch, seq, d = x.shape
    depth = norm_mix.shape[0]
    h = x.reshape(batch * seq, d)
    ffn_f32 = [ffn_w_gate, ffn_w_up, ffn_w_down]
    for i in range(depth):
        kind, j = i % N_MIXERS, i // N_MIXERS
        if kind == 0:
            o, ffn_w = _nsa_mixer(h, norm_mix[i], nsa_w_in[j], nsa_cmp_pos_k[j], nsa_cmp_pos_v[j],
                                  nsa_cmp_k_w1[j], nsa_cmp_k_w2[j], nsa_cmp_v_w1[j], nsa_cmp_v_w2[j],
                                  batch, seq, ffn_f32, i)
            wo = nsa_w_out[j]
        elif kind == 1:
            o, ffn_w = _moba_mixer(h, norm_mix[i], moba_w_in[j], batch, seq, ffn_f32, i)
            wo = moba_w_out[j]
        else:
            o, ffn_w = _conv_mixer(h, norm_mix[i], conv_w_in[j], conv_w[j], batch, seq, ffn_f32, i)
            wo = conv_w_out[j]
        wg, wu, wd = ffn_w
        h = _ffn(h, o, wo.astype(BF16), norm_ffn[i], wg, wu, wd, norm_final,
                 final_norm=(i == depth - 1))
    return h.reshape(batch, seq, d)
```

```python
import functools

import jax
import jax.numpy as jnp
from jax import lax
from jax.experimental import pallas as pl
from jax.experimental.pallas import tpu as pltpu

F32 = jnp.float32
BF16 = jnp.bfloat16

D_MODEL = 1024
HEAD_DIM = 64
RMS_EPS = 1e-6
N_MIXERS = 3

NSA_HEADS = 16
NSA_GROUPS = 4
NSA_REP = NSA_HEADS // NSA_GROUPS
NSA_KV = NSA_GROUPS * HEAD_DIM
CMP_BLOCK = 32
CMP_STRIDE = 16
CMP_HIDDEN = 256
SLC_BLOCK = 64
SLC_TOPK = 16
WINDOW = 512

MOBA_HEADS = 16
MOBA_BLOCK = 256
MOBA_TOPK = 3

CONV_WIDTH = 3

LANES = 128
SUBLANES = 8
BF16_SUBLANES = 16
MXU_TILE = 256
VMEM_LIMIT = 56 * 1024 * 1024
MASKED = -1e30
M_FLOOR = -1e29
LOG2E = 1.4426950408889634

ONES_ROWS = 16
ACC_ROWS = ONES_ROWS + HEAD_DIM

ATT_TQ = 256
ATT_TK = 256


def _dot(a, b):
    return jnp.dot(a, b, preferred_element_type=F32)


def _rms(x, g):
    y = x * lax.rsqrt(jnp.mean(x * x, axis=-1, keepdims=True) + RMS_EPS)
    return y * g


def _cparams(sem):
    return pltpu.CompilerParams(dimension_semantics=sem, vmem_limit_bytes=VMEM_LIMIT)


def _cast_rider(stacked, layer, n_steps, step_of):
    in_specs, out_specs, out_shapes = [], [], []
    for w in stacked:
        _, r, c = w.shape
        rb = next(b for b in range(BF16_SUBLANES, r + 1, BF16_SUBLANES)
                  if r % b == 0 and r // b <= n_steps)
        last = r // rb - 1
        out_idx = lambda *g, last=last: (jnp.minimum(step_of(*g), last), 0)
        in_idx = lambda *g, last=last: (layer, jnp.minimum(step_of(*g), last), 0)
        in_specs.append(pl.BlockSpec((None, rb, c), in_idx))
        out_specs.append(pl.BlockSpec((rb, c), out_idx))
        out_shapes.append(jax.ShapeDtypeStruct((r, c), BF16))
    return in_specs, out_specs, out_shapes


def _cast_blocks(w_refs, o_refs):
    for w_ref, o_ref in zip(w_refs, o_refs):
        o_ref[...] = w_ref[...].astype(o_ref.dtype)


def _proj_kernel(*refs, n_out, n_cast, chunk):
    x_ref, g_ref, sc_ref = refs[0], refs[1], refs[2]
    w_refs = refs[3:3 + n_out]
    c_refs = refs[3 + n_out:3 + n_out + n_cast]
    o_refs = refs[3 + n_out + n_cast:3 + 2 * n_out + n_cast]
    _cast_blocks(c_refs, refs[3 + 2 * n_out + n_cast:])
    a = _rms(x_ref[...], g_ref[...]).astype(BF16)
    for k, (w_ref, o_ref) in enumerate(zip(w_refs, o_refs)):
        n = w_ref.shape[1]
        for n0 in range(0, n, chunk):
            n1 = min(n0 + chunk, n)
            y = _dot(a, w_ref[:, n0:n1])
            if k == 0:
                y = y * sc_ref[:, n0:n1]
            o_ref[:, n0:n1] = y.astype(o_ref.dtype)


def _proj(h, gain, scale0, weights, out_dtypes, cast, layer, tm=1024):
    t = h.shape[0]
    n_out = len(weights)
    n_steps = t // tm
    const = lambda a: pl.BlockSpec(a.shape, lambda i: (0, 0))
    gain = gain.reshape(1, D_MODEL)
    c_in, c_out, c_shapes = _cast_rider(cast, layer, n_steps, lambda i: i)
    in_specs = [pl.BlockSpec((tm, D_MODEL), lambda i: (i, 0)), const(gain), const(scale0)]
    in_specs += [const(w) for w in weights] + c_in
    out_specs = [pl.BlockSpec((tm, w.shape[1]), lambda i: (i, 0)) for w in weights] + c_out
    out_shape = [jax.ShapeDtypeStruct((t, w.shape[1]), dt) for w, dt in zip(weights, out_dtypes)]
    outs = pl.pallas_call(
        functools.partial(_proj_kernel, n_out=n_out, n_cast=len(cast), chunk=512),
        grid=(n_steps,),
        in_specs=in_specs, out_specs=out_specs, out_shape=out_shape + c_shapes,
        compiler_params=_cparams(("arbitrary",)),
        name="proj",
    )(h, gain, scale0, *weights, *cast)
    return outs[:n_out], outs[n_out:]


def _q_scale_row(n_q, n_total):
    q = jnp.full((1, n_q), (HEAD_DIM ** -0.5) * LOG2E, F32)
    return jnp.concatenate([q, jnp.ones((1, n_total - n_q), F32)], axis=1)


def _ffn_kernel(h_ref, o_ref, wo_ref, g_ref, wg_ref, wu_ref, wd_ref, gf_ref, out_ref,
                *, final_norm, chunks):
    h1 = h_ref[...] + _dot(o_ref[...], wo_ref[...])
    a = _rms(h1, g_ref[...]).astype(BF16)
    out_ref[...] = h1
    for c0, c1 in chunks:
        gate = _dot(a, wg_ref[:, c0:c1])
        up = _dot(a, wu_ref[:, c0:c1])
        t = (jax.nn.silu(gate) * up).astype(BF16)
        out_ref[...] += _dot(t, wd_ref[c0:c1, :])
    if final_norm:
        out_ref[...] = _rms(out_ref[...], gf_ref[...])


def _ffn(h, o, wo, gain, wg, wu, wd, final_gain, final_norm, tm=512):
    t = h.shape[0]
    d_ff = wg.shape[1]
    step = 3 * MXU_TILE
    chunks = tuple((c0, min(c0 + step, d_ff)) for c0 in range(0, d_ff, step))
    assert d_ff % MXU_TILE == 0
    gain = gain.reshape(1, D_MODEL)
    final_gain = final_gain.reshape(1, D_MODEL)
    const = lambda a: pl.BlockSpec(a.shape, lambda i: (0, 0), pipeline_mode=pl.Buffered(1))
    return pl.pallas_call(
        functools.partial(_ffn_kernel, final_norm=final_norm, chunks=chunks),
        grid=(t // tm,),
        in_specs=[
            pl.BlockSpec((tm, D_MODEL), lambda i: (i, 0)),
            pl.BlockSpec((tm, o.shape[1]), lambda i: (i, 0)),
            const(wo), const(gain), const(wg), const(wu), const(wd), const(final_gain),
        ],
        out_specs=pl.BlockSpec((tm, D_MODEL), lambda i: (i, 0)),
        out_shape=jax.ShapeDtypeStruct((t, D_MODEL), F32),
        compiler_params=_cparams(("parallel",)),
        name="ffn",
    )(h, o, wo, gain, wg, wu, wd, final_gain)


def _block_max(s, rows):
    if rows is None:
        return jnp.max(s, axis=0, keepdims=True)
    h = s.shape[0] // len(rows)
    ms = [jnp.max(s[a * h:(a + 1) * h].reshape(h // SUBLANES, SUBLANES, s.shape[1]), axis=0) + row
          for a, row in enumerate(rows)]
    return jnp.max(functools.reduce(jnp.maximum, ms), axis=0, keepdims=True)


def _block_exp2(s, rows, m):
    if rows is None:
        return jnp.exp2(s - m)
    h = s.shape[0] // len(rows)
    return jnp.concatenate([jnp.exp2(s[a * h:(a + 1) * h] - (m - row))
                            for a, row in enumerate(rows)], axis=0)


def _flash_step_multi(kvs, qps, rows, m_scs, acc_scs):
    n = len(kvs)
    ss = [_dot(kvs[c], qps[c]) for c in range(n)]
    m_olds = [m_scs[c][...] for c in range(n)]
    acc_olds = [acc_scs[c][...] for c in range(n)]
    m_news = [jnp.maximum(m_olds[c], _block_max(ss[c], rows[c])) for c in range(n)]
    alphas = [jnp.exp2(m_olds[c] - m_news[c]) for c in range(n)]
    ps = [_block_exp2(ss[c], rows[c], m_news[c]) for c in range(n)]
    pvs = [_pv(kvs[c], ps[c]) for c in range(n)]
    acc_news = [alphas[c] * acc_olds[c] + pvs[c] for c in range(n)]
    for c in range(n):
        m_scs[c][...] = m_news[c]
        acc_scs[c][...] = acc_news[c]


def _flash_first_multi(kvs, ss, rows, m_scs, acc_scs):
    n = len(kvs)
    ms = [jnp.maximum(_block_max(ss[c], rows[c]), M_FLOOR) for c in range(n)]
    ps = [_block_exp2(ss[c], rows[c], ms[c]) for c in range(n)]
    accs = [_pv(kvs[c], ps[c]) for c in range(n)]
    for c in range(n):
        m_scs[c][...] = ms[c]
        acc_scs[c][...] = accs[c]


def _pv(kv, p):
    tk = kv.shape[0]
    lhs = jnp.concatenate([jnp.ones((ONES_ROWS, tk), BF16), kv.T[HEAD_DIM:2 * HEAD_DIM]], axis=0)
    return _dot(lhs, p.astype(BF16))


def _flash_out(acc):
    l = acc[0:1, :]
    return acc[ONES_ROWS:ACC_ROWS, :] / jnp.where(l > 0, l, 1.0)


def _iota(shape, dim):
    return lax.broadcasted_iota(jnp.int32, shape, dim)


def _causal_bias(tk, tq):
    ok = _iota((tk, tq), 0) <= _iota((tk, tq), 1)
    return jnp.where(ok, 0.0, MASKED)


def _rank_select(adj, n_rows, topk):
    j = _iota(adj.shape, 0)
    rank = jnp.zeros(adj.shape, jnp.int32)
    for k in range(n_rows):
        rk = adj[k:k + 1, :]
        beats = (rk > adj) | ((rk == adj) & (j > k))
        rank = rank + beats.astype(jnp.int32)
    return rank < topk


def _nsa_cmp_kernel(*refs):
    n_slab = 2 * NSA_KV // LANES
    x_refs = refs[:n_slab]
    w1k_ref, w2k_ref, pk_ref, w1v_ref, w2v_ref, pv_ref, o_ref = refs[n_slab:]
    per_slab = LANES // HEAD_DIM
    nb, _, n_seg, _ = o_ref.shape
    seq = n_seg * CMP_STRIDE
    for t, x_ref in enumerate(x_refs):
        is_v = t >= n_slab // 2
        w1_ref, w2_ref, p_ref = (w1v_ref, w2v_ref, pv_ref) if is_v else (w1k_ref, w2k_ref, pk_ref)
        lane0 = HEAD_DIM if is_v else 0
        tok = [jnp.concatenate([x_ref[pl.ds(bb * seq + l, n_seg, stride=CMP_STRIDE), :]
                                for bb in range(nb)], axis=0) for l in range(CMP_STRIDE)]
        pp = _dot(p_ref[...], w1_ref[...])
        posb = pp[0:1, :CMP_HIDDEN] + pp[1:2, CMP_HIDDEN:]
        for e in range(per_slab):
            g = (t % (n_slab // 2)) * per_slab + e
            x = jnp.concatenate([tk_[:, e * HEAD_DIM:(e + 1) * HEAD_DIM] for tk_ in tok],
                                axis=1).astype(BF16)
            acc = _dot(x, w1_ref[...])
            hid = acc[:, :CMP_HIDDEN] + pltpu.roll(acc[:, CMP_HIDDEN:], nb * n_seg - 1, 0) + posb
            y = _dot(jax.nn.gelu(hid).astype(BF16), w2_ref[...])
            for bb in range(nb):
                o_ref[bb, g, :, lane0:lane0 + HEAD_DIM] = (
                    y[bb * n_seg:(bb + 1) * n_seg].astype(o_ref.dtype))


def _nsa_cmp(cmp_in, batch, seq, w1k, w2k, pk, w1v, w2v, pv):
    full = lambda a: pl.BlockSpec(a.shape, lambda i: (0,) * a.ndim)
    n_seg = seq // CMP_STRIDE
    n_slab = 2 * NSA_KV // LANES
    nb = max(d for d in (1, 2, 4) if batch % d == 0)
    slabs = [pl.BlockSpec((nb * seq, LANES), functools.partial(lambda i, t: (i, t), t=t))
             for t in range(n_slab)]
    return pl.pallas_call(
        _nsa_cmp_kernel,
        grid=(batch // nb,),
        in_specs=slabs + [full(w1k), full(w2k), full(pk), full(w1v), full(w2v), full(pv)],
        out_specs=pl.BlockSpec((nb, NSA_GROUPS, n_seg, 2 * HEAD_DIM), lambda i: (i, 0, 0, 0)),
        out_shape=jax.ShapeDtypeStruct((batch, NSA_GROUPS, n_seg, 2 * HEAD_DIM), BF16),
        compiler_params=_cparams(("parallel",)),
        name="nsa_cmp",
    )(*([cmp_in] * n_slab), w1k, w2k, pk, w1v, w2v, pv)


def _nsa_attn_kernel(q_ref, kvs_ref, kvw_ref, kvc_ref, gt_ref, o_ref,
                     m_sc, acc_sc, selb_sc, out_sc, *, seq, ng):
    tq, tk = ATT_TQ, ATT_TK
    i = pl.program_id(2)
    q0 = i * tq
    n_slc = seq // SLC_BLOCK
    n_cmp_pad = seq // CMP_STRIDE
    rep = NSA_REP
    sub = tk // SLC_BLOCK
    win_rows = WINDOW + tq
    groups = range(ng)
    lanes = [slice(g * LANES, (g + 1) * LANES) for g in groups]

    def heads(x):
        return jnp.concatenate([x] * rep, axis=1)

    qt = q_ref[...].astype(F32).T
    zpad = jnp.zeros((HEAD_DIM, tq), F32)
    qps = [jnp.concatenate(
        [jnp.concatenate([qt[(g * rep + r) * HEAD_DIM:(g * rep + r + 1) * HEAD_DIM], zpad], axis=0)
         for r in range(rep)], axis=1).astype(BF16) for g in groups]
    gates = jax.nn.sigmoid(gt_ref[...]).T

    def emit(g, branch, o_t, first):
        for r in range(rep):
            row = branch * NSA_HEADS + g * rep + r
            val = gates[row:row + 1, :] * o_t[:, r * tq:(r + 1) * tq]
            sl = slice((g * rep + r) * HEAD_DIM, (g * rep + r + 1) * HEAD_DIM)
            if first:
                out_sc[sl, :] = val
            else:
                out_sc[sl, :] += val

    w0 = pl.multiple_of(jnp.maximum(q0 - WINDOW, 0), tk)
    d0 = pl.multiple_of(q0, tk)
    kvcs = [kvc_ref[0, g] for g in groups]
    kvws = [kvw_ref[pl.ds(w0, win_rows), lanes[g]] for g in groups]
    kvds = [kvs_ref[pl.ds(d0, tk), lanes[g]] for g in groups]
    s_cs = [_dot(kvcs[g], qps[g]) for g in groups]
    s_ws = [_dot(kvws[g], qps[g]) for g in groups]

    c_end = _iota((n_cmp_pad, tq), 0) * CMP_STRIDE + (CMP_BLOCK - 1)
    c_ok = (c_end <= q0 + _iota((n_cmp_pad, tq), 1)) & (c_end < seq)
    c_bias = heads(jnp.where(c_ok, 0.0, MASKED))
    k_pos = w0 + _iota((win_rows, tq), 0)
    t_pos = q0 + _iota((win_rows, tq), 1)
    w_bias = heads(jnp.where((k_pos <= t_pos) & (k_pos > t_pos - WINDOW), 0.0, MASKED))
    causal = _causal_bias(tk, tq)

    jj = _iota((n_slc, n_cmp_pad), 0) * SLC_BLOCK
    cc = _iota((n_slc, n_cmp_pad), 1) * CMP_STRIDE
    overlap_t = jnp.where((cc < jj + SLC_BLOCK) & (cc + CMP_BLOCK > jj) & (cc + CMP_BLOCK <= seq),
                          1.0, 0.0).astype(BF16)
    j_blk = _iota((n_slc, tq), 0)
    cur = (q0 + _iota((n_slc, tq), 1)) // SLC_BLOCK
    forced = (j_blk == 0) | (j_blk == cur) | (j_blk == cur - 1)
    valid = j_blk <= cur

    ss = [s_cs[g] + c_bias for g in groups]
    ms = [jnp.maximum(jnp.max(s, axis=0, keepdims=True), M_FLOOR) for s in ss]
    es = [jnp.exp2(ss[g] - ms[g]) for g in groups]
    ds = [jnp.sum(e, axis=0, keepdims=True) for e in es]
    p_cs = [es[g] / jnp.where(ds[g] > 0, ds[g], 1.0) for g in groups]
    o_cs = [_pv(kvcs[g], p_cs[g])[ONES_ROWS:ACC_ROWS, :] for g in groups]

    ss = [s_ws[g] + w_bias for g in groups]
    ms = [jnp.max(s, axis=0, keepdims=True) for s in ss]
    o_ws = [_flash_out(_pv(kvws[g], jnp.exp2(ss[g] - ms[g]))) for g in groups]

    selbs = []
    for g in groups:
        p_sum = p_cs[g][:, 0:tq]
        for r in range(1, rep):
            p_sum = p_sum + p_cs[g][:, r * tq:(r + 1) * tq]
        p_hi = p_sum.astype(BF16)
        p_lo = (p_sum - p_hi.astype(F32)).astype(BF16)
        imp = _dot(overlap_t, p_hi) + _dot(overlap_t, p_lo)
        adj = jnp.where(forced, jnp.inf, jnp.where(valid, imp, -jnp.inf))
        sel = valid & _rank_select(adj, n_slc, min(SLC_TOPK, n_slc))
        selbs.append(jnp.where(sel, 0.0, MASKED))

    for g in groups:
        selb_sc[g] = selbs[g]
        emit(g, 0, o_cs[g], True)
        emit(g, 2, o_ws[g], False)

    def sel_rows(g, blk0):
        return [heads(selb_sc[g, pl.ds(blk0 + a, 1), :]) for a in range(sub)]

    m_scs = [m_sc.at[g] for g in groups]
    acc_scs = [acc_sc.at[g] for g in groups]

    causal_h = heads(causal)
    s_ds = [_dot(kvds[g], qps[g]) for g in groups]
    _flash_first_multi(kvds, [s_ds[g] + causal_h for g in groups],
                       [sel_rows(g, i * sub) for g in groups], m_scs, acc_scs)

    def slc_body(kt, carry):
        k0 = pl.multiple_of(kt * tk, tk)
        _flash_step_multi([kvs_ref[pl.ds(k0, tk), lanes[g]] for g in groups], qps,
                          [sel_rows(g, kt * sub) for g in groups], m_scs, acc_scs)
        return carry

    lax.fori_loop(0, i, slc_body, 0)
    for g in groups:
        emit(g, 1, _flash_out(acc_scs[g][...]), False)

    o_ref[...] = out_sc[...].T.astype(o_ref.dtype)


def _nsa_attn(main, kvc, gates, batch, seq):
    tq = ATT_TQ
    ng = NSA_GROUPS
    assert ATT_TK == tq and WINDOW % tq == 0 and seq % tq == 0 and seq >= WINDOW + tq
    assert tq % SLC_BLOCK == 0
    nq = seq // tq
    t = batch * seq
    gw = ng * NSA_REP * HEAD_DIM
    kvw = ng * LANES
    assert (NSA_HEADS * HEAD_DIM) % kvw == 0
    kv0 = NSA_HEADS * HEAD_DIM // kvw
    m = NSA_REP * tq
    n_cmp_pad = seq // CMP_STRIDE
    return pl.pallas_call(
        functools.partial(_nsa_attn_kernel, seq=seq, ng=ng),
        grid=(batch, NSA_GROUPS // ng, nq),
        in_specs=[
            pl.BlockSpec((tq, gw), lambda b, g, i: (b * nq + i, g)),
            pl.BlockSpec((seq, kvw), lambda b, g, i: (b, kv0 + g)),
            pl.BlockSpec((seq, kvw), lambda b, g, i: (b, kv0 + NSA_GROUPS // ng + g)),
            pl.BlockSpec((1, ng, n_cmp_pad, LANES), lambda b, g, i: (b, g, 0, 0)),
            pl.BlockSpec((tq, LANES), lambda b, g, i: (b * nq + i, 0)),
        ],
        out_specs=pl.BlockSpec((tq, gw), lambda b, g, i: (b * nq + i, g)),
        out_shape=jax.ShapeDtypeStruct((t, NSA_HEADS * HEAD_DIM), BF16),
        scratch_shapes=[pltpu.VMEM((ng, 1, m), F32),
                        pltpu.VMEM((ng, ACC_ROWS, m), F32),
                        pltpu.VMEM((ng, seq // SLC_BLOCK, tq), F32),
                        pltpu.VMEM((gw, tq), F32)],
        compiler_params=_cparams(("parallel", "parallel", "arbitrary")),
        name="nsa_attn",
    )(main, main, main, kvc, gates)


def _interleave_heads(wa, wb, n):
    d = wa.shape[0]
    return jnp.stack([wa.reshape(d, n, HEAD_DIM), wb.reshape(d, n, HEAD_DIM)],
                     axis=2).reshape(d, 2 * n * HEAD_DIM)


def _nsa_mixer(h, gain, w_in, pos_k, pos_v, k_w1, k_w2, v_w1, v_w2, batch, seq, cast, layer):
    nh = NSA_HEADS * HEAD_DIM
    c = [nh + i * NSA_KV for i in range(7)]
    wq = w_in[:, :c[0]]
    wkc, wvc, wks, wvs, wkw, wvw = (w_in[:, c[i]:c[i + 1]] for i in range(6))
    w_gate = w_in[:, c[6]:]
    w_main = jnp.concatenate(
        [wq, _interleave_heads(wks, wvs, NSA_GROUPS), _interleave_heads(wkw, wvw, NSA_GROUPS)],
        axis=1).astype(BF16)
    w_cmp = jnp.concatenate([wkc, wvc], axis=1).astype(BF16)
    wg = jnp.pad(w_gate, ((0, 0), (0, LANES - 3 * NSA_HEADS))).astype(BF16)

    (main, gates, cmp_in), casted = _proj(h, gain, _q_scale_row(nh, w_main.shape[1]),
                                          [w_main, wg, w_cmp], [BF16, F32, F32], cast, layer)

    half = CMP_STRIDE * HEAD_DIM

    def seg_weights(w1):
        return jnp.concatenate([w1[:half], w1[half:]], axis=1).astype(BF16)

    def pos_rows(p):
        rows = p.reshape(2, half)
        return jnp.pad(rows, ((0, SUBLANES - 2), (0, 0))).astype(BF16)

    kvc = _nsa_cmp(cmp_in, batch, seq, seg_weights(k_w1), k_w2.astype(BF16), pos_rows(pos_k),
                   seg_weights(v_w1), v_w2.astype(BF16), pos_rows(pos_v))
    return _nsa_attn(main, kvc, gates, batch, seq), casted


def _moba_attn_kernel(q_ref, kv_ref, o_ref, m_sc, acc_sc, selb_sc, kmean_sc, *, seq, nh):
    tq, tk = ATT_TQ, ATT_TK
    i = pl.program_id(2)
    q0 = i * tq
    n_blk = seq // MOBA_BLOCK
    heads = range(nh)
    lanes = [slice(h * LANES, (h + 1) * LANES) for h in heads]

    @pl.when(i == 0)
    def _():
        for h in heads:
            for j in range(n_blk):
                blk = kv_ref[j * MOBA_BLOCK:(j + 1) * MOBA_BLOCK, lanes[h]].astype(F32)
                kmean_sc[h, j:j + 1, :] = jnp.sum(blk, axis=0, keepdims=True) * (1.0 / MOBA_BLOCK)

    qt = q_ref[...].astype(F32).T
    zpad = jnp.zeros((HEAD_DIM, tq), F32)
    qps = [jnp.concatenate([qt[h * HEAD_DIM:(h + 1) * HEAD_DIM], zpad], axis=0).astype(BF16)
           for h in heads]
    d0 = pl.multiple_of(q0, tk)
    kvds = [kv_ref[pl.ds(d0, tk), lanes[h]] for h in heads]
    s_ds = [_dot(kvds[h], qps[h]) for h in heads]
    past = _iota((n_blk, tq), 0) < i
    for h in heads:
        score = _dot(kmean_sc[h].astype(BF16), qps[h])
        adj = jnp.where(past, score, -jnp.inf)
        sel = past & _rank_select(adj, n_blk, MOBA_TOPK)
        selb_sc[h] = jnp.where(sel, 0.0, MASKED)

    m_scs = [m_sc.at[h] for h in heads]
    acc_scs = [acc_sc.at[h] for h in heads]

    causal = _causal_bias(tk, tq)
    _flash_first_multi(kvds, [s_ds[h] + causal for h in heads], [None] * nh, m_scs, acc_scs)

    def body(kt, carry):
        k0 = pl.multiple_of(kt * tk, tk)
        _flash_step_multi([kv_ref[pl.ds(k0, tk), lanes[h]] for h in heads], qps,
                          [[selb_sc[h, pl.ds(kt, 1), :]] for h in heads], m_scs, acc_scs)
        return carry

    lax.fori_loop(0, i, body, 0)
    outs =[_flash_out(acc_scs[h][...]) for h in heads]
    o_ref[...] = jnp.concatenate(outs, axis=0).T.astype(o_ref.dtype)


def _moba_attn(q, kv, batch, seq):
    tq = ATT_TQ
    nh = MOBA_HEADS
    assert ATT_TK == tq and MOBA_BLOCK == tq and seq % tq == 0
    nq = seq // tq
    t = batch * seq
    d_heads = MOBA_HEADS * HEAD_DIM
    n_blk = seq // MOBA_BLOCK
    qw = nh * HEAD_DIM
    kvw = nh * LANES
    return pl.pallas_call(
        functools.partial(_moba_attn_kernel, seq=seq, nh=nh),
        grid=(batch, MOBA_HEADS // nh, nq),
        in_specs=[
            pl.BlockSpec((tq, qw), lambda b, hg, i: (b * nq + i, hg)),
            pl.BlockSpec((seq, kvw), lambda b, hg, i: (b, hg)),
        ],
        out_specs=pl.BlockSpec((tq, qw), lambda b, hg, i: (b * nq + i, hg)),
        out_shape=jax.ShapeDtypeStruct((t, d_heads), BF16),
        scratch_shapes=[pltpu.VMEM((nh, 1, tq), F32),
                        pltpu.VMEM((nh, ACC_ROWS, tq), F32),
                        pltpu.VMEM((nh, n_blk, tq), F32),
                        pltpu.VMEM((nh, n_blk, LANES), F32)],
        compiler_params=_cparams(("parallel", "parallel", "arbitrary")),
        name="moba_attn",
    )(q, kv)


def _moba_mixer(h, gain, w_in, batch, seq, cast, layer):
    nh = MOBA_HEADS * HEAD_DIM
    wq = w_in[:, :nh]
    wk = w_in[:, nh:2 * nh]
    wv = w_in[:, 2 * nh:]
    w_kv = _interleave_heads(wk, wv, MOBA_HEADS).astype(BF16)
    (q, kv), casted = _proj(h, gain, _q_scale_row(nh, nh), [wq.astype(BF16), w_kv], [BF16, BF16],
                            cast, layer)
    return _moba_attn(q, kv, batch, seq), casted


def _conv_kernel(*refs, n_cast):
    h_ref, g_ref, win_ref, cw_ref = refs[:4]
    c_refs = refs[4:4 + n_cast]
    o_ref = refs[4 + n_cast]
    ztail_sc = refs[-1]
    _cast_blocks(c_refs, refs[5 + n_cast:5 + 2 * n_cast])
    i = pl.program_id(1)
    d = D_MODEL
    a = _rms(h_ref[...], g_ref[...]).astype(BF16)
    ts = a.shape[0]

    @pl.when(i == 0)
    def _():
        ztail_sc[...] = jnp.zeros_like(ztail_sc)

    row = _iota((ts, MXU_TILE), 0)
    for c0 in range(0, d, MXU_TILE):
        cols = slice(c0, c0 + MXU_TILE)
        b_gate = _dot(a, win_ref[:, c0:c0 + MXU_TILE])
        z = (_dot(a, win_ref[:, d + c0:d + c0 + MXU_TILE])
             * _dot(a, win_ref[:, 2 * d + c0:2 * d + c0 + MXU_TILE]))
        zp = ztail_sc[:, cols]
        ztail_sc[:, cols] = z[ts - SUBLANES:ts]
        z1 = jnp.where(row == 0, zp[7:8], pltpu.roll(z, 1, 0))
        z2 = jnp.where(row == 0, zp[6:7], jnp.where(row == 1, zp[7:8], pltpu.roll(z, 2, 0)))
        y = cw_ref[0:1, cols] * z2 + cw_ref[1:2, cols] * z1 + cw_ref[2:3, cols] * z
        o_ref[:, cols] = (b_gate * y).astype(o_ref.dtype)


def _conv_mixer(h, gain, w_in, conv_w, batch, seq, cast, layer, ts=512):
    t = batch * seq
    ns = seq // ts
    assert CONV_WIDTH - 1 <= SUBLANES
    cw = jnp.pad(conv_w, ((0, SUBLANES - CONV_WIDTH), (0, 0)))
    c_in, c_out, c_shapes = _cast_rider(cast, layer, batch * ns, lambda b, i: b * ns + i)
    outs = pl.pallas_call(
        functools.partial(_conv_kernel, n_cast=len(cast)),
        grid=(batch, ns),
        in_specs=[
            pl.BlockSpec((ts, D_MODEL), lambda b, i: (b * ns + i, 0)),
            pl.BlockSpec((1, D_MODEL), lambda b, i: (0, 0)),
            pl.BlockSpec(w_in.shape, lambda b, i: (0, 0)),
            pl.BlockSpec(cw.shape, lambda b, i: (0, 0)),
        ] + c_in,
        out_specs=[pl.BlockSpec((ts, D_MODEL), lambda b, i: (b * ns + i, 0))] + c_out,
        out_shape=[jax.ShapeDtypeStruct((t, D_MODEL), BF16)] + c_shapes,
        scratch_shapes=[pltpu.VMEM((SUBLANES, D_MODEL), F32)],
        compiler_params=_cparams(("arbitrary", "arbitrary")),
        name="conv_mix",
    )(h, gain.reshape(1, D_MODEL), w_in.astype(BF16), cw, *cast)
    return outs[0], outs[1:]


def kernel(x, norm_mix, norm_ffn, norm_final, ffn_w_gate, ffn_w_up, ffn_w_down, nsa_w_in, nsa_w_out, nsa_cmp_pos_k, nsa_cmp_pos_v, nsa_cmp_k_w1, nsa_cmp_k_w2, nsa_cmp_v_w1, nsa_cmp_v_w2, moba_w_in, moba_w_out, conv_w_in, conv_w, conv_w_out):
    batch, seq, d = x.shape
    depth = norm_mix.shape[0]
    h = x.reshape(batch * seq, d)
    ffn_f32 = [ffn_w_gate, ffn_w_up, ffn_w_down]
    for i in range(depth):
        kind, j = i % N_MIXERS, i // N_MIXERS
        if kind == 0:
            o, ffn_w = _nsa_mixer(h, norm_mix[i], nsa_w_in[j], nsa_cmp_pos_k[j], nsa_cmp_pos_v[j],
                                  nsa_cmp_k_w1[j], nsa_cmp_k_w2[j], nsa_cmp_v_w1[j], nsa_cmp_v_w2[j],
                                  batch, seq, ffn_f32, i)
            wo = nsa_w_out[j]
        elif kind == 1:
            o, ffn_w = _moba_mixer(h, norm_mix[i], moba_w_in[j], batch, seq, ffn_f32, i)
            wo = moba_w_out[j]
        else:
            o, ffn_w = _conv_mixer(h, norm_mix[i], conv_w_in[j], conv_w[j], batch, seq, ffn_f32, i)
            wo = conv_w_out[j]
        wg, wu, wd = ffn_w
        h = _ffn(h, o, wo.astype(BF16), norm_ffn[i], wg, wu, wd, norm_final,
                 final_norm=(i == depth - 1))
    return h.reshape(batch, seq, d)
```

```python
import functools

import jax
import jax.numpy as jnp
from jax import lax
from jax.experimental import pallas as pl
from jax.experimental.pallas import tpu as pltpu

F32 = jnp.float32
BF16 = jnp.bfloat16

D_MODEL = 1024
HEAD_DIM = 64
RMS_EPS = 1e-6
N_MIXERS = 3

NSA_HEADS = 16
NSA_GROUPS = 4
NSA_REP = NSA_HEADS // NSA_GROUPS
NSA_KV = NSA_GROUPS * HEAD_DIM
CMP_BLOCK = 32
CMP_STRIDE = 16
CMP_HIDDEN = 256
SLC_BLOCK = 64
SLC_TOPK = 16
WINDOW = 512

MOBA_HEADS = 16
MOBA_BLOCK = 256
MOBA_TOPK = 3

CONV_WIDTH = 3

LANES = 128
SUBLANES = 8
BF16_SUBLANES = 16
MXU_TILE = 256
VMEM_LIMIT = 56 * 1024 * 1024
MASKED = -1e30
M_FLOOR = -1e29
LOG2E = 1.4426950408889634

ONES_ROWS = 16
ACC_ROWS = ONES_ROWS + HEAD_DIM

ATT_TQ = 256
ATT_TK = 256


def _dot(a, b):
    return jnp.dot(a, b, preferred_element_type=F32)


def _rms(x, g):
    y = x * lax.rsqrt(jnp.mean(x * x, axis=-1, keepdims=True) + RMS_EPS)
    return y * g


def _cparams(sem):
    return pltpu.CompilerParams(dimension_semantics=sem, vmem_limit_bytes=VMEM_LIMIT)


def _cast_rider(stacked, layer, n_steps, step_of):
    in_specs, out_specs, out_shapes = [], [], []
    for w in stacked:
        _, r, c = w.shape
        rb = next(b for b in range(BF16_SUBLANES, r + 1, BF16_SUBLANES)
                  if r % b == 0 and r // b <= n_steps)
        last = r // rb - 1
        out_idx = lambda *g, last=last: (jnp.minimum(step_of(*g), last), 0)
        in_idx = lambda *g, last=last: (layer, jnp.minimum(step_of(*g), last), 0)
        in_specs.append(pl.BlockSpec((None, rb, c), in_idx))
        out_specs.append(pl.BlockSpec((rb, c), out_idx))
        out_shapes.append(jax.ShapeDtypeStruct((r, c), BF16))
    return in_specs, out_specs, out_shapes


def _cast_blocks(w_refs, o_refs):
    for w_ref, o_ref in zip(w_refs, o_refs):
        o_ref[...] = w_ref[...].astype(o_ref.dtype)


def _proj_kernel(*refs, n_out, n_cast, chunk):
    x_ref, g_ref, sc_ref = refs[0], refs[1], refs[2]
    w_refs = refs[3:3 + n_out]
    c_refs = refs[3 + n_out:3 + n_out + n_cast]
    o_refs = refs[3 + n_out + n_cast:3 + 2 * n_out + n_cast]
    _cast_blocks(c_refs, refs[3 + 2 * n_out + n_cast:])
    a = _rms(x_ref[...], g_ref[...]).astype(BF16)
    for k, (w_ref, o_ref) in enumerate(zip(w_refs, o_refs)):
        n = w_ref.shape[1]
        for n0 in range(0, n, chunk):
            n1 = min(n0 + chunk, n)
            y = _dot(a, w_ref[:, n0:n1])
            if k == 0:
                y = y * sc_ref[:, n0:n1]
            o_ref[:, n0:n1] = y.astype(o_ref.dtype)


def _proj(h, gain, scale0, weights, out_dtypes, cast, layer, tm=1024):
    t = h.shape[0]
    n_out = len(weights)
    n_steps = t // tm
    const = lambda a: pl.BlockSpec(a.shape, lambda i: (0, 0))
    gain = gain.reshape(1, D_MODEL)
    c_in, c_out, c_shapes = _cast_rider(cast, layer, n_steps, lambda i: i)
    in_specs = [pl.BlockSpec((tm, D_MODEL), lambda i: (i, 0)), const(gain), const(scale0)]
    in_specs += [const(w) for w in weights] + c_in
    out_specs = [pl.BlockSpec((tm, w.shape[1]), lambda i: (i, 0)) for w in weights] + c_out
    out_shape = [jax.ShapeDtypeStruct((t, w.shape[1]), dt) for w, dt in zip(weights, out_dtypes)]
    outs = pl.pallas_call(
        functools.partial(_proj_kernel, n_out=n_out, n_cast=len(cast), chunk=512),
        grid=(n_steps,),
        in_specs=in_specs, out_specs=out_specs, out_shape=out_shape + c_shapes,
        compiler_params=_cparams(("arbitrary",)),
        name="proj",
    )(h, gain, scale0, *weights, *cast)
    return outs[:n_out], outs[n_out:]


def _q_scale_row(n_q, n_total):
    q = jnp.full((1, n_q), (HEAD_DIM ** -0.5) * LOG2E, F32)
    return jnp.concatenate([q, jnp.ones((1, n_total - n_q), F32)], axis=1)


def _ffn_kernel(h_ref, o_ref, wo_ref, g_ref, wg_ref, wu_ref, wd_ref, gf_ref, out_ref,
                *, final_norm, chunks):
    h1 = h_ref[...] + _dot(o_ref[...], wo_ref[...])
    a = _rms(h1, g_ref[...]).astype(BF16)
    out_ref[...] = h1
    for c0, c1 in chunks:
        gate = _dot(a, wg_ref[:, c0:c1])
        up = _dot(a, wu_ref[:, c0:c1])
        t = (jax.nn.silu(gate) * up).astype(BF16)
        out_ref[...] += _dot(t, wd_ref[c0:c1, :])
    if final_norm:
        out_ref[...] = _rms(out_ref[...], gf_ref[...])


def _ffn(h, o, wo, gain, wg, wu, wd, final_gain, final_norm, tm=1024):
    t = h.shape[0]
    d_ff = wg.shape[1]
    step = 3 * MXU_TILE
    chunks = tuple((c0, min(c0 + step, d_ff)) for c0 in range(0, d_ff, step))
    assert d_ff % MXU_TILE == 0
    gain = gain.reshape(1, D_MODEL)
    final_gain = final_gain.reshape(1, D_MODEL)
    const = lambda a: pl.BlockSpec(a.shape, lambda i: (0, 0), pipeline_mode=pl.Buffered(1))
    return pl.pallas_call(
        functools.partial(_ffn_kernel, final_norm=final_norm, chunks=chunks),
        grid=(t // tm,),
        in_specs=[
            pl.BlockSpec((tm, D_MODEL), lambda i: (i, 0)),
            pl.BlockSpec((tm, o.shape[1]), lambda i: (i, 0)),
            const(wo), const(gain), const(wg), const(wu), const(wd), const(final_gain),
        ],
        out_specs=pl.BlockSpec((tm, D_MODEL), lambda i: (i, 0)),
        out_shape=jax.ShapeDtypeStruct((t, D_MODEL), F32),
        compiler_params=_cparams(("parallel",)),
        name="ffn",
    )(h, o, wo, gain, wg, wu, wd, final_gain)


def _block_max(s, rows):
    if rows is None:
        return jnp.max(s, axis=0, keepdims=True)
    h = s.shape[0] // len(rows)
    ms = [jnp.max(s[a * h:(a + 1) * h].reshape(h // SUBLANES, SUBLANES, s.shape[1]), axis=0) + row
          for a, row in enumerate(rows)]
    return jnp.max(functools.reduce(jnp.maximum, ms), axis=0, keepdims=True)


def _block_exp2(s, rows, m):
    if rows is None:
        return jnp.exp2(s - m)
    h = s.shape[0] // len(rows)
    return jnp.concatenate([jnp.exp2(s[a * h:(a + 1) * h] - (m - row))
                            for a, row in enumerate(rows)], axis=0)


def _flash_step_multi(kvs, qps, rows, m_scs, acc_scs):
    n = len(kvs)
    ss = [_dot(kvs[c], qps[c]) for c in range(n)]
    m_olds = [m_scs[c][...] for c in range(n)]
    acc_olds = [acc_scs[c][...] for c in range(n)]
    m_news = [jnp.maximum(m_olds[c], _block_max(ss[c], rows[c])) for c in range(n)]
    alphas = [jnp.exp2(m_olds[c] - m_news[c]) for c in range(n)]
    ps = [_block_exp2(ss[c], rows[c], m_news[c]) for c in range(n)]
    pvs = [_pv(kvs[c], ps[c]) for c in range(n)]
    acc_news = [alphas[c] * acc_olds[c] + pvs[c] for c in range(n)]
    for c in range(n):
        m_scs[c][...] = m_news[c]
        acc_scs[c][...] = acc_news[c]


def _flash_first_multi(kvs, ss, rows, m_scs, acc_scs):
    n = len(kvs)
    ms = [jnp.maximum(_block_max(ss[c], rows[c]), M_FLOOR) for c in range(n)]
    ps = [_block_exp2(ss[c], rows[c], ms[c]) for c in range(n)]
    accs = [_pv(kvs[c], ps[c]) for c in range(n)]
    for c in range(n):
        m_scs[c][...] = ms[c]
        acc_scs[c][...] = accs[c]


def _pv(kv, p):
    tk = kv.shape[0]
    lhs = jnp.concatenate([jnp.ones((ONES_ROWS, tk), BF16), kv.T[HEAD_DIM:2 * HEAD_DIM]], axis=0)
    return _dot(lhs, p.astype(BF16))


def _flash_out(acc):
    l = acc[0:1, :]
    return acc[ONES_ROWS:ACC_ROWS, :] / jnp.where(l > 0, l, 1.0)


def _iota(shape, dim):
    return lax.broadcasted_iota(jnp.int32, shape, dim)


def _causal_bias(tk, tq):
    ok = _iota((tk, tq), 0) <= _iota((tk, tq), 1)
    return jnp.where(ok, 0.0, MASKED)


def _rank_select(adj, n_rows, topk):
    j = _iota(adj.shape, 0)
    rank = jnp.zeros(adj.shape, jnp.int32)
    for k in range(n_rows):
        rk = adj[k:k + 1, :]
        beats = (rk > adj) | ((rk == adj) & (j > k))
        rank = rank + beats.astype(jnp.int32)
    return rank < topk


def _nsa_cmp_kernel(*refs):
    n_slab = 2 * NSA_KV // LANES
    x_refs = refs[:n_slab]
    w1k_ref, w2k_ref, pk_ref, w1v_ref, w2v_ref, pv_ref, o_ref = refs[n_slab:]
    per_slab = LANES // HEAD_DIM
    nb, _, n_seg, _ = o_ref.shape
    seq = n_seg * CMP_STRIDE
    for t, x_ref in enumerate(x_refs):
        is_v = t >= n_slab // 2
        w1_ref, w2_ref, p_ref = (w1v_ref, w2v_ref, pv_ref) if is_v else (w1k_ref, w2k_ref, pk_ref)
        lane0 = HEAD_DIM if is_v else 0
        tok = [jnp.concatenate([x_ref[pl.ds(bb * seq + l, n_seg, stride=CMP_STRIDE), :]
                                for bb in range(nb)], axis=0) for l in range(CMP_STRIDE)]
        pp = _dot(p_ref[...], w1_ref[...])
        posb = pp[0:1, :CMP_HIDDEN] + pp[1:2, CMP_HIDDEN:]
        for e in range(per_slab):
            g = (t % (n_slab // 2)) * per_slab + e
            x = jnp.concatenate([tk_[:, e * HEAD_DIM:(e + 1) * HEAD_DIM] for tk_ in tok],
                                axis=1).astype(BF16)
            acc = _dot(x, w1_ref[...])
            hid = acc[:, :CMP_HIDDEN] + pltpu.roll(acc[:, CMP_HIDDEN:], nb * n_seg - 1, 0) + posb
            y = _dot(jax.nn.gelu(hid).astype(BF16), w2_ref[...])
            for bb in range(nb):
                o_ref[bb, g, :, lane0:lane0 + HEAD_DIM] = (
                    y[bb * n_seg:(bb + 1) * n_seg].astype(o_ref.dtype))


def _nsa_cmp(cmp_in, batch, seq, w1k, w2k, pk, w1v, w2v, pv):
    full = lambda a: pl.BlockSpec(a.shape, lambda i: (0,) * a.ndim)
    n_seg = seq // CMP_STRIDE
    n_slab = 2 * NSA_KV // LANES
    nb = max(d for d in (1, 2, 4) if batch % d == 0)
    slabs = [pl.BlockSpec((nb * seq, LANES), functools.partial(lambda i, t: (i, t), t=t))
             for t in range(n_slab)]
    return pl.pallas_call(
        _nsa_cmp_kernel,
        grid=(batch // nb,),
        in_specs=slabs + [full(w1k), full(w2k), full(pk), full(w1v), full(w2v), full(pv)],
        out_specs=pl.BlockSpec((nb, NSA_GROUPS, n_seg, 2 * HEAD_DIM), lambda i: (i, 0, 0, 0)),
        out_shape=jax.ShapeDtypeStruct((batch, NSA_GROUPS, n_seg, 2 * HEAD_DIM), BF16),
        compiler_params=_cparams(("parallel",)),
        name="nsa_cmp",
    )(*([cmp_in] * n_slab), w1k, w2k, pk, w1v, w2v, pv)


def _nsa_attn_kernel(q_ref, kvs_ref, kvw_ref, kvc_ref, gt_ref, o_ref,
                     m_sc, acc_sc, selb_sc, out_sc, *, seq, ng):
    tq, tk = ATT_TQ, ATT_TK
    i = pl.program_id(2)
    q0 = i * tq
    n_slc = seq // SLC_BLOCK
    n_cmp_pad = seq // CMP_STRIDE
    rep = NSA_REP
    sub = tk // SLC_BLOCK
    win_rows = WINDOW + tq
    groups = range(ng)
    lanes = [slice(g * LANES, (g + 1) * LANES) for g in groups]

    def heads(x):
        return jnp.concatenate([x] * rep, axis=1)

    qt = q_ref[...].astype(F32).T
    zpad = jnp.zeros((HEAD_DIM, tq), F32)
    qps = [jnp.concatenate(
        [jnp.concatenate([qt[(g * rep + r) * HEAD_DIM:(g * rep + r + 1) * HEAD_DIM], zpad], axis=0)
         for r in range(rep)], axis=1).astype(BF16) for g in groups]
    gates = jax.nn.sigmoid(gt_ref[...]).T

    def emit(g, branch, o_t, first):
        for r in range(rep):
            row = branch * NSA_HEADS + g * rep + r
            val = gates[row:row + 1, :] * o_t[:, r * tq:(r + 1) * tq]
            sl = slice((g * rep + r) * HEAD_DIM, (g * rep + r + 1) * HEAD_DIM)
            if first:
                out_sc[sl, :] = val
            else:
                out_sc[sl, :] += val

    w0 = pl.multiple_of(jnp.maximum(q0 - WINDOW, 0), tk)
    d0 = pl.multiple_of(q0, tk)
    kvcs = [kvc_ref[0, g] for g in groups]
    kvws = [kvw_ref[pl.ds(w0, win_rows), lanes[g]] for g in groups]
    kvds = [kvs_ref[pl.ds(d0, tk), lanes[g]] for g in groups]
    s_cs = [_dot(kvcs[g], qps[g]) for g in groups]
    s_ws = [_dot(kvws[g], qps[g]) for g in groups]

    c_end = _iota((n_cmp_pad, tq), 0) * CMP_STRIDE + (CMP_BLOCK - 1)
    c_ok = (c_end <= q0 + _iota((n_cmp_pad, tq), 1)) & (c_end < seq)
    c_bias = heads(jnp.where(c_ok, 0.0, MASKED))
    k_pos = w0 + _iota((win_rows, tq), 0)
    t_pos = q0 + _iota((win_rows, tq), 1)
    w_bias = heads(jnp.where((k_pos <= t_pos) & (k_pos > t_pos - WINDOW), 0.0, MASKED))
    causal = _causal_bias(tk, tq)

    jj = _iota((n_slc, n_cmp_pad), 0) * SLC_BLOCK
    cc = _iota((n_slc, n_cmp_pad), 1) * CMP_STRIDE
    overlap_t = jnp.where((cc < jj + SLC_BLOCK) & (cc + CMP_BLOCK > jj) & (cc + CMP_BLOCK <= seq),
                          1.0, 0.0).astype(BF16)
    j_blk = _iota((n_slc, tq), 0)
    cur = (q0 + _iota((n_slc, tq), 1)) // SLC_BLOCK
    forced = (j_blk == 0) | (j_blk == cur) | (j_blk == cur - 1)
    valid = j_blk <= cur

    ss = [s_cs[g] + c_bias for g in groups]
    ms = [jnp.maximum(jnp.max(s, axis=0, keepdims=True), M_FLOOR) for s in ss]
    es = [jnp.exp2(ss[g] - ms[g]) for g in groups]
    ds = [jnp.sum(e, axis=0, keepdims=True) for e in es]
    p_cs = [es[g] / jnp.where(ds[g] > 0, ds[g], 1.0) for g in groups]
    o_cs = [_pv(kvcs[g], p_cs[g])[ONES_ROWS:ACC_ROWS, :] for g in groups]

    ss = [s_ws[g] + w_bias for g in groups]
    ms = [jnp.max(s, axis=0, keepdims=True) for s in ss]
    o_ws = [_flash_out(_pv(kvws[g], jnp.exp2(ss[g] - ms[g]))) for g in groups]

    selbs = []
    for g in groups:
        p_sum = p_cs[g][:, 0:tq]
        for r in range(1, rep):
            p_sum = p_sum + p_cs[g][:, r * tq:(r + 1) * tq]
        p_hi = p_sum.astype(BF16)
        p_lo = (p_sum - p_hi.astype(F32)).astype(BF16)
        imp = _dot(overlap_t, p_hi) + _dot(overlap_t, p_lo)
        adj = jnp.where(forced, jnp.inf, jnp.where(valid, imp, -jnp.inf))
        sel = valid & _rank_select(adj, n_slc, min(SLC_TOPK, n_slc))
        selbs.append(jnp.where(sel, 0.0, MASKED))

    for g in groups:
        selb_sc[g] = selbs[g]
        emit(g, 0, o_cs[g], True)
        emit(g, 2, o_ws[g], False)

    def sel_rows(g, blk0):
        return [heads(selb_sc[g, pl.ds(blk0 + a, 1), :]) for a in range(sub)]

    m_scs = [m_sc.at[g] for g in groups]
    acc_scs = [acc_sc.at[g] for g in groups]

    causal_h = heads(causal)
    s_ds = [_dot(kvds[g], qps[g]) for g in groups]
    _flash_first_multi(kvds, [s_ds[g] + causal_h for g in groups],
                       [sel_rows(g, i * sub) for g in groups], m_scs, acc_scs)

    def slc_body(kt, carry):
        k0 = pl.multiple_of(kt * tk, tk)
        _flash_step_multi([kvs_ref[pl.ds(k0, tk), lanes[g]] for g in groups], qps,
                          [sel_rows(g, kt * sub) for g in groups], m_scs, acc_scs)
        return carry

    lax.fori_loop(0, i, slc_body, 0)
    for g in groups:
        emit(g, 1, _flash_out(acc_scs[g][...]), False)

    o_ref[...] = out_sc[...].T.astype(o_ref.dtype)


def _nsa_attn(main, kvc, gates, batch, seq):
    tq = ATT_TQ
    ng = NSA_GROUPS
    assert ATT_TK == tq and WINDOW % tq == 0 and seq % tq == 0 and seq >= WINDOW + tq
    assert tq % SLC_BLOCK == 0
    nq = seq // tq
    t = batch * seq
    gw = ng * NSA_REP * HEAD_DIM
    kvw = ng * LANES
    assert (NSA_HEADS * HEAD_DIM) % kvw == 0
    kv0 = NSA_HEADS * HEAD_DIM // kvw
    m = NSA_REP * tq
    n_cmp_pad = seq // CMP_STRIDE
    return pl.pallas_call(
        functools.partial(_nsa_attn_kernel, seq=seq, ng=ng),
        grid=(batch, NSA_GROUPS // ng, nq),
        in_specs=[
            pl.BlockSpec((tq, gw), lambda b, g, i: (b * nq + i, g)),
            pl.BlockSpec((seq, kvw), lambda b, g, i: (b, kv0 + g)),
            pl.BlockSpec((seq, kvw), lambda b, g, i: (b, kv0 + NSA_GROUPS // ng + g)),
            pl.BlockSpec((1, ng, n_cmp_pad, LANES), lambda b, g, i: (b, g, 0, 0)),
            pl.BlockSpec((tq, LANES), lambda b, g, i: (b * nq + i, 0)),
        ],
        out_specs=pl.BlockSpec((tq, gw), lambda b, g, i: (b * nq + i, g)),
        out_shape=jax.ShapeDtypeStruct((t, NSA_HEADS * HEAD_DIM), BF16),
        scratch_shapes=[pltpu.VMEM((ng, 1, m), F32),
                        pltpu.VMEM((ng, ACC_ROWS, m), F32),
                        pltpu.VMEM((ng, seq // SLC_BLOCK, tq), F32),
                        pltpu.VMEM((gw, tq), F32)],
        compiler_params=_cparams(("parallel", "parallel", "arbitrary")),
        name="nsa_attn",
    )(main, main, main, kvc, gates)


def _interleave_heads(wa, wb, n):
    d = wa.shape[0]
    return jnp.stack([wa.reshape(d, n, HEAD_DIM), wb.reshape(d, n, HEAD_DIM)],
                     axis=2).reshape(d, 2 * n * HEAD_DIM)


def _nsa_mixer(h, gain, w_in, pos_k, pos_v, k_w1, k_w2, v_w1, v_w2, batch, seq, cast, layer):
    nh = NSA_HEADS * HEAD_DIM
    c = [nh + i * NSA_KV for i in range(7)]
    wq = w_in[:, :c[0]]
    wkc, wvc, wks, wvs, wkw, wvw = (w_in[:, c[i]:c[i + 1]] for i in range(6))
    w_gate = w_in[:, c[6]:]
    w_main = jnp.concatenate(
        [wq, _interleave_heads(wks, wvs, NSA_GROUPS), _interleave_heads(wkw, wvw, NSA_GROUPS)],
        axis=1).astype(BF16)
    w_cmp = jnp.concatenate([wkc, wvc], axis=1).astype(BF16)
    wg = jnp.pad(w_gate, ((0, 0), (0, LANES - 3 * NSA_HEADS))).astype(BF16)

    (main, gates, cmp_in), casted = _proj(h, gain, _q_scale_row(nh, w_main.shape[1]),
                                          [w_main, wg, w_cmp], [BF16, F32, F32], cast, layer)

    half = CMP_STRIDE * HEAD_DIM

    def seg_weights(w1):
        return jnp.concatenate([w1[:half], w1[half:]], axis=1).astype(BF16)

    def pos_rows(p):
        rows = p.reshape(2, half)
        return jnp.pad(rows, ((0, SUBLANES - 2), (0, 0))).astype(BF16)

    kvc = _nsa_cmp(cmp_in, batch, seq, seg_weights(k_w1), k_w2.astype(BF16), pos_rows(pos_k),
                   seg_weights(v_w1), v_w2.astype(BF16), pos_rows(pos_v))
    return _nsa_attn(main, kvc, gates, batch, seq), casted


def _moba_attn_kernel(q_ref, kv_ref, o_ref, m_sc, acc_sc, selb_sc, kmean_sc, *, seq, nh):
    tq, tk = ATT_TQ, ATT_TK
    i = pl.program_id(2)
    q0 = i * tq
    n_blk = seq // MOBA_BLOCK
    heads = range(nh)
    lanes = [slice(h * LANES, (h + 1) * LANES) for h in heads]

    @pl.when(i == 0)
    def _():
        for h in heads:
            for j in range(n_blk):
                blk = kv_ref[j * MOBA_BLOCK:(j + 1) * MOBA_BLOCK, lanes[h]].astype(F32)
                kmean_sc[h, j:j + 1, :] = jnp.sum(blk, axis=0, keepdims=True) * (1.0 / MOBA_BLOCK)

    qt = q_ref[...].astype(F32).T
    zpad = jnp.zeros((HEAD_DIM, tq), F32)
    qps = [jnp.concatenate([qt[h * HEAD_DIM:(h + 1) * HEAD_DIM], zpad], axis=0).astype(BF16)
           for h in heads]
    d0 = pl.multiple_of(q0, tk)
    kvds = [kv_ref[pl.ds(d0, tk), lanes[h]] for h in heads]
    s_ds = [_dot(kvds[h], qps[h]) for h in heads]
    past = _iota((n_blk, tq), 0) < i
    for h in heads:
        score = _dot(kmean_sc[h].astype(BF16), qps[h])
        adj = jnp.where(past, score, -jnp.inf)
        sel = past & _rank_select(adj, n_blk, MOBA_TOPK)
        selb_sc[h] = jnp.where(sel, 0.0, MASKED)

    m_scs = [m_sc.at[h] for h in heads]
    acc_scs = [acc_sc.at[h] for h in heads]

    causal = _causal_bias(tk, tq)
    _flash_first_multi(kvds, [s_ds[h] + causal for h in heads], [None] * nh, m_scs, acc_scs)

    def body(kt, carry):
        k0 = pl.multiple_of(kt * tk, tk)
        _flash_step_multi([kv_ref[pl.ds(k0, tk), lanes[h]] for h in heads], qps,
                          [[selb_sc[h, pl.ds(kt, 1), :]] for h in heads], m_scs, acc_scs)
        return carry

    lax.fori_loop(0, i, body, 0)
    outs =[_flash_out(acc_scs[h][...]) for h in heads]
    o_ref[...] = jnp.concatenate(outs, axis=0).T.astype(o_ref.dtype)


def _moba_attn(q, kv, batch, seq):
    tq = ATT_TQ
    nh = MOBA_HEADS
    assert ATT_TK == tq and MOBA_BLOCK == tq and seq % tq == 0
    nq = seq // tq
    t = batch * seq
    d_heads = MOBA_HEADS * HEAD_DIM
    n_blk = seq // MOBA_BLOCK
    qw = nh * HEAD_DIM
    kvw = nh * LANES
    return pl.pallas_call(
        functools.partial(_moba_attn_kernel, seq=seq, nh=nh),
        grid=(batch, MOBA_HEADS // nh, nq),
        in_specs=[
            pl.BlockSpec((tq, qw), lambda b, hg, i: (b * nq + i, hg)),
            pl.BlockSpec((seq, kvw), lambda b, hg, i: (b, hg)),
        ],
        out_specs=pl.BlockSpec((tq, qw), lambda b, hg, i: (b * nq + i, hg)),
        out_shape=jax.ShapeDtypeStruct((t, d_heads), BF16),
        scratch_shapes=[pltpu.VMEM((nh, 1, tq), F32),
                        pltpu.VMEM((nh, ACC_ROWS, tq), F32),
                        pltpu.VMEM((nh, n_blk, tq), F32),
                        pltpu.VMEM((nh, n_blk, LANES), F32)],
        compiler_params=_cparams(("parallel", "parallel", "arbitrary")),
        name="moba_attn",
    )(q, kv)


def _moba_mixer(h, gain, w_in, batch, seq, cast, layer):
    nh = MOBA_HEADS * HEAD_DIM
    wq = w_in[:, :nh]
    wk = w_in[:, nh:2 * nh]
    wv = w_in[:, 2 * nh:]
    w_kv = _interleave_heads(wk, wv, MOBA_HEADS).astype(BF16)
    (q, kv), casted = _proj(h, gain, _q_scale_row(nh, nh), [wq.astype(BF16), w_kv], [BF16, BF16],
                            cast, layer)
    return _moba_attn(q, kv, batch, seq), casted


def _conv_kernel(*refs, n_cast):
    h_ref, g_ref, win_ref, cw_ref = refs[:4]
    c_refs = refs[4:4 + n_cast]
    o_ref = refs[4 + n_cast]
    ztail_sc = refs[-1]
    _cast_blocks(c_refs, refs[5 + n_cast:5 + 2 * n_cast])
    i = pl.program_id(1)
    d = D_MODEL
    a = _rms(h_ref[...], g_ref[...]).astype(BF16)
    ts = a.shape[0]

    @pl.when(i == 0)
    def _():
        ztail_sc[...] = jnp.zeros_like(ztail_sc)

    row = _iota((ts, MXU_TILE), 0)
    for c0 in range(0, d, MXU_TILE):
        cols = slice(c0, c0 + MXU_TILE)
        b_gate = _dot(a, win_ref[:, c0:c0 + MXU_TILE])
        z = (_dot(a, win_ref[:, d + c0:d + c0 + MXU_TILE])
             * _dot(a, win_ref[:, 2 * d + c0:2 * d + c0 + MXU_TILE]))
        zp = ztail_sc[:, cols]
        ztail_sc[:, cols] = z[ts - SUBLANES:ts]
        z1 = jnp.where(row == 0, zp[7:8], pltpu.roll(z, 1, 0))
        z2 = jnp.where(row == 0, zp[6:7], jnp.where(row == 1, zp[7:8], pltpu.roll(z, 2, 0)))
        y = cw_ref[0:1, cols] * z2 + cw_ref[1:2, cols] * z1 + cw_ref[2:3, cols] * z
        o_ref[:, cols] = (b_gate * y).astype(o_ref.dtype)


def _conv_mixer(h, gain, w_in, conv_w, batch, seq, cast, layer, ts=1024):
    t = batch * seq
    ns = seq // ts
    assert CONV_WIDTH - 1 <= SUBLANES
    cw = jnp.pad(conv_w, ((0, SUBLANES - CONV_WIDTH), (0, 0)))
    c_in, c_out, c_shapes = _cast_rider(cast, layer, batch * ns, lambda b, i: b * ns + i)
    outs = pl.pallas_call(
        functools.partial(_conv_kernel, n_cast=len(cast)),
        grid=(batch, ns),
        in_specs=[
            pl.BlockSpec((ts, D_MODEL), lambda b, i: (b * ns + i, 0)),
            pl.BlockSpec((1, D_MODEL), lambda b, i: (0, 0)),
            pl.BlockSpec(w_in.shape, lambda b, i: (0, 0)),
            pl.BlockSpec(cw.shape, lambda b, i: (0, 0)),
        ] + c_in,
        out_specs=[pl.BlockSpec((ts, D_MODEL), lambda b, i: (b * ns + i, 0))] + c_out,
        out_shape=[jax.ShapeDtypeStruct((t, D_MODEL), BF16)] + c_shapes,
        scratch_shapes=[pltpu.VMEM((SUBLANES, D_MODEL), F32)],
        compiler_params=_cparams(("arbitrary", "arbitrary")),
        name="conv_mix",
    )(h, gain.reshape(1, D_MODEL), w_in.astype(BF16), cw, *cast)
    return outs[0], outs[1:]


def kernel(x, norm_mix, norm_ffn, norm_final, ffn_w_gate, ffn_w_up, ffn_w_down, nsa_w_in, nsa_w_out, nsa_cmp_pos_k, nsa_cmp_pos_v, nsa_cmp_k_w1, nsa_cmp_k_w2, nsa_cmp_v_w1, nsa_cmp_v_w2, moba_w_in, moba_w_out, conv_w_in, conv_w, conv_w_out):
    batch, seq, d = x.shape
    depth = norm_mix.shape[0]
    h = x.reshape(batch * seq, d)
    ffn_f32 = [ffn_w_gate, ffn_w_up, ffn_w_down]
    for i in range(depth):
        kind, j = i % N_MIXERS, i // N_MIXERS
        if kind == 0:
            o, ffn_w = _nsa_mixer(h, norm_mix[i], nsa_w_in[j], nsa_cmp_pos_k[j], nsa_cmp_pos_v[j],
                                  nsa_cmp_k_w1[j], nsa_cmp_k_w2[j], nsa_cmp_v_w1[j], nsa_cmp_v_w2[j],
                                  batch, seq, ffn_f32, i)
            wo = nsa_w_out[j]
        elif kind == 1:
            o, ffn_w = _moba_mixer(h, norm_mix[i], moba_w_in[j], batch, seq, ffn_f32, i)
            wo = moba_w_out[j]
        else:
            o, ffn_w = _conv_mixer(h, norm_mix[i], conv_w_in[j], conv_w[j], batch, seq, ffn_f32, i)
            wo = conv_w_out[j]
        wg, wu, wd = ffn_w
        h = _ffn(h, o, wo.astype(BF16), norm_ffn[i], wg, wu, wd, norm_final,
                 final_norm=(i == depth - 1))
    return h.reshape(batch, seq, d)
```

```python
import functools

import jax
import jax.numpy as jnp
from jax import lax
from jax.experimental import pallas as pl
from jax.experimental.pallas import tpu as pltpu

F32 = jnp.float32
BF16 = jnp.bfloat16

D_MODEL = 1024
HEAD_DIM = 64
RMS_EPS = 1e-6
N_MIXERS = 3

NSA_HEADS = 16
NSA_GROUPS = 4
NSA_REP = NSA_HEADS // NSA_GROUPS
NSA_KV = NSA_GROUPS * HEAD_DIM
CMP_BLOCK = 32
CMP_STRIDE = 16
CMP_HIDDEN = 256
SLC_BLOCK = 64
SLC_TOPK = 16
WINDOW = 512

MOBA_HEADS = 16
MOBA_BLOCK = 256
MOBA_TOPK = 3

CONV_WIDTH = 3

LANES = 128
SUBLANES = 8
BF16_SUBLANES = 16
MXU_TILE = 256
VMEM_LIMIT = 56 * 1024 * 1024
MASKED = -1e30
M_FLOOR = -1e29
LOG2E = 1.4426950408889634

ONES_ROWS = 16
ACC_ROWS = ONES_ROWS + HEAD_DIM

ATT_TQ = 256
ATT_TK = 256


def _dot(a, b):
    return jnp.dot(a, b, preferred_element_type=F32)


def _rms(x, g):
    y = x * lax.rsqrt(jnp.mean(x * x, axis=-1, keepdims=True) + RMS_EPS)
    return y * g


def _cparams(sem):
    return pltpu.CompilerParams(dimension_semantics=sem, vmem_limit_bytes=VMEM_LIMIT)


def _cast_rider(stacked, layer, n_steps, step_of):
    in_specs, out_specs, out_shapes = [], [], []
    for w in stacked:
        _, r, c = w.shape
        rb = next(b for b in range(BF16_SUBLANES, r + 1, BF16_SUBLANES)
                  if r % b == 0 and r // b <= n_steps)
        last = r // rb - 1
        out_idx = lambda *g, last=last: (jnp.minimum(step_of(*g), last), 0)
        in_idx = lambda *g, last=last: (layer, jnp.minimum(step_of(*g), last), 0)
        in_specs.append(pl.BlockSpec((None, rb, c), in_idx))
        out_specs.append(pl.BlockSpec((rb, c), out_idx))
        out_shapes.append(jax.ShapeDtypeStruct((r, c), BF16))
    return in_specs, out_specs, out_shapes


def _cast_blocks(w_refs, o_refs):
    for w_ref, o_ref in zip(w_refs, o_refs):
        o_ref[...] = w_ref[...].astype(o_ref.dtype)


def _proj_kernel(*refs, n_out, n_cast, chunk):
    x_ref, g_ref, sc_ref = refs[0], refs[1], refs[2]
    w_refs = refs[3:3 + n_out]
    c_refs = refs[3 + n_out:3 + n_out + n_cast]
    o_refs = refs[3 + n_out + n_cast:3 + 2 * n_out + n_cast]
    _cast_blocks(c_refs, refs[3 + 2 * n_out + n_cast:])
    a = _rms(x_ref[...], g_ref[...]).astype(BF16)
    for k, (w_ref, o_ref) in enumerate(zip(w_refs, o_refs)):
        n = w_ref.shape[1]
        for n0 in range(0, n, chunk):
            n1 = min(n0 + chunk, n)
            y = _dot(a, w_ref[:, n0:n1])
            if k == 0:
                y = y * sc_ref[:, n0:n1]
            o_ref[:, n0:n1] = y.astype(o_ref.dtype)


def _proj(h, gain, scale0, weights, out_dtypes, cast, layer, tm=1024):
    t = h.shape[0]
    n_out = len(weights)
    n_steps = t // tm
    const = lambda a: pl.BlockSpec(a.shape, lambda i: (0, 0))
    gain = gain.reshape(1, D_MODEL)
    c_in, c_out, c_shapes = _cast_rider(cast, layer, n_steps, lambda i: i)
    in_specs = [pl.BlockSpec((tm, D_MODEL), lambda i: (i, 0)), const(gain), const(scale0)]
    in_specs += [const(w) for w in weights] + c_in
    out_specs = [pl.BlockSpec((tm, w.shape[1]), lambda i: (i, 0)) for w in weights] + c_out
    out_shape = [jax.ShapeDtypeStruct((t, w.shape[1]), dt) for w, dt in zip(weights, out_dtypes)]
    outs = pl.pallas_call(
        functools.partial(_proj_kernel, n_out=n_out, n_cast=len(cast), chunk=512),
        grid=(n_steps,),
        in_specs=in_specs, out_specs=out_specs, out_shape=out_shape + c_shapes,
        compiler_params=_cparams(("arbitrary",)),
        name="proj",
    )(h, gain, scale0, *weights, *cast)
    return outs[:n_out], outs[n_out:]


def _q_scale_row(n_q, n_total):
    q = jnp.full((1, n_q), (HEAD_DIM ** -0.5) * LOG2E, F32)
    return jnp.concatenate([q, jnp.ones((1, n_total - n_q), F32)], axis=1)


def _ffn_kernel(h_ref, o_ref, wo_ref, g_ref, wg_ref, wu_ref, wd_ref, gf_ref, out_ref,
                *, final_norm, chunks):
    h1 = h_ref[...] + _dot(o_ref[...], wo_ref[...])
    a = _rms(h1, g_ref[...]).astype(BF16)
    out_ref[...] = h1
    for c0, c1 in chunks:
        gate = _dot(a, wg_ref[:, c0:c1])
        up = _dot(a, wu_ref[:, c0:c1])
        t = (jax.nn.silu(gate) * up).astype(BF16)
        out_ref[...] += _dot(t, wd_ref[c0:c1, :])
    if final_norm:
        out_ref[...] = _rms(out_ref[...], gf_ref[...])


def _ffn(h, o, wo, gain, wg, wu, wd, final_gain, final_norm, tm=1024):
    t = h.shape[0]
    d_ff = wg.shape[1]
    step = 3 * MXU_TILE
    chunks = tuple((c0, min(c0 + step, d_ff)) for c0 in range(0, d_ff, step))
    assert d_ff % MXU_TILE == 0
    gain = gain.reshape(1, D_MODEL)
    final_gain = final_gain.reshape(1, D_MODEL)
    const = lambda a: pl.BlockSpec(a.shape, lambda i: (0, 0), pipeline_mode=pl.Buffered(1))
    return pl.pallas_call(
        functools.partial(_ffn_kernel, final_norm=final_norm, chunks=chunks),
        grid=(t // tm,),
        in_specs=[
            pl.BlockSpec((tm, D_MODEL), lambda i: (i, 0)),
            pl.BlockSpec((tm, o.shape[1]), lambda i: (i, 0)),
            const(wo), const(gain), const(wg), const(wu), const(wd), const(final_gain),
        ],
        out_specs=pl.BlockSpec((tm, D_MODEL), lambda i: (i, 0)),
        out_shape=jax.ShapeDtypeStruct((t, D_MODEL), F32),
        compiler_params=_cparams(("parallel",)),
        name="ffn",
    )(h, o, wo, gain, wg, wu, wd, final_gain)


def _block_max(s, rows):
    if rows is None:
        return jnp.max(s, axis=0, keepdims=True)
    h = s.shape[0] // len(rows)
    ms = [jnp.max(s[a * h:(a + 1) * h].reshape(h // SUBLANES, SUBLANES, s.shape[1]), axis=0) + row
          for a, row in enumerate(rows)]
    return jnp.max(functools.reduce(jnp.maximum, ms), axis=0, keepdims=True)


def _block_exp2(s, rows, m):
    if rows is None:
        return jnp.exp2(s - m)
    h = s.shape[0] // len(rows)
    return jnp.concatenate([jnp.exp2(s[a * h:(a + 1) * h] - (m - row))
                            for a, row in enumerate(rows)], axis=0)


def _flash_step_multi(kvs, qps, rows, m_scs, acc_scs):
    n = len(kvs)
    ss = [_dot(kvs[c], qps[c]) for c in range(n)]
    m_olds = [m_scs[c][...] for c in range(n)]
    acc_olds = [acc_scs[c][...] for c in range(n)]
    m_news = [jnp.maximum(m_olds[c], _block_max(ss[c], rows[c])) for c in range(n)]
    alphas = [jnp.exp2(m_olds[c] - m_news[c]) for c in range(n)]
    ps = [_block_exp2(ss[c], rows[c], m_news[c]) for c in range(n)]
    pvs = [_pv(kvs[c], ps[c]) for c in range(n)]
    acc_news = [alphas[c] * acc_olds[c] + pvs[c] for c in range(n)]
    for c in range(n):
        m_scs[c][...] = m_news[c]
        acc_scs[c][...] = acc_news[c]


def _flash_first_multi(kvs, ss, rows, m_scs, acc_scs):
    n = len(kvs)
    ms = [jnp.maximum(_block_max(ss[c], rows[c]), M_FLOOR) for c in range(n)]
    ps = [_block_exp2(ss[c], rows[c], ms[c]) for c in range(n)]
    accs = [_pv(kvs[c], ps[c]) for c in range(n)]
    for c in range(n):
        m_scs[c][...] = ms[c]
        acc_scs[c][...] = accs[c]


def _pv(kv, p):
    tk = kv.shape[0]
    lhs = jnp.concatenate([jnp.ones((ONES_ROWS, tk), BF16), kv.T[HEAD_DIM:2 * HEAD_DIM]], axis=0)
    return _dot(lhs, p.astype(BF16))


def _flash_out(acc):
    l = acc[0:1, :]
    return acc[ONES_ROWS:ACC_ROWS, :] / jnp.where(l > 0, l, 1.0)


def _iota(shape, dim):
    return lax.broadcasted_iota(jnp.int32, shape, dim)


def _causal_bias(tk, tq):
    ok = _iota((tk, tq), 0) <= _iota((tk, tq), 1)
    return jnp.where(ok, 0.0, MASKED)


def _rank_select(adj, n_rows, topk):
    j = _iota(adj.shape, 0)
    rank = jnp.zeros(adj.shape, jnp.int32)
    for k in range(n_rows):
        rk = adj[k:k + 1, :]
        beats = (rk > adj) | ((rk == adj) & (j > k))
        rank = rank + beats.astype(jnp.int32)
    return rank < topk


def _nsa_cmp_kernel(*refs):
    n_slab = 2 * NSA_KV // LANES
    x_refs = refs[:n_slab]
    w1k_ref, w2k_ref, pk_ref, w1v_ref, w2v_ref, pv_ref, o_ref = refs[n_slab:]
    per_slab = LANES // HEAD_DIM
    nb, _, n_seg, _ = o_ref.shape
    seq = n_seg * CMP_STRIDE
    for t, x_ref in enumerate(x_refs):
        is_v = t >= n_slab // 2
        w1_ref, w2_ref, p_ref = (w1v_ref, w2v_ref, pv_ref) if is_v else (w1k_ref, w2k_ref, pk_ref)
        lane0 = HEAD_DIM if is_v else 0
        tok = [jnp.concatenate([x_ref[pl.ds(bb * seq + l, n_seg, stride=CMP_STRIDE), :]
                                for bb in range(nb)], axis=0) for l in range(CMP_STRIDE)]
        pp = _dot(p_ref[...], w1_ref[...])
        posb = pp[0:1, :CMP_HIDDEN] + pp[1:2, CMP_HIDDEN:]
        for e in range(per_slab):
            g = (t % (n_slab // 2)) * per_slab + e
            x = jnp.concatenate([tk_[:, e * HEAD_DIM:(e + 1) * HEAD_DIM] for tk_ in tok],
                                axis=1).astype(BF16)
            acc = _dot(x, w1_ref[...])
            hid = acc[:, :CMP_HIDDEN] + pltpu.roll(acc[:, CMP_HIDDEN:], nb * n_seg - 1, 0) + posb
            y = _dot(jax.nn.gelu(hid).astype(BF16), w2_ref[...])
            for bb in range(nb):
                o_ref[bb, g, :, lane0:lane0 + HEAD_DIM] = (
                    y[bb * n_seg:(bb + 1) * n_seg].astype(o_ref.dtype))


def _nsa_cmp(cmp_in, batch, seq, w1k, w2k, pk, w1v, w2v, pv):
    full = lambda a: pl.BlockSpec(a.shape, lambda i: (0,) * a.ndim)
    n_seg = seq // CMP_STRIDE
    n_slab = 2 * NSA_KV // LANES
    nb = max(d for d in (1, 2, 4) if batch % d == 0)
    slabs = [pl.BlockSpec((nb * seq, LANES), functools.partial(lambda i, t: (i, t), t=t))
             for t in range(n_slab)]
    return pl.pallas_call(
        _nsa_cmp_kernel,
        grid=(batch // nb,),
        in_specs=slabs + [full(w1k), full(w2k), full(pk), full(w1v), full(w2v), full(pv)],
        out_specs=pl.BlockSpec((nb, NSA_GROUPS, n_seg, 2 * HEAD_DIM), lambda i: (i, 0, 0, 0)),
        out_shape=jax.ShapeDtypeStruct((batch, NSA_GROUPS, n_seg, 2 * HEAD_DIM), BF16),
        compiler_params=_cparams(("parallel",)),
        name="nsa_cmp",
    )(*([cmp_in] * n_slab), w1k, w2k, pk, w1v, w2v, pv)


def _nsa_attn_kernel(q_ref, kvs_ref, kvw_ref, kvc_ref, gt_ref, o_ref,
                     m_sc, acc_sc, selb_sc, out_sc, *, seq, ng):
    tq, tk = ATT_TQ, ATT_TK
    i = pl.program_id(2)
    q0 = i * tq
    n_slc = seq // SLC_BLOCK
    n_cmp_pad = seq // CMP_STRIDE
    rep = NSA_REP
    sub = tk // SLC_BLOCK
    parts = tq // LANES
    win_rows = WINDOW + LANES
    groups = range(ng)
    lanes = [slice(g * LANES, (g + 1) * LANES) for g in groups]

    def heads(x):
        return jnp.concatenate([x] * rep, axis=1)

    def part_lanes(x, part):
        return jnp.concatenate([x[:, (r * parts + part) * LANES:(r * parts + part + 1) * LANES]
                                for r in range(rep)], axis=1)

    def head_major(xs):
        return jnp.concatenate([xs[p][:, r * LANES:(r + 1) * LANES]
                                for r in range(rep) for p in range(parts)], axis=1)

    qt = q_ref[...].astype(F32).T
    zpad = jnp.zeros((HEAD_DIM, tq), F32)
    qps = [jnp.concatenate(
        [jnp.concatenate([qt[(g * rep + r) * HEAD_DIM:(g * rep + r + 1) * HEAD_DIM], zpad], axis=0)
         for r in range(rep)], axis=1).astype(BF16) for g in groups]
    gates = jax.nn.sigmoid(gt_ref[...]).T

    def emit(g, branch, o_t, first):
        for r in range(rep):
            row = branch * NSA_HEADS + g * rep + r
            val = gates[row:row + 1, :] * o_t[:, r * tq:(r + 1) * tq]
            sl = slice((g * rep + r) * HEAD_DIM, (g * rep + r + 1) * HEAD_DIM)
            if first:
                out_sc[sl, :] = val
            else:
                out_sc[sl, :] += val

    d0 = pl.multiple_of(q0, tk)
    kvcs = [kvc_ref[0, g] for g in groups]
    kvds = [kvs_ref[pl.ds(d0, tk), lanes[g]] for g in groups]
    s_cs = [_dot(kvcs[g], qps[g]) for g in groups]
    kvws, s_ws, w_biases = [], [], []
    for part in range(parts):
        h0 = q0 + part * LANES
        w0 = pl.multiple_of(jnp.maximum(h0 - WINDOW, 0), LANES)
        k_pos = w0 + _iota((win_rows, LANES), 0)
        t_pos = h0 + _iota((win_rows, LANES), 1)
        ok = (k_pos <= t_pos) & (k_pos > t_pos - WINDOW)
        w_biases.append(jnp.concatenate([jnp.where(ok, 0.0, MASKED)] * rep, axis=1))
        kvws.append([kvw_ref[pl.ds(w0, win_rows), lanes[g]] for g in groups])
        s_ws.append([_dot(kvws[part][g], part_lanes(qps[g], part)) for g in groups])

    c_end = _iota((n_cmp_pad, tq), 0) * CMP_STRIDE + (CMP_BLOCK - 1)
    c_ok = (c_end <= q0 + _iota((n_cmp_pad, tq), 1)) & (c_end < seq)
    c_bias = heads(jnp.where(c_ok, 0.0, MASKED))

    jj = _iota((n_slc, n_cmp_pad), 0) * SLC_BLOCK
    cc = _iota((n_slc, n_cmp_pad), 1) * CMP_STRIDE
    overlap_t = jnp.where((cc < jj + SLC_BLOCK) & (cc + CMP_BLOCK > jj) & (cc + CMP_BLOCK <= seq),
                          1.0, 0.0).astype(BF16)
    j_blk = _iota((n_slc, tq), 0)
    cur = (q0 + _iota((n_slc, tq), 1)) // SLC_BLOCK
    forced = (j_blk == 0) | (j_blk == cur) | (j_blk == cur - 1)
    valid = j_blk <= cur

    ss = [s_cs[g] + c_bias for g in groups]
    ms = [jnp.maximum(jnp.max(s, axis=0, keepdims=True), M_FLOOR) for s in ss]
    es = [jnp.exp2(ss[g] - ms[g]) for g in groups]
    ds = [jnp.sum(e, axis=0, keepdims=True) for e in es]
    p_cs = [es[g] / jnp.where(ds[g] > 0, ds[g], 1.0) for g in groups]
    o_cs = [_pv(kvcs[g], p_cs[g])[ONES_ROWS:ACC_ROWS, :] for g in groups]

    o_parts = []
    for part in range(parts):
        ss = [s_ws[part][g] + w_biases[part] for g in groups]
        ms = [jnp.max(s, axis=0, keepdims=True) for s in ss]
        o_parts.append([_flash_out(_pv(kvws[part][g], jnp.exp2(ss[g] - ms[g]))) for g in groups])
    o_ws = [head_major([o_parts[p][g] for p in range(parts)]) for g in groups]

    selbs = []
    for g in groups:
        p_sum = p_cs[g][:, 0:tq]
        for r in range(1, rep):
            p_sum = p_sum + p_cs[g][:, r * tq:(r + 1) * tq]
        p_hi = p_sum.astype(BF16)
        p_lo = (p_sum - p_hi.astype(F32)).astype(BF16)
        imp = _dot(overlap_t, p_hi) + _dot(overlap_t, p_lo)
        adj = jnp.where(forced, jnp.inf, jnp.where(valid, imp, -jnp.inf))
        sel = valid & _rank_select(adj, n_slc, min(SLC_TOPK, n_slc))
        selbs.append(jnp.where(sel, 0.0, MASKED))

    for g in groups:
        selb_sc[g] = selbs[g]
        emit(g, 0, o_cs[g], True)
        emit(g, 2, o_ws[g], False)

    def sel_rows(g, blk0):
        return [heads(selb_sc[g, pl.ds(blk0 + a, 1), :]) for a in range(sub)]

    m_scs = [m_sc.at[g] for g in groups]
    acc_scs = [acc_sc.at[g] for g in groups]

    d_rows = [sel_rows(g, i * sub) for g in groups]
    blk_per_part = LANES // SLC_BLOCK
    d_kvs, d_ss, d_rws = [], [], []
    for part in range(parts):
        nk = (part + 1) * LANES
        ok = _iota((nk, LANES), 0) <= part * LANES + _iota((nk, LANES), 1)
        bias = jnp.concatenate([jnp.where(ok, 0.0, MASKED)] * rep, axis=1)
        d_kvs.append([kvds[g][:nk] for g in groups])
        d_ss.append([_dot(d_kvs[part][g], part_lanes(qps[g], part)) + bias for g in groups])
        d_rws.append([[part_lanes(row, part) for row in d_rows[g][:(part + 1) * blk_per_part]]
                      for g in groups])
    d_ms = [[jnp.maximum(_block_max(d_ss[p][g], d_rws[p][g]), M_FLOOR) for g in groups]
            for p in range(parts)]
    d_ps = [[_block_exp2(d_ss[p][g], d_rws[p][g], d_ms[p][g]) for g in groups] for p in range(parts)]
    d_accs = [[_pv(d_kvs[p][g], d_ps[p][g]) for g in groups] for p in range(parts)]

    for g in groups:
        m_scs[g][...] = head_major([d_ms[p][g] for p in range(parts)])
        acc_scs[g][...] = head_major([d_accs[p][g] for p in range(parts)])

    def slc_body(kt, carry):
        k0 = pl.multiple_of(kt * tk, tk)
        _flash_step_multi([kvs_ref[pl.ds(k0, tk), lanes[g]] for g in groups], qps,
                          [sel_rows(g, kt * sub) for g in groups], m_scs, acc_scs)
        return carry

    lax.fori_loop(0, i, slc_body, 0)
    for g in groups:
        emit(g, 1, _flash_out(acc_scs[g][...]), False)

    o_ref[...] = out_sc[...].T.astype(o_ref.dtype)


def _nsa_attn(main, kvc, gates, batch, seq):
    tq = ATT_TQ
    ng = NSA_GROUPS
    assert ATT_TK == tq and WINDOW % tq == 0 and seq % tq == 0 and seq >= WINDOW + tq
    assert tq % SLC_BLOCK == 0
    nq = seq // tq
    t = batch * seq
    gw = ng * NSA_REP * HEAD_DIM
    kvw = ng * LANES
    assert (NSA_HEADS * HEAD_DIM) % kvw == 0
    kv0 = NSA_HEADS * HEAD_DIM // kvw
    m = NSA_REP * tq
    n_cmp_pad = seq // CMP_STRIDE
    return pl.pallas_call(
        functools.partial(_nsa_attn_kernel, seq=seq, ng=ng),
        grid=(batch, NSA_GROUPS // ng, nq),
        in_specs=[
            pl.BlockSpec((tq, gw), lambda b, g, i: (b * nq + i, g)),
            pl.BlockSpec((seq, kvw), lambda b, g, i: (b, kv0 + g)),
            pl.BlockSpec((seq, kvw), lambda b, g, i: (b, kv0 + NSA_GROUPS // ng + g)),
            pl.BlockSpec((1, ng, n_cmp_pad, LANES), lambda b, g, i: (b, g, 0, 0)),
            pl.BlockSpec((tq, LANES), lambda b, g, i: (b * nq + i, 0)),
        ],
        out_specs=pl.BlockSpec((tq, gw), lambda b, g, i: (b * nq + i, g)),
        out_shape=jax.ShapeDtypeStruct((t, NSA_HEADS * HEAD_DIM), BF16),
        scratch_shapes=[pltpu.VMEM((ng, 1, m), F32),
                        pltpu.VMEM((ng, ACC_ROWS, m), F32),
                        pltpu.VMEM((ng, seq // SLC_BLOCK, tq), F32),
                        pltpu.VMEM((gw, tq), F32)],
        compiler_params=_cparams(("parallel", "parallel", "arbitrary")),
        name="nsa_attn",
    )(main, main, main, kvc, gates)


def _interleave_heads(wa, wb, n):
    d = wa.shape[0]
    return jnp.stack([wa.reshape(d, n, HEAD_DIM), wb.reshape(d, n, HEAD_DIM)],
                     axis=2).reshape(d, 2 * n * HEAD_DIM)


def _nsa_mixer(h, gain, w_in, pos_k, pos_v, k_w1, k_w2, v_w1, v_w2, batch, seq, cast, layer):
    nh = NSA_HEADS * HEAD_DIM
    c = [nh + i * NSA_KV for i in range(7)]
    wq = w_in[:, :c[0]]
    wkc, wvc, wks, wvs, wkw, wvw = (w_in[:, c[i]:c[i + 1]] for i in range(6))
    w_gate = w_in[:, c[6]:]
    w_main = jnp.concatenate(
        [wq, _interleave_heads(wks, wvs, NSA_GROUPS), _interleave_heads(wkw, wvw, NSA_GROUPS)],
        axis=1).astype(BF16)
    w_cmp = jnp.concatenate([wkc, wvc], axis=1).astype(BF16)
    wg = jnp.pad(w_gate, ((0, 0), (0, LANES - 3 * NSA_HEADS))).astype(BF16)

    (main, gates, cmp_in), casted = _proj(h, gain, _q_scale_row(nh, w_main.shape[1]),
                                          [w_main, wg, w_cmp], [BF16, F32, F32], cast, layer)

    half = CMP_STRIDE * HEAD_DIM

    def seg_weights(w1):
        return jnp.concatenate([w1[:half], w1[half:]], axis=1).astype(BF16)

    def pos_rows(p):
        rows = p.reshape(2, half)
        return jnp.pad(rows, ((0, SUBLANES - 2), (0, 0))).astype(BF16)

    kvc = _nsa_cmp(cmp_in, batch, seq, seg_weights(k_w1), k_w2.astype(BF16), pos_rows(pos_k),
                   seg_weights(v_w1), v_w2.astype(BF16), pos_rows(pos_v))
    return _nsa_attn(main, kvc, gates, batch, seq), casted


def _moba_attn_kernel(q_ref, kv_ref, o_ref, m_sc, acc_sc, selb_sc, kmean_sc, *, seq, nh):
    tq, tk = ATT_TQ, ATT_TK
    i = pl.program_id(2)
    q0 = i * tq
    n_blk = seq // MOBA_BLOCK
    heads = range(nh)
    lanes = [slice(h * LANES, (h + 1) * LANES) for h in heads]

    @pl.when(i == 0)
    def _():
        for h in heads:
            for j in range(n_blk):
                blk = kv_ref[j * MOBA_BLOCK:(j + 1) * MOBA_BLOCK, lanes[h]].astype(F32)
                kmean_sc[h, j:j + 1, :] = jnp.sum(blk, axis=0, keepdims=True) * (1.0 / MOBA_BLOCK)

    qt = q_ref[...].astype(F32).T
    zpad = jnp.zeros((HEAD_DIM, tq), F32)
    qps = [jnp.concatenate([qt[h * HEAD_DIM:(h + 1) * HEAD_DIM], zpad], axis=0).astype(BF16)
           for h in heads]
    d0 = pl.multiple_of(q0, tk)
    kvds = [kv_ref[pl.ds(d0, tk), lanes[h]] for h in heads]
    s_ds = [_dot(kvds[h], qps[h]) for h in heads]
    past = _iota((n_blk, tq), 0) < i
    for h in heads:
        score = _dot(kmean_sc[h].astype(BF16), qps[h])
        adj = jnp.where(past, score, -jnp.inf)
        sel = past & _rank_select(adj, n_blk, MOBA_TOPK)
        selb_sc[h] = jnp.where(sel, 0.0, MASKED)

    m_scs = [m_sc.at[h] for h in heads]
    acc_scs = [acc_sc.at[h] for h in heads]

    causal = _causal_bias(tk, tq)
    _flash_first_multi(kvds, [s_ds[h] + causal for h in heads], [None] * nh, m_scs, acc_scs)

    def body(kt, carry):
        k0 = pl.multiple_of(kt * tk, tk)
        _flash_step_multi([kv_ref[pl.ds(k0, tk), lanes[h]] for h in heads], qps,
                          [[selb_sc[h, pl.ds(kt, 1), :]] for h in heads], m_scs, acc_scs)
        return carry

    lax.fori_loop(0, i, body, 0)
    outs =[_flash_out(acc_scs[h][...]) for h in heads]
    o_ref[...] = jnp.concatenate(outs, axis=0).T.astype(o_ref.dtype)


def _moba_attn(q, kv, batch, seq):
    tq = ATT_TQ
    nh = MOBA_HEADS
    assert ATT_TK == tq and MOBA_BLOCK == tq and seq % tq == 0
    nq = seq // tq
    t = batch * seq
    d_heads = MOBA_HEADS * HEAD_DIM
    n_blk = seq // MOBA_BLOCK
    qw = nh * HEAD_DIM
    kvw = nh * LANES
    return pl.pallas_call(
        functools.partial(_moba_attn_kernel, seq=seq, nh=nh),
        grid=(batch, MOBA_HEADS // nh, nq),
        in_specs=[
            pl.BlockSpec((tq, qw), lambda b, hg, i: (b * nq + i, hg)),
            pl.BlockSpec((seq, kvw), lambda b, hg, i: (b, hg)),
        ],
        out_specs=pl.BlockSpec((tq, qw), lambda b, hg, i: (b * nq + i, hg)),
        out_shape=jax.ShapeDtypeStruct((t, d_heads), BF16),
        scratch_shapes=[pltpu.VMEM((nh, 1, tq), F32),
                        pltpu.VMEM((nh, ACC_ROWS, tq), F32),
                        pltpu.VMEM((nh, n_blk, tq), F32),
                        pltpu.VMEM((nh, n_blk, LANES), F32)],
        compiler_params=_cparams(("parallel", "parallel", "arbitrary")),
        name="moba_attn",
    )(q, kv)


def _moba_mixer(h, gain, w_in, batch, seq, cast, layer):
    nh = MOBA_HEADS * HEAD_DIM
    wq = w_in[:, :nh]
    wk = w_in[:, nh:2 * nh]
    wv = w_in[:, 2 * nh:]
    w_kv = _interleave_heads(wk, wv, MOBA_HEADS).astype(BF16)
    (q, kv), casted = _proj(h, gain, _q_scale_row(nh, nh), [wq.astype(BF16), w_kv], [BF16, BF16],
                            cast, layer)
    return _moba_attn(q, kv, batch, seq), casted


def _conv_kernel(*refs, n_cast):
    h_ref, g_ref, win_ref, cw_ref = refs[:4]
    c_refs = refs[4:4 + n_cast]
    o_ref = refs[4 + n_cast]
    ztail_sc = refs[-1]
    _cast_blocks(c_refs, refs[5 + n_cast:5 + 2 * n_cast])
    i = pl.program_id(1)
    d = D_MODEL
    a = _rms(h_ref[...], g_ref[...]).astype(BF16)
    ts = a.shape[0]

    @pl.when(i == 0)
    def _():
        ztail_sc[...] = jnp.zeros_like(ztail_sc)

    row = _iota((ts, MXU_TILE), 0)
    for c0 in range(0, d, MXU_TILE):
        cols = slice(c0, c0 + MXU_TILE)
        b_gate = _dot(a, win_ref[:, c0:c0 + MXU_TILE])
        z = (_dot(a, win_ref[:, d + c0:d + c0 + MXU_TILE])
             * _dot(a, win_ref[:, 2 * d + c0:2 * d + c0 + MXU_TILE]))
        zp = ztail_sc[:, cols]
        ztail_sc[:, cols] = z[ts - SUBLANES:ts]
        z1 = jnp.where(row == 0, zp[7:8], pltpu.roll(z, 1, 0))
        z2 = jnp.where(row == 0, zp[6:7], jnp.where(row == 1, zp[7:8], pltpu.roll(z, 2, 0)))
        y = cw_ref[0:1, cols] * z2 + cw_ref[1:2, cols] * z1 + cw_ref[2:3, cols] * z
        o_ref[:, cols] = (b_gate * y).astype(o_ref.dtype)


def _conv_mixer(h, gain, w_in, conv_w, batch, seq, cast, layer, ts=1024):
    t = batch * seq
    ns = seq // ts
    assert CONV_WIDTH - 1 <= SUBLANES
    cw = jnp.pad(conv_w, ((0, SUBLANES - CONV_WIDTH), (0, 0)))
    c_in, c_out, c_shapes = _cast_rider(cast, layer, batch * ns, lambda b, i: b * ns + i)
    outs = pl.pallas_call(
        functools.partial(_conv_kernel, n_cast=len(cast)),
        grid=(batch, ns),
        in_specs=[
            pl.BlockSpec((ts, D_MODEL), lambda b, i: (b * ns + i, 0)),
            pl.BlockSpec((1, D_MODEL), lambda b, i: (0, 0)),
            pl.BlockSpec(w_in.shape, lambda b, i: (0, 0)),
            pl.BlockSpec(cw.shape, lambda b, i: (0, 0)),
        ] + c_in,
        out_specs=[pl.BlockSpec((ts, D_MODEL), lambda b, i: (b * ns + i, 0))] + c_out,
        out_shape=[jax.ShapeDtypeStruct((t, D_MODEL), BF16)] + c_shapes,
        scratch_shapes=[pltpu.VMEM((SUBLANES, D_MODEL), F32)],
        compiler_params=_cparams(("arbitrary", "arbitrary")),
        name="conv_mix",
    )(h, gain.reshape(1, D_MODEL), w_in.astype(BF16), cw, *cast)
    return outs[0], outs[1:]


def kernel(x, norm_mix, norm_ffn, norm_final, ffn_w_gate, ffn_w_up, ffn_w_down, nsa_w_in, nsa_w_out, nsa_cmp_pos_k, nsa_cmp_pos_v, nsa_cmp_k_w1, nsa_cmp_k_w2, nsa_cmp_v_w1, nsa_cmp_v_w2, moba_w_in, moba_w_out, conv_w_in, conv_w, conv_w_out):
    batch, seq, d = x.shape
    depth = norm_mix.shape[0]
    h = x.reshape(batch * seq, d)
    ffn_f32 = [ffn_w_gate, ffn_w_up, ffn_w_down]
    for i in range(depth):
        kind, j = i % N_MIXERS, i // N_MIXERS
        if kind == 0:
            o, ffn_w = _nsa_mixer(h, norm_mix[i], nsa_w_in[j], nsa_cmp_pos_k[j], nsa_cmp_pos_v[j],
                                  nsa_cmp_k_w1[j], nsa_cmp_k_w2[j], nsa_cmp_v_w1[j], nsa_cmp_v_w2[j],
                                  batch, seq, ffn_f32, i)
            wo = nsa_w_out[j]
        elif kind == 1:
            o, ffn_w = _moba_mixer(h, norm_mix[i], moba_w_in[j], batch, seq, ffn_f32, i)
            wo = moba_w_out[j]
        else:
            o, ffn_w = _conv_mixer(h, norm_mix[i], conv_w_in[j], conv_w[j], batch, seq, ffn_f32, i)
            wo = conv_w_out[j]
        wg, wu, wd = ffn_w
        h = _ffn(h, o, wo.astype(BF16), norm_ffn[i], wg, wu, wd, norm_final,
                 final_norm=(i == depth - 1))
    return h.reshape(batch, seq, d)
```

```python
import functools

import jax
import jax.numpy as jnp
from jax import lax
from jax.experimental import pallas as pl
from jax.experimental.pallas import tpu as pltpu

F32 = jnp.float32
BF16 = jnp.bfloat16

D_MODEL = 1024
HEAD_DIM = 64
RMS_EPS = 1e-6
N_MIXERS = 3

NSA_HEADS = 16
NSA_GROUPS = 4
NSA_REP = NSA_HEADS // NSA_GROUPS
NSA_KV = NSA_GROUPS * HEAD_DIM
CMP_BLOCK = 32
CMP_STRIDE = 16
CMP_HIDDEN = 256
SLC_BLOCK = 64
SLC_TOPK = 16
WINDOW = 512

MOBA_HEADS = 16
MOBA_BLOCK = 256
MOBA_TOPK = 3

CONV_WIDTH = 3

LANES = 128
SUBLANES = 8
BF16_SUBLANES = 16
MXU_TILE = 256
VMEM_LIMIT = 56 * 1024 * 1024
MASKED = -1e30
M_FLOOR = -1e29
LOG2E = 1.4426950408889634

ONES_ROWS = 16
ACC_ROWS = ONES_ROWS + HEAD_DIM

ROW_TILE = 1024
ATT_TQ = 256
ATT_TK = 256


def _dot(a, b):
    return jnp.dot(a, b, preferred_element_type=F32)


def _rms(x, g):
    y = x * lax.rsqrt(jnp.mean(x * x, axis=-1, keepdims=True) + RMS_EPS)
    return y * g


def _cparams(sem):
    return pltpu.CompilerParams(dimension_semantics=sem, vmem_limit_bytes=VMEM_LIMIT)


def _cast_rider(stacked, layer, n_steps, step_of):
    in_specs, out_specs, out_shapes = [], [], []
    for w in stacked:
        _, r, c = w.shape
        rb = next(b for b in range(BF16_SUBLANES, r + 1, BF16_SUBLANES)
                  if r % b == 0 and r // b <= n_steps)
        last = r // rb - 1
        out_idx = lambda *g, last=last: (jnp.minimum(step_of(*g), last), 0)
        in_idx = lambda *g, last=last: (layer, jnp.minimum(step_of(*g), last), 0)
        in_specs.append(pl.BlockSpec((None, rb, c), in_idx))
        out_specs.append(pl.BlockSpec((rb, c), out_idx))
        out_shapes.append(jax.ShapeDtypeStruct((r, c), BF16))
    return in_specs, out_specs, out_shapes


def _cast_blocks(w_refs, o_refs):
    for w_ref, o_ref in zip(w_refs, o_refs):
        o_ref[...] = w_ref[...].astype(o_ref.dtype)


def _proj_kernel(*refs, n_out, n_cast, chunk):
    x_ref, g_ref, sc_ref = refs[0], refs[1], refs[2]
    w_refs = refs[3:3 + n_out]
    c_refs = refs[3 + n_out:3 + n_out + n_cast]
    o_refs = refs[3 + n_out + n_cast:3 + 2 * n_out + n_cast]
    _cast_blocks(c_refs, refs[3 + 2 * n_out + n_cast:])
    a = _rms(x_ref[...], g_ref[...]).astype(BF16)
    for k, (w_ref, o_ref) in enumerate(zip(w_refs, o_refs)):
        n = w_ref.shape[1]
        for n0 in range(0, n, chunk):
            n1 = min(n0 + chunk, n)
            y = _dot(a, w_ref[:, n0:n1])
            if k == 0:
                y = y * sc_ref[:, n0:n1]
            o_ref[:, n0:n1] = y.astype(o_ref.dtype)


def _proj(h, gain, scale0, weights, out_dtypes, cast, layer):
    t = h.shape[0]
    tm = min(ROW_TILE, t)
    n_out = len(weights)
    n_steps = t // tm
    const = lambda a: pl.BlockSpec(a.shape, lambda i: (0, 0))
    gain = gain.reshape(1, D_MODEL)
    c_in, c_out, c_shapes = _cast_rider(cast, layer, n_steps, lambda i: i)
    in_specs = [pl.BlockSpec((tm, D_MODEL), lambda i: (i, 0)), const(gain), const(scale0)]
    in_specs += [const(w) for w in weights] + c_in
    out_specs = [pl.BlockSpec((tm, w.shape[1]), lambda i: (i, 0)) for w in weights] + c_out
    out_shape = [jax.ShapeDtypeStruct((t, w.shape[1]), dt) for w, dt in zip(weights, out_dtypes)]
    outs = pl.pallas_call(
        functools.partial(_proj_kernel, n_out=n_out, n_cast=len(cast), chunk=512),
        grid=(n_steps,),
        in_specs=in_specs, out_specs=out_specs, out_shape=out_shape + c_shapes,
        compiler_params=_cparams(("arbitrary",)),
        name="proj",
    )(h, gain, scale0, *weights, *cast)
    return outs[:n_out], outs[n_out:]


def _q_scale_row(n_q, n_total):
    q = jnp.full((1, n_q), (HEAD_DIM ** -0.5) * LOG2E, F32)
    return jnp.concatenate([q, jnp.ones((1, n_total - n_q), F32)], axis=1)


def _ffn_kernel(h_ref, o_ref, wo_ref, g_ref, wg_ref, wu_ref, wd_ref, gf_ref, out_ref,
                *, final_norm, chunks):
    h1 = h_ref[...] + _dot(o_ref[...], wo_ref[...])
    a = _rms(h1, g_ref[...]).astype(BF16)
    out_ref[...] = h1
    for c0, c1 in chunks:
        gate = _dot(a, wg_ref[:, c0:c1])
        up = _dot(a, wu_ref[:, c0:c1])
        t = (jax.nn.silu(gate) * up).astype(BF16)
        out_ref[...] += _dot(t, wd_ref[c0:c1, :])
    if final_norm:
        out_ref[...] = _rms(out_ref[...], gf_ref[...])


def _ffn(h, o, wo, gain, wg, wu, wd, final_gain, final_norm):
    t = h.shape[0]
    tm = min(ROW_TILE, t)
    d_ff = wg.shape[1]
    step = 3 * MXU_TILE
    chunks = tuple((c0, min(c0 + step, d_ff)) for c0 in range(0, d_ff, step))
    assert d_ff % MXU_TILE == 0
    gain = gain.reshape(1, D_MODEL)
    final_gain = final_gain.reshape(1, D_MODEL)
    const = lambda a: pl.BlockSpec(a.shape, lambda i: (0, 0), pipeline_mode=pl.Buffered(1))
    return pl.pallas_call(
        functools.partial(_ffn_kernel, final_norm=final_norm, chunks=chunks),
        grid=(t // tm,),
        in_specs=[
            pl.BlockSpec((tm, D_MODEL), lambda i: (i, 0)),
            pl.BlockSpec((tm, o.shape[1]), lambda i: (i, 0)),
            const(wo), const(gain), const(wg), const(wu), const(wd), const(final_gain),
        ],
        out_specs=pl.BlockSpec((tm, D_MODEL), lambda i: (i, 0)),
        out_shape=jax.ShapeDtypeStruct((t, D_MODEL), F32),
        compiler_params=_cparams(("parallel",)),
        name="ffn",
    )(h, o, wo, gain, wg, wu, wd, final_gain)


def _block_max(s, rows):
    if rows is None:
        return jnp.max(s, axis=0, keepdims=True)
    h = s.shape[0] // len(rows)
    ms = [jnp.max(s[a * h:(a + 1) * h].reshape(h // SUBLANES, SUBLANES, s.shape[1]), axis=0) + row
          for a, row in enumerate(rows)]
    return jnp.max(functools.reduce(jnp.maximum, ms), axis=0, keepdims=True)


def _block_exp2(s, rows, m):
    if rows is None:
        return jnp.exp2(s - m)
    h = s.shape[0] // len(rows)
    return jnp.concatenate([jnp.exp2(s[a * h:(a + 1) * h] - (m - row))
                            for a, row in enumerate(rows)], axis=0)


def _flash_step_multi(kvs, qps, rows, m_scs, acc_scs):
    n = len(kvs)
    ss = [_dot(kvs[c], qps[c]) for c in range(n)]
    m_olds = [m_scs[c][...] for c in range(n)]
    acc_olds = [acc_scs[c][...] for c in range(n)]
    m_news = [jnp.maximum(m_olds[c], _block_max(ss[c], rows[c])) for c in range(n)]
    alphas = [jnp.exp2(m_olds[c] - m_news[c]) for c in range(n)]
    ps = [_block_exp2(ss[c], rows[c], m_news[c]) for c in range(n)]
    pvs = [_pv(kvs[c], ps[c]) for c in range(n)]
    acc_news = [alphas[c] * acc_olds[c] + pvs[c] for c in range(n)]
    for c in range(n):
        m_scs[c][...] = m_news[c]
        acc_scs[c][...] = acc_news[c]


def _flash_first_multi(kvs, ss, rows, m_scs, acc_scs):
    n = len(kvs)
    ms = [jnp.maximum(_block_max(ss[c], rows[c]), M_FLOOR) for c in range(n)]
    ps = [_block_exp2(ss[c], rows[c], ms[c]) for c in range(n)]
    accs = [_pv(kvs[c], ps[c]) for c in range(n)]
    for c in range(n):
        m_scs[c][...] = ms[c]
        acc_scs[c][...] = accs[c]


def _pv(kv, p):
    tk = kv.shape[0]
    lhs = jnp.concatenate([jnp.ones((ONES_ROWS, tk), BF16), kv.T[HEAD_DIM:2 * HEAD_DIM]], axis=0)
    return _dot(lhs, p.astype(BF16))


def _flash_out(acc):
    l = acc[0:1, :]
    return acc[ONES_ROWS:ACC_ROWS, :] / jnp.where(l > 0, l, 1.0)


def _iota(shape, dim):
    return lax.broadcasted_iota(jnp.int32, shape, dim)


def _causal_bias(tk, tq):
    ok = _iota((tk, tq), 0) <= _iota((tk, tq), 1)
    return jnp.where(ok, 0.0, MASKED)


def _rank_select(adj, n_rows, topk):
    j = _iota(adj.shape, 0)
    rank = jnp.zeros(adj.shape, jnp.int32)
    for k in range(n_rows):
        rk = adj[k:k + 1, :]
        beats = (rk > adj) | ((rk == adj) & (j > k))
        rank = rank + beats.astype(jnp.int32)
    return rank < topk


def _nsa_cmp_kernel(*refs):
    n_slab = 2 * NSA_KV // LANES
    x_refs = refs[:n_slab]
    w1k_ref, w2k_ref, pk_ref, w1v_ref, w2v_ref, pv_ref, o_ref = refs[n_slab:]
    per_slab = LANES // HEAD_DIM
    nb, _, n_seg, _ = o_ref.shape
    seq = n_seg * CMP_STRIDE
    for t, x_ref in enumerate(x_refs):
        is_v = t >= n_slab // 2
        w1_ref, w2_ref, p_ref = (w1v_ref, w2v_ref, pv_ref) if is_v else (w1k_ref, w2k_ref, pk_ref)
        lane0 = HEAD_DIM if is_v else 0
        tok = [jnp.concatenate([x_ref[pl.ds(bb * seq + l, n_seg, stride=CMP_STRIDE), :]
                                for bb in range(nb)], axis=0) for l in range(CMP_STRIDE)]
        pp = _dot(p_ref[...], w1_ref[...])
        posb = pp[0:1, :CMP_HIDDEN] + pp[1:2, CMP_HIDDEN:]
        for e in range(per_slab):
            g = (t % (n_slab // 2)) * per_slab + e
            x = jnp.concatenate([tk_[:, e * HEAD_DIM:(e + 1) * HEAD_DIM] for tk_ in tok],
                                axis=1).astype(BF16)
            acc = _dot(x, w1_ref[...])
            hid = acc[:, :CMP_HIDDEN] + pltpu.roll(acc[:, CMP_HIDDEN:], nb * n_seg - 1, 0) + posb
            y = _dot(jax.nn.gelu(hid).astype(BF16), w2_ref[...])
            for bb in range(nb):
                o_ref[bb, g, :, lane0:lane0 + HEAD_DIM] = (
                    y[bb * n_seg:(bb + 1) * n_seg].astype(o_ref.dtype))


def _nsa_cmp(cmp_in, batch, seq, w1k, w2k, pk, w1v, w2v, pv):
    full = lambda a: pl.BlockSpec(a.shape, lambda i: (0,) * a.ndim)
    n_seg = seq // CMP_STRIDE
    n_slab = 2 * NSA_KV // LANES
    nb = max(d for d in (1, 2, 4) if batch % d == 0)
    slabs = [pl.BlockSpec((nb * seq, LANES), functools.partial(lambda i, t: (i, t), t=t))
             for t in range(n_slab)]
    return pl.pallas_call(
        _nsa_cmp_kernel,
        grid=(batch // nb,),
        in_specs=slabs + [full(w1k), full(w2k), full(pk), full(w1v), full(w2v), full(pv)],
        out_specs=pl.BlockSpec((nb, NSA_GROUPS, n_seg, 2 * HEAD_DIM), lambda i: (i, 0, 0, 0)),
        out_shape=jax.ShapeDtypeStruct((batch, NSA_GROUPS, n_seg, 2 * HEAD_DIM), BF16),
        compiler_params=_cparams(("parallel",)),
        name="nsa_cmp",
    )(*([cmp_in] * n_slab), w1k, w2k, pk, w1v, w2v, pv)


def _nsa_attn_kernel(q_ref, kvs_ref, kvw_ref, kvc_ref, gt_ref, o_ref,
                     m_sc, acc_sc, selb_sc, out_sc, *, seq, ng):
    tq, tk = ATT_TQ, ATT_TK
    i = pl.program_id(2)
    q0 = i * tq
    n_slc = seq // SLC_BLOCK
    n_cmp_pad = seq // CMP_STRIDE
    rep = NSA_REP
    sub = tk // SLC_BLOCK
    parts = tq // LANES
    win_rows = WINDOW + LANES
    groups = range(ng)
    lanes = [slice(g * LANES, (g + 1) * LANES) for g in groups]

    def heads(x):
        return jnp.concatenate([x] * rep, axis=1)

    def part_lanes(x, part):
        return jnp.concatenate([x[:, (r * parts + part) * LANES:(r * parts + part + 1) * LANES]
                                for r in range(rep)], axis=1)

    def head_major(xs):
        return jnp.concatenate([xs[p][:, r * LANES:(r + 1) * LANES]
                                for r in range(rep) for p in range(parts)], axis=1)

    qt = q_ref[...].astype(F32).T
    zpad = jnp.zeros((HEAD_DIM, tq), F32)
    qps = [jnp.concatenate(
        [jnp.concatenate([qt[(g * rep + r) * HEAD_DIM:(g * rep + r + 1) * HEAD_DIM], zpad], axis=0)
         for r in range(rep)], axis=1).astype(BF16) for g in groups]
    gates = jax.nn.sigmoid(gt_ref[...]).T

    def emit(g, branch, o_t, first):
        for r in range(rep):
            row = branch * NSA_HEADS + g * rep + r
            val = gates[row:row + 1, :] * o_t[:, r * tq:(r + 1) * tq]
            sl = slice((g * rep + r) * HEAD_DIM, (g * rep + r + 1) * HEAD_DIM)
            if first:
                out_sc[sl, :] = val
            else:
                out_sc[sl, :] += val

    d0 = pl.multiple_of(q0, tk)
    kvcs = [kvc_ref[0, g] for g in groups]
    kvds = [kvs_ref[pl.ds(d0, tk), lanes[g]] for g in groups]
    s_cs = [_dot(kvcs[g], qps[g]) for g in groups]
    kvws, s_ws, w_biases = [], [], []
    for part in range(parts):
        h0 = q0 + part * LANES
        w0 = pl.multiple_of(jnp.maximum(h0 - WINDOW, 0), LANES)
        k_pos = w0 + _iota((win_rows, LANES), 0)
        t_pos = h0 + _iota((win_rows, LANES), 1)
        ok = (k_pos <= t_pos) & (k_pos > t_pos - WINDOW)
        w_biases.append(jnp.concatenate([jnp.where(ok, 0.0, MASKED)] * rep, axis=1))
        kvws.append([kvw_ref[pl.ds(w0, win_rows), lanes[g]] for g in groups])
        s_ws.append([_dot(kvws[part][g], part_lanes(qps[g], part)) for g in groups])

    c_end = _iota((n_cmp_pad, tq), 0) * CMP_STRIDE + (CMP_BLOCK - 1)
    c_ok = (c_end <= q0 + _iota((n_cmp_pad, tq), 1)) & (c_end < seq)
    c_bias = heads(jnp.where(c_ok, 0.0, MASKED))

    jj = _iota((n_slc, n_cmp_pad), 0) * SLC_BLOCK
    cc = _iota((n_slc, n_cmp_pad), 1) * CMP_STRIDE
    overlap_t = jnp.where((cc < jj + SLC_BLOCK) & (cc + CMP_BLOCK > jj) & (cc + CMP_BLOCK <= seq),
                          1.0, 0.0).astype(BF16)
    j_blk = _iota((n_slc, tq), 0)
    cur = (q0 + _iota((n_slc, tq), 1)) // SLC_BLOCK
    forced = (j_blk == 0) | (j_blk == cur) | (j_blk == cur - 1)
    valid = j_blk <= cur

    ss = [s_cs[g] + c_bias for g in groups]
    ms = [jnp.maximum(jnp.max(s, axis=0, keepdims=True), M_FLOOR) for s in ss]
    es = [jnp.exp2(ss[g] - ms[g]) for g in groups]
    ds = [jnp.sum(e, axis=0, keepdims=True) for e in es]
    p_cs = [es[g] / jnp.where(ds[g] > 0, ds[g], 1.0) for g in groups]
    o_cs = [_pv(kvcs[g], p_cs[g])[ONES_ROWS:ACC_ROWS, :] for g in groups]

    o_parts = []
    for part in range(parts):
        ss = [s_ws[part][g] + w_biases[part] for g in groups]
        ms = [jnp.max(s, axis=0, keepdims=True) for s in ss]
        o_parts.append([_flash_out(_pv(kvws[part][g], jnp.exp2(ss[g] - ms[g]))) for g in groups])
    o_ws = [head_major([o_parts[p][g] for p in range(parts)]) for g in groups]

    for g in groups:
        emit(g, 0, o_cs[g], True)
        emit(g, 2, o_ws[g], False)
        selb_sc[g] = jnp.where(valid, 0.0, MASKED)

    @pl.when((i + 1) * (tq // SLC_BLOCK) > SLC_TOPK)
    def _():
        selbs = []
        for g in groups:
            p_sum = p_cs[g][:, 0:tq]
            for r in range(1, rep):
                p_sum = p_sum + p_cs[g][:, r * tq:(r + 1) * tq]
            p_hi = p_sum.astype(BF16)
            p_lo = (p_sum - p_hi.astype(F32)).astype(BF16)
            imp = _dot(overlap_t, p_hi) + _dot(overlap_t, p_lo)
            adj = jnp.where(forced, jnp.inf, jnp.where(valid, imp, -jnp.inf))
            sel = valid & _rank_select(adj, n_slc, min(SLC_TOPK, n_slc))
            selbs.append(jnp.where(sel, 0.0, MASKED))
        for g in groups:
            selb_sc[g] = selbs[g]

    def sel_rows(g, blk0):
        return [heads(selb_sc[g, pl.ds(blk0 + a, 1), :]) for a in range(sub)]

    m_scs = [m_sc.at[g] for g in groups]
    acc_scs = [acc_sc.at[g] for g in groups]

    d_rows = [sel_rows(g, i * sub) for g in groups]
    blk_per_part = LANES // SLC_BLOCK
    d_kvs, d_ss, d_rws = [], [], []
    for part in range(parts):
        nk = (part + 1) * LANES
        ok = _iota((nk, LANES), 0) <= part * LANES + _iota((nk, LANES), 1)
        bias = jnp.concatenate([jnp.where(ok, 0.0, MASKED)] * rep, axis=1)
        d_kvs.append([kvds[g][:nk] for g in groups])
        d_ss.append([_dot(d_kvs[part][g], part_lanes(qps[g], part)) + bias for g in groups])
        d_rws.append([[part_lanes(row, part) for row in d_rows[g][:(part + 1) * blk_per_part]]
                      for g in groups])
    d_ms = [[jnp.maximum(_block_max(d_ss[p][g], d_rws[p][g]), M_FLOOR) for g in groups]
            for p in range(parts)]
    d_ps = [[_block_exp2(d_ss[p][g], d_rws[p][g], d_ms[p][g]) for g in groups] for p in range(parts)]
    d_accs = [[_pv(d_kvs[p][g], d_ps[p][g]) for g in groups] for p in range(parts)]

    for g in groups:
        m_scs[g][...] = head_major([d_ms[p][g] for p in range(parts)])
        acc_scs[g][...] = head_major([d_accs[p][g] for p in range(parts)])

    def slc_body(kt, carry):
        k0 = pl.multiple_of(kt * tk, tk)
        _flash_step_multi([kvs_ref[pl.ds(k0, tk), lanes[g]] for g in groups], qps,
                          [sel_rows(g, kt * sub) for g in groups], m_scs, acc_scs)
        return carry

    lax.fori_loop(0, i, slc_body, 0)
    for g in groups:
        emit(g, 1, _flash_out(acc_scs[g][...]), False)

    o_ref[...] = out_sc[...].T.astype(o_ref.dtype)


def _nsa_attn(main, kvc, gates, batch, seq):
    tq = ATT_TQ
    ng = NSA_GROUPS
    assert ATT_TK == tq and WINDOW % tq == 0 and seq % tq == 0 and seq >= WINDOW + tq
    assert tq % SLC_BLOCK == 0
    nq = seq // tq
    t = batch * seq
    gw = ng * NSA_REP * HEAD_DIM
    kvw = ng * LANES
    assert (NSA_HEADS * HEAD_DIM) % kvw == 0
    kv0 = NSA_HEADS * HEAD_DIM // kvw
    m = NSA_REP * tq
    n_cmp_pad = seq // CMP_STRIDE
    return pl.pallas_call(
        functools.partial(_nsa_attn_kernel, seq=seq, ng=ng),
        grid=(batch, NSA_GROUPS // ng, nq),
        in_specs=[
            pl.BlockSpec((tq, gw), lambda b, g, i: (b * nq + i, g)),
            pl.BlockSpec((seq, kvw), lambda b, g, i: (b, kv0 + g)),
            pl.BlockSpec((seq, kvw), lambda b, g, i: (b, kv0 + NSA_GROUPS // ng + g)),
            pl.BlockSpec((1, ng, n_cmp_pad, LANES), lambda b, g, i: (b, g, 0, 0)),
            pl.BlockSpec((tq, LANES), lambda b, g, i: (b * nq + i, 0)),
        ],
        out_specs=pl.BlockSpec((tq, gw), lambda b, g, i: (b * nq + i, g)),
        out_shape=jax.ShapeDtypeStruct((t, NSA_HEADS * HEAD_DIM), BF16),
        scratch_shapes=[pltpu.VMEM((ng, 1, m), F32),
                        pltpu.VMEM((ng, ACC_ROWS, m), F32),
                        pltpu.VMEM((ng, seq // SLC_BLOCK, tq), F32),
                        pltpu.VMEM((gw, tq), F32)],
        compiler_params=_cparams(("parallel", "parallel", "arbitrary")),
        name="nsa_attn",
    )(main, main, main, kvc, gates)


def _interleave_heads(wa, wb, n):
    d = wa.shape[0]
    return jnp.stack([wa.reshape(d, n, HEAD_DIM), wb.reshape(d, n, HEAD_DIM)],
                     axis=2).reshape(d, 2 * n * HEAD_DIM)


def _nsa_mixer(h, gain, w_in, pos_k, pos_v, k_w1, k_w2, v_w1, v_w2, batch, seq, cast, layer):
    nh = NSA_HEADS * HEAD_DIM
    c = [nh + i * NSA_KV for i in range(7)]
    wq = w_in[:, :c[0]]
    wkc, wvc, wks, wvs, wkw, wvw = (w_in[:, c[i]:c[i + 1]] for i in range(6))
    w_gate = w_in[:, c[6]:]
    w_main = jnp.concatenate(
        [wq, _interleave_heads(wks, wvs, NSA_GROUPS), _interleave_heads(wkw, wvw, NSA_GROUPS)],
        axis=1).astype(BF16)
    w_cmp = jnp.concatenate([wkc, wvc], axis=1).astype(BF16)
    wg = jnp.pad(w_gate, ((0, 0), (0, LANES - 3 * NSA_HEADS))).astype(BF16)

    (main, gates, cmp_in), casted = _proj(h, gain, _q_scale_row(nh, w_main.shape[1]),
                                          [w_main, wg, w_cmp], [BF16, F32, F32], cast, layer)

    half = CMP_STRIDE * HEAD_DIM

    def seg_weights(w1):
        return jnp.concatenate([w1[:half], w1[half:]], axis=1).astype(BF16)

    def pos_rows(p):
        rows = p.reshape(2, half)
        return jnp.pad(rows, ((0, SUBLANES - 2), (0, 0))).astype(BF16)

    kvc = _nsa_cmp(cmp_in, batch, seq, seg_weights(k_w1), k_w2.astype(BF16), pos_rows(pos_k),
                   seg_weights(v_w1), v_w2.astype(BF16), pos_rows(pos_v))
    return _nsa_attn(main, kvc, gates, batch, seq), casted


def _moba_attn_kernel(q_ref, kv_ref, o_ref, m_sc, acc_sc, selb_sc, kmean_sc, *, seq, nh):
    tq, tk = ATT_TQ, ATT_TK
    i = pl.program_id(2)
    q0 = i * tq
    n_blk = seq // MOBA_BLOCK
    heads = range(nh)
    lanes = [slice(h * LANES, (h + 1) * LANES) for h in heads]

    @pl.when(i == 0)
    def _():
        for h in heads:
            for j in range(n_blk):
                blk = kv_ref[j * MOBA_BLOCK:(j + 1) * MOBA_BLOCK, lanes[h]].astype(F32)
                kmean_sc[h, j:j + 1, :] = jnp.sum(blk, axis=0, keepdims=True) * (1.0 / MOBA_BLOCK)

    qt = q_ref[...].astype(F32).T
    zpad = jnp.zeros((HEAD_DIM, tq), F32)
    qps = [jnp.concatenate([qt[h * HEAD_DIM:(h + 1) * HEAD_DIM], zpad], axis=0).astype(BF16)
           for h in heads]
    d0 = pl.multiple_of(q0, tk)
    kvds = [kv_ref[pl.ds(d0, tk), lanes[h]] for h in heads]
    s_ds = [_dot(kvds[h], qps[h]) for h in heads]
    past = _iota((n_blk, tq), 0) < i
    for h in heads:
        score = _dot(kmean_sc[h].astype(BF16), qps[h])
        adj = jnp.where(past, score, -jnp.inf)
        sel = past & _rank_select(adj, n_blk, MOBA_TOPK)
        selb_sc[h] = jnp.where(sel, 0.0, MASKED)

    m_scs = [m_sc.at[h] for h in heads]
    acc_scs = [acc_sc.at[h] for h in heads]

    causal = _causal_bias(tk, tq)
    _flash_first_multi(kvds, [s_ds[h] + causal for h in heads], [None] * nh, m_scs, acc_scs)

    def body(kt, carry):
        k0 = pl.multiple_of(kt * tk, tk)
        _flash_step_multi([kv_ref[pl.ds(k0, tk), lanes[h]] for h in heads], qps,
                          [[selb_sc[h, pl.ds(kt, 1), :]] for h in heads], m_scs, acc_scs)
        return carry

    lax.fori_loop(0, i, body, 0)
    outs =[_flash_out(acc_scs[h][...]) for h in heads]
    o_ref[...] = jnp.concatenate(outs, axis=0).T.astype(o_ref.dtype)


def _moba_attn(q, kv, batch, seq):
    tq = ATT_TQ
    nh = MOBA_HEADS
    assert ATT_TK == tq and MOBA_BLOCK == tq and seq % tq == 0
    nq = seq // tq
    t = batch * seq
    d_heads = MOBA_HEADS * HEAD_DIM
    n_blk = seq // MOBA_BLOCK
    qw = nh * HEAD_DIM
    kvw = nh * LANES
    return pl.pallas_call(
        functools.partial(_moba_attn_kernel, seq=seq, nh=nh),
        grid=(batch, MOBA_HEADS // nh, nq),
        in_specs=[
            pl.BlockSpec((tq, qw), lambda b, hg, i: (b * nq + i, hg)),
            pl.BlockSpec((seq, kvw), lambda b, hg, i: (b, hg)),
        ],
        out_specs=pl.BlockSpec((tq, qw), lambda b, hg, i: (b * nq + i, hg)),
        out_shape=jax.ShapeDtypeStruct((t, d_heads), BF16),
        scratch_shapes=[pltpu.VMEM((nh, 1, tq), F32),
                        pltpu.VMEM((nh, ACC_ROWS, tq), F32),
                        pltpu.VMEM((nh, n_blk, tq), F32),
                        pltpu.VMEM((nh, n_blk, LANES), F32)],
        compiler_params=_cparams(("parallel", "parallel", "arbitrary")),
        name="moba_attn",
    )(q, kv)


def _moba_mixer(h, gain, w_in, batch, seq, cast, layer):
    nh = MOBA_HEADS * HEAD_DIM
    wq = w_in[:, :nh]
    wk = w_in[:, nh:2 * nh]
    wv = w_in[:, 2 * nh:]
    w_kv = _interleave_heads(wk, wv, MOBA_HEADS).astype(BF16)
    (q, kv), casted = _proj(h, gain, _q_scale_row(nh, nh), [wq.astype(BF16), w_kv], [BF16, BF16],
                            cast, layer)
    return _moba_attn(q, kv, batch, seq), casted


def _conv_kernel(*refs, n_cast):
    h_ref, g_ref, win_ref, cw_ref = refs[:4]
    c_refs = refs[4:4 + n_cast]
    o_ref = refs[4 + n_cast]
    ztail_sc = refs[-1]
    _cast_blocks(c_refs, refs[5 + n_cast:5 + 2 * n_cast])
    i = pl.program_id(1)
    d = D_MODEL
    a = _rms(h_ref[...], g_ref[...]).astype(BF16)
    ts = a.shape[0]

    @pl.when(i == 0)
    def _():
        ztail_sc[...] = jnp.zeros_like(ztail_sc)

    row = _iota((ts, MXU_TILE), 0)
    for c0 in range(0, d, MXU_TILE):
        cols = slice(c0, c0 + MXU_TILE)
        b_gate = _dot(a, win_ref[:, c0:c0 + MXU_TILE])
        z = (_dot(a, win_ref[:, d + c0:d + c0 + MXU_TILE])
             * _dot(a, win_ref[:, 2 * d + c0:2 * d + c0 + MXU_TILE]))
        zp = ztail_sc[:, cols]
        ztail_sc[:, cols] = z[ts - SUBLANES:ts]
        z1 = jnp.where(row == 0, zp[7:8], pltpu.roll(z, 1, 0))
        z2 = jnp.where(row == 0, zp[6:7], jnp.where(row == 1, zp[7:8], pltpu.roll(z, 2, 0)))
        y = cw_ref[0:1, cols] * z2 + cw_ref[1:2, cols] * z1 + cw_ref[2:3, cols] * z
        o_ref[:, cols] = (b_gate * y).astype(o_ref.dtype)


def _conv_mixer(h, gain, w_in, conv_w, batch, seq, cast, layer):
    t = batch * seq
    ts = min(ROW_TILE, seq)
    ns = seq // ts
    assert CONV_WIDTH - 1 <= SUBLANES
    cw = jnp.pad(conv_w, ((0, SUBLANES - CONV_WIDTH), (0, 0)))
    c_in, c_out, c_shapes = _cast_rider(cast, layer, batch * ns, lambda b, i: b * ns + i)
    outs = pl.pallas_call(
        functools.partial(_conv_kernel, n_cast=len(cast)),
        grid=(batch, ns),
        in_specs=[
            pl.BlockSpec((ts, D_MODEL), lambda b, i: (b * ns + i, 0)),
            pl.BlockSpec((1, D_MODEL), lambda b, i: (0, 0)),
            pl.BlockSpec(w_in.shape, lambda b, i: (0, 0)),
            pl.BlockSpec(cw.shape, lambda b, i: (0, 0)),
        ] + c_in,
        out_specs=[pl.BlockSpec((ts, D_MODEL), lambda b, i: (b * ns + i, 0))] + c_out,
        out_shape=[jax.ShapeDtypeStruct((t, D_MODEL), BF16)] + c_shapes,
        scratch_shapes=[pltpu.VMEM((SUBLANES, D_MODEL), F32)],
        compiler_params=_cparams(("arbitrary", "arbitrary")),
        name="conv_mix",
    )(h, gain.reshape(1, D_MODEL), w_in.astype(BF16), cw, *cast)
    return outs[0], outs[1:]


def kernel(x, norm_mix, norm_ffn, norm_final, ffn_w_gate, ffn_w_up, ffn_w_down, nsa_w_in, nsa_w_out, nsa_cmp_pos_k, nsa_cmp_pos_v, nsa_cmp_k_w1, nsa_cmp_k_w2, nsa_cmp_v_w1, nsa_cmp_v_w2, moba_w_in, moba_w_out, conv_w_in, conv_w, conv_w_out):
    batch, seq, d = x.shape
    depth = norm_mix.shape[0]
    h = x.reshape(batch * seq, d)
    ffn_f32 = [ffn_w_gate, ffn_w_up, ffn_w_down]
    for i in range(depth):
        kind, j = i % N_MIXERS, i // N_MIXERS
        if kind == 0:
            o, ffn_w = _nsa_mixer(h, norm_mix[i], nsa_w_in[j], nsa_cmp_pos_k[j], nsa_cmp_pos_v[j],
                                  nsa_cmp_k_w1[j], nsa_cmp_k_w2[j], nsa_cmp_v_w1[j], nsa_cmp_v_w2[j],
                                  batch, seq, ffn_f32, i)
            wo = nsa_w_out[j]
        elif kind == 1:
            o, ffn_w = _moba_mixer(h, norm_mix[i], moba_w_in[j], batch, seq, ffn_f32, i)
            wo = moba_w_out[j]
        else:
            o, ffn_w = _conv_mixer(h, norm_mix[i], conv_w_in[j], conv_w[j], batch, seq, ffn_f32, i)
            wo = conv_w_out[j]
        wg, wu, wd = ffn_w
        h = _ffn(h, o, wo.astype(BF16), norm_ffn[i], wg, wu, wd, norm_final,
                 final_norm=(i == depth - 1))
    return h.reshape(batch, seq, d)
```

```python
import functools

import jax
import jax.numpy as jnp
from jax import lax
from jax.experimental import pallas as pl
from jax.experimental.pallas import tpu as pltpu

F32 = jnp.float32
BF16 = jnp.bfloat16

D_MODEL = 1024
HEAD_DIM = 64
RMS_EPS = 1e-6
N_MIXERS = 3

NSA_HEADS = 16
NSA_GROUPS = 4
NSA_REP = NSA_HEADS // NSA_GROUPS
NSA_KV = NSA_GROUPS * HEAD_DIM
CMP_BLOCK = 32
CMP_STRIDE = 16
CMP_HIDDEN = 256
SLC_BLOCK = 64
SLC_TOPK = 16
WINDOW = 512

MOBA_HEADS = 16
MOBA_BLOCK = 256
MOBA_TOPK = 3

CONV_WIDTH = 3

LANES = 128
SUBLANES = 8
BF16_SUBLANES = 16
MXU_TILE = 256
VMEM_LIMIT = 56 * 1024 * 1024
MASKED = -1e30
M_FLOOR = -1e29
LOG2E = 1.4426950408889634

ONES_ROWS = 16
ACC_ROWS = ONES_ROWS + HEAD_DIM

ROW_TILE = 1024
ATT_TQ = 256
ATT_TK = 256


def _dot(a, b):
    return jnp.dot(a, b, preferred_element_type=F32)


def _rms(x, g):
    y = x * lax.rsqrt(jnp.mean(x * x, axis=-1, keepdims=True) + RMS_EPS)
    return y * g


def _cparams(sem):
    return pltpu.CompilerParams(dimension_semantics=sem, vmem_limit_bytes=VMEM_LIMIT)


def _cast_rider(stacked, layer, n_steps, step_of):
    in_specs, out_specs, out_shapes = [], [], []
    for w in stacked:
        _, r, c = w.shape
        rb = next(b for b in range(BF16_SUBLANES, r + 1, BF16_SUBLANES)
                  if r % b == 0 and r // b <= n_steps)
        last = r // rb - 1
        out_idx = lambda *g, last=last: (jnp.minimum(step_of(*g), last), 0)
        in_idx = lambda *g, last=last: (layer, jnp.minimum(step_of(*g), last), 0)
        in_specs.append(pl.BlockSpec((None, rb, c), in_idx))
        out_specs.append(pl.BlockSpec((rb, c), out_idx))
        out_shapes.append(jax.ShapeDtypeStruct((r, c), BF16))
    return in_specs, out_specs, out_shapes


def _cast_blocks(w_refs, o_refs):
    for w_ref, o_ref in zip(w_refs, o_refs):
        o_ref[...] = w_ref[...].astype(o_ref.dtype)


def _proj_kernel(*refs, n_out, n_cast, chunk, mean_of, mean_rows):
    x_ref, g_ref, sc_ref = refs[0], refs[1], refs[2]
    w_refs = refs[3:3 + n_out]
    c_refs = refs[3 + n_out:3 + n_out + n_cast]
    o_refs = refs[3 + n_out + n_cast:3 + 2 * n_out + n_cast]
    _cast_blocks(c_refs, refs[3 + 2 * n_out + n_cast:3 + 2 * n_out + 2 * n_cast])
    a = _rms(x_ref[...], g_ref[...]).astype(BF16)
    for k, (w_ref, o_ref) in enumerate(zip(w_refs, o_refs)):
        n = w_ref.shape[1]
        for n0 in range(0, n, chunk):
            n1 = min(n0 + chunk, n)
            y = _dot(a, w_ref[:, n0:n1])
            if k == 0:
                y = y * sc_ref[:, n0:n1]
            o_ref[:, n0:n1] = y.astype(o_ref.dtype)
            if k == mean_of:
                blocks = y.reshape(y.shape[0] // mean_rows, mean_rows, n1 - n0)
                refs[-1][0, :, n0:n1] = jnp.sum(blocks, axis=1) * (1.0 / mean_rows)


def _proj(h, gain, scale0, weights, out_dtypes, cast, layer, mean_of=None, mean_rows=None):
    t = h.shape[0]
    tm = min(ROW_TILE, t)
    n_out = len(weights)
    n_steps = t // tm
    const = lambda a: pl.BlockSpec(a.shape, lambda i: (0, 0))
    gain = gain.reshape(1, D_MODEL)
    c_in, c_out, c_shapes = _cast_rider(cast, layer, n_steps, lambda i: i)
    in_specs = [pl.BlockSpec((tm, D_MODEL), lambda i: (i, 0)), const(gain), const(scale0)]
    in_specs += [const(w) for w in weights] + c_in
    out_specs = [pl.BlockSpec((tm, w.shape[1]), lambda i: (i, 0)) for w in weights] + c_out
    out_shape = [jax.ShapeDtypeStruct((t, w.shape[1]), dt) for w, dt in zip(weights, out_dtypes)]
    out_shape += c_shapes
    if mean_of is not None:
        per_step, cols = tm // mean_rows, weights[mean_of].shape[1]
        out_specs.append(pl.BlockSpec((1, per_step, cols), lambda i: (i, 0, 0)))
        out_shape.append(jax.ShapeDtypeStruct((n_steps, per_step, cols), F32))
    outs = pl.pallas_call(
        functools.partial(_proj_kernel, n_out=n_out, n_cast=len(cast), chunk=512,
                          mean_of=mean_of, mean_rows=mean_rows),
        grid=(n_steps,),
        in_specs=in_specs, out_specs=out_specs, out_shape=out_shape,
        compiler_params=_cparams(("arbitrary",)),
        name="proj",
    )(h, gain, scale0, *weights, *cast)
    n_c = len(cast)
    means = outs[n_out + n_c].reshape(t // mean_rows, -1) if mean_of is not None else None
    return outs[:n_out], outs[n_out:n_out + n_c], means


def _q_scale_row(n_q, n_total):
    q = jnp.full((1, n_q), (HEAD_DIM ** -0.5) * LOG2E, F32)
    return jnp.concatenate([q, jnp.ones((1, n_total - n_q), F32)], axis=1)


def _ffn_kernel(h_ref, o_ref, wo_ref, g_ref, wg_ref, wu_ref, wd_ref, gf_ref, out_ref,
                *, final_norm, chunks):
    h1 = h_ref[...] + _dot(o_ref[...], wo_ref[...])
    a = _rms(h1, g_ref[...]).astype(BF16)
    out_ref[...] = h1
    for c0, c1 in chunks:
        gate = _dot(a, wg_ref[:, c0:c1])
        up = _dot(a, wu_ref[:, c0:c1])
        t = (jax.nn.silu(gate) * up).astype(BF16)
        out_ref[...] += _dot(t, wd_ref[c0:c1, :])
    if final_norm:
        out_ref[...] = _rms(out_ref[...], gf_ref[...])


def _ffn(h, o, wo, gain, wg, wu, wd, final_gain, final_norm):
    t = h.shape[0]
    tm = min(ROW_TILE, t)
    d_ff = wg.shape[1]
    step = 3 * MXU_TILE
    chunks = tuple((c0, min(c0 + step, d_ff)) for c0 in range(0, d_ff, step))
    assert d_ff % MXU_TILE == 0
    gain = gain.reshape(1, D_MODEL)
    final_gain = final_gain.reshape(1, D_MODEL)
    const = lambda a: pl.BlockSpec(a.shape, lambda i: (0, 0), pipeline_mode=pl.Buffered(1))
    return pl.pallas_call(
        functools.partial(_ffn_kernel, final_norm=final_norm, chunks=chunks),
        grid=(t // tm,),
        in_specs=[
            pl.BlockSpec((tm, D_MODEL), lambda i: (i, 0)),
            pl.BlockSpec((tm, o.shape[1]), lambda i: (i, 0)),
            const(wo), const(gain), const(wg), const(wu), const(wd), const(final_gain),
        ],
        out_specs=pl.BlockSpec((tm, D_MODEL), lambda i: (i, 0)),
        out_shape=jax.ShapeDtypeStruct((t, D_MODEL), F32),
        compiler_params=_cparams(("parallel",)),
        name="ffn",
    )(h, o, wo, gain, wg, wu, wd, final_gain)


def _block_max(s, rows):
    if rows is None:
        return jnp.max(s, axis=0, keepdims=True)
    h = s.shape[0] // len(rows)
    ms = [jnp.max(s[a * h:(a + 1) * h].reshape(h // SUBLANES, SUBLANES, s.shape[1]), axis=0) + row
          for a, row in enumerate(rows)]
    return jnp.max(functools.reduce(jnp.maximum, ms), axis=0, keepdims=True)


def _block_exp2(s, rows, m):
    if rows is None:
        return jnp.exp2(s - m)
    h = s.shape[0] // len(rows)
    return jnp.concatenate([jnp.exp2(s[a * h:(a + 1) * h] - (m - row))
                            for a, row in enumerate(rows)], axis=0)


def _flash_step_multi(kvs, qps, rows, m_scs, acc_scs):
    n = len(kvs)
    ss = [_dot(kvs[c], qps[c]) for c in range(n)]
    m_olds = [m_scs[c][...] for c in range(n)]
    acc_olds = [acc_scs[c][...] for c in range(n)]
    m_news = [jnp.maximum(m_olds[c], _block_max(ss[c], rows[c])) for c in range(n)]
    alphas = [jnp.exp2(m_olds[c] - m_news[c]) for c in range(n)]
    ps = [_block_exp2(ss[c], rows[c], m_news[c]) for c in range(n)]
    pvs = [_pv(kvs[c], ps[c]) for c in range(n)]
    acc_news = [alphas[c] * acc_olds[c] + pvs[c] for c in range(n)]
    for c in range(n):
        m_scs[c][...] = m_news[c]
        acc_scs[c][...] = acc_news[c]


def _flash_first_multi(kvs, ss, rows, m_scs, acc_scs):
    n = len(kvs)
    ms = [jnp.maximum(_block_max(ss[c], rows[c]), M_FLOOR) for c in range(n)]
    ps = [_block_exp2(ss[c], rows[c], ms[c]) for c in range(n)]
    accs = [_pv(kvs[c], ps[c]) for c in range(n)]
    for c in range(n):
        m_scs[c][...] = ms[c]
        acc_scs[c][...] = accs[c]


def _pv(kv, p):
    tk = kv.shape[0]
    lhs = jnp.concatenate([jnp.ones((ONES_ROWS, tk), BF16), kv.T[HEAD_DIM:2 * HEAD_DIM]], axis=0)
    return _dot(lhs, p.astype(BF16))


def _flash_out(acc):
    l = acc[0:1, :]
    return acc[ONES_ROWS:ACC_ROWS, :] / jnp.where(l > 0, l, 1.0)


def _iota(shape, dim):
    return lax.broadcasted_iota(jnp.int32, shape, dim)


def _causal_bias(tk, tq):
    ok = _iota((tk, tq), 0) <= _iota((tk, tq), 1)
    return jnp.where(ok, 0.0, MASKED)


def _rank_select(adj, n_rows, topk):
    j = _iota(adj.shape, 0)
    rank = jnp.zeros(adj.shape, jnp.int32)
    for k in range(n_rows):
        rk = adj[k:k + 1, :]
        beats = (rk > adj) | ((rk == adj) & (j > k))
        rank = rank + beats.astype(jnp.int32)
    return rank < topk


def _nsa_cmp_kernel(*refs):
    n_slab = 2 * NSA_KV // LANES
    x_refs = refs[:n_slab]
    w1k_ref, w2k_ref, pk_ref, w1v_ref, w2v_ref, pv_ref, o_ref = refs[n_slab:]
    per_slab = LANES // HEAD_DIM
    nb, _, n_seg, _ = o_ref.shape
    seq = n_seg * CMP_STRIDE
    for t, x_ref in enumerate(x_refs):
        is_v = t >= n_slab // 2
        w1_ref, w2_ref, p_ref = (w1v_ref, w2v_ref, pv_ref) if is_v else (w1k_ref, w2k_ref, pk_ref)
        lane0 = HEAD_DIM if is_v else 0
        tok = [jnp.concatenate([x_ref[pl.ds(bb * seq + l, n_seg, stride=CMP_STRIDE), :]
                                for bb in range(nb)], axis=0) for l in range(CMP_STRIDE)]
        pp = _dot(p_ref[...], w1_ref[...])
        posb = pp[0:1, :CMP_HIDDEN] + pp[1:2, CMP_HIDDEN:]
        for e in range(per_slab):
            g = (t % (n_slab // 2)) * per_slab + e
            x = jnp.concatenate([tk_[:, e * HEAD_DIM:(e + 1) * HEAD_DIM] for tk_ in tok],
                                axis=1).astype(BF16)
            acc = _dot(x, w1_ref[...])
            hid = acc[:, :CMP_HIDDEN] + pltpu.roll(acc[:, CMP_HIDDEN:], nb * n_seg - 1, 0) + posb
            y = _dot(jax.nn.gelu(hid).astype(BF16), w2_ref[...])
            for bb in range(nb):
                o_ref[bb, g, :, lane0:lane0 + HEAD_DIM] = (
                    y[bb * n_seg:(bb + 1) * n_seg].astype(o_ref.dtype))


def _nsa_cmp(cmp_in, batch, seq, w1k, w2k, pk, w1v, w2v, pv):
    full = lambda a: pl.BlockSpec(a.shape, lambda i: (0,) * a.ndim)
    n_seg = seq // CMP_STRIDE
    n_slab = 2 * NSA_KV // LANES
    nb = max(d for d in (1, 2, 4) if batch % d == 0)
    slabs = [pl.BlockSpec((nb * seq, LANES), functools.partial(lambda i, t: (i, t), t=t))
             for t in range(n_slab)]
    return pl.pallas_call(
        _nsa_cmp_kernel,
        grid=(batch // nb,),
        in_specs=slabs + [full(w1k), full(w2k), full(pk), full(w1v), full(w2v), full(pv)],
        out_specs=pl.BlockSpec((nb, NSA_GROUPS, n_seg, 2 * HEAD_DIM), lambda i: (i, 0, 0, 0)),
        out_shape=jax.ShapeDtypeStruct((batch, NSA_GROUPS, n_seg, 2 * HEAD_DIM), BF16),
        compiler_params=_cparams(("parallel",)),
        name="nsa_cmp",
    )(*([cmp_in] * n_slab), w1k, w2k, pk, w1v, w2v, pv)


def _nsa_attn_kernel(q_ref, kvs_ref, kvw_ref, kvc_ref, gt_ref, o_ref,
                     m_sc, acc_sc, selb_sc, out_sc, *, seq, ng):
    tq, tk = ATT_TQ, ATT_TK
    i = pl.program_id(2)
    q0 = i * tq
    n_slc = seq // SLC_BLOCK
    n_cmp_pad = seq // CMP_STRIDE
    rep = NSA_REP
    sub = tk // SLC_BLOCK
    parts = tq // LANES
    win_rows = WINDOW + LANES
    groups = range(ng)
    lanes = [slice(g * LANES, (g + 1) * LANES) for g in groups]

    def heads(x):
        return jnp.concatenate([x] * rep, axis=1)

    def part_lanes(x, part):
        return jnp.concatenate([x[:, (r * parts + part) * LANES:(r * parts + part + 1) * LANES]
                                for r in range(rep)], axis=1)

    def head_major(xs):
        return jnp.concatenate([xs[p][:, r * LANES:(r + 1) * LANES]
                                for r in range(rep) for p in range(parts)], axis=1)

    qt = q_ref[...].astype(F32).T
    zpad = jnp.zeros((HEAD_DIM, tq), F32)
    qps = [jnp.concatenate(
        [jnp.concatenate([qt[(g * rep + r) * HEAD_DIM:(g * rep + r + 1) * HEAD_DIM], zpad], axis=0)
         for r in range(rep)], axis=1).astype(BF16) for g in groups]
    gates = jax.nn.sigmoid(gt_ref[...]).T

    def emit(g, branch, o_t, first):
        for r in range(rep):
            row = branch * NSA_HEADS + g * rep + r
            val = gates[row:row + 1, :] * o_t[:, r * tq:(r + 1) * tq]
            sl = slice((g * rep + r) * HEAD_DIM, (g * rep + r + 1) * HEAD_DIM)
            if first:
                out_sc[sl, :] = val
            else:
                out_sc[sl, :] += val

    d0 = pl.multiple_of(q0, tk)
    kvcs = [kvc_ref[0, g] for g in groups]
    kvds = [kvs_ref[pl.ds(d0, tk), lanes[g]] for g in groups]
    s_cs = [_dot(kvcs[g], qps[g]) for g in groups]
    kvws, s_ws, w_biases = [], [], []
    for part in range(parts):
        h0 = q0 + part * LANES
        w0 = pl.multiple_of(jnp.maximum(h0 - WINDOW, 0), LANES)
        k_pos = w0 + _iota((win_rows, LANES), 0)
        t_pos = h0 + _iota((win_rows, LANES), 1)
        ok = (k_pos <= t_pos) & (k_pos > t_pos - WINDOW)
        w_biases.append(jnp.concatenate([jnp.where(ok, 0.0, MASKED)] * rep, axis=1))
        kvws.append([kvw_ref[pl.ds(w0, win_rows), lanes[g]] for g in groups])
        s_ws.append([_dot(kvws[part][g], part_lanes(qps[g], part)) for g in groups])

    c_end = _iota((n_cmp_pad, tq), 0) * CMP_STRIDE + (CMP_BLOCK - 1)
    c_ok = (c_end <= q0 + _iota((n_cmp_pad, tq), 1)) & (c_end < seq)
    c_bias = heads(jnp.where(c_ok, 0.0, MASKED))

    jj = _iota((n_slc, n_cmp_pad), 0) * SLC_BLOCK
    cc = _iota((n_slc, n_cmp_pad), 1) * CMP_STRIDE
    overlap_t = jnp.where((cc < jj + SLC_BLOCK) & (cc + CMP_BLOCK > jj) & (cc + CMP_BLOCK <= seq),
                          1.0, 0.0).astype(BF16)
    j_blk = _iota((n_slc, tq), 0)
    cur = (q0 + _iota((n_slc, tq), 1)) // SLC_BLOCK
    forced = (j_blk == 0) | (j_blk == cur) | (j_blk == cur - 1)
    valid = j_blk <= cur

    ss = [s_cs[g] + c_bias for g in groups]
    ms = [jnp.maximum(jnp.max(s, axis=0, keepdims=True), M_FLOOR) for s in ss]
    es = [jnp.exp2(ss[g] - ms[g]) for g in groups]
    ds = [jnp.sum(e, axis=0, keepdims=True) for e in es]
    p_cs = [es[g] / jnp.where(ds[g] > 0, ds[g], 1.0) for g in groups]
    o_cs = [_pv(kvcs[g], p_cs[g])[ONES_ROWS:ACC_ROWS, :] for g in groups]

    o_parts = []
    for part in range(parts):
        ss = [s_ws[part][g] + w_biases[part] for g in groups]
        ms = [jnp.max(s, axis=0, keepdims=True) for s in ss]
        o_parts.append([_flash_out(_pv(kvws[part][g], jnp.exp2(ss[g] - ms[g]))) for g in groups])
    o_ws = [head_major([o_parts[p][g] for p in range(parts)]) for g in groups]

    for g in groups:
        emit(g, 0, o_cs[g], True)
        emit(g, 2, o_ws[g], False)
        selb_sc[g] = jnp.where(valid, 0.0, MASKED)

    @pl.when((i + 1) * (tq // SLC_BLOCK) > SLC_TOPK)
    def _():
        selbs = []
        for g in groups:
            p_sum = p_cs[g][:, 0:tq]
            for r in range(1, rep):
                p_sum = p_sum + p_cs[g][:, r * tq:(r + 1) * tq]
            p_hi = p_sum.astype(BF16)
            p_lo = (p_sum - p_hi.astype(F32)).astype(BF16)
            imp = _dot(overlap_t, p_hi) + _dot(overlap_t, p_lo)
            adj = jnp.where(forced, jnp.inf, jnp.where(valid, imp, -jnp.inf))
            sel = valid & _rank_select(adj, n_slc, min(SLC_TOPK, n_slc))
            selbs.append(jnp.where(sel, 0.0, MASKED))
        for g in groups:
            selb_sc[g] = selbs[g]

    def sel_rows(g, blk0):
        return [heads(selb_sc[g, pl.ds(blk0 + a, 1), :]) for a in range(sub)]

    m_scs = [m_sc.at[g] for g in groups]
    acc_scs = [acc_sc.at[g] for g in groups]

    d_rows = [sel_rows(g, i * sub) for g in groups]
    blk_per_part = LANES // SLC_BLOCK
    d_kvs, d_ss, d_rws = [], [], []
    for part in range(parts):
        nk = (part + 1) * LANES
        ok = _iota((nk, LANES), 0) <= part * LANES + _iota((nk, LANES), 1)
        bias = jnp.concatenate([jnp.where(ok, 0.0, MASKED)] * rep, axis=1)
        d_kvs.append([kvds[g][:nk] for g in groups])
        d_ss.append([_dot(d_kvs[part][g], part_lanes(qps[g], part)) + bias for g in groups])
        d_rws.append([[part_lanes(row, part) for row in d_rows[g][:(part + 1) * blk_per_part]]
                      for g in groups])
    d_ms = [[jnp.maximum(_block_max(d_ss[p][g], d_rws[p][g]), M_FLOOR) for g in groups]
            for p in range(parts)]
    d_ps = [[_block_exp2(d_ss[p][g], d_rws[p][g], d_ms[p][g]) for g in groups] for p in range(parts)]
    d_accs = [[_pv(d_kvs[p][g], d_ps[p][g]) for g in groups] for p in range(parts)]

    for g in groups:
        m_scs[g][...] = head_major([d_ms[p][g] for p in range(parts)])
        acc_scs[g][...] = head_major([d_accs[p][g] for p in range(parts)])

    def slc_body(kt, carry):
        k0 = pl.multiple_of(kt * tk, tk)
        _flash_step_multi([kvs_ref[pl.ds(k0, tk), lanes[g]] for g in groups], qps,
                          [sel_rows(g, kt * sub) for g in groups], m_scs, acc_scs)
        return carry

    lax.fori_loop(0, i, slc_body, 0)
    for g in groups:
        emit(g, 1, _flash_out(acc_scs[g][...]), False)

    o_ref[...] = out_sc[...].T.astype(o_ref.dtype)


def _nsa_attn(main, kvc, gates, batch, seq):
    tq = ATT_TQ
    ng = NSA_GROUPS
    assert ATT_TK == tq and WINDOW % tq == 0 and seq % tq == 0 and seq >= WINDOW + tq
    assert tq % SLC_BLOCK == 0
    nq = seq // tq
    t = batch * seq
    gw = ng * NSA_REP * HEAD_DIM
    kvw = ng * LANES
    assert (NSA_HEADS * HEAD_DIM) % kvw == 0
    kv0 = NSA_HEADS * HEAD_DIM // kvw
    m = NSA_REP * tq
    n_cmp_pad = seq // CMP_STRIDE
    return pl.pallas_call(
        functools.partial(_nsa_attn_kernel, seq=seq, ng=ng),
        grid=(batch, NSA_GROUPS // ng, nq),
        in_specs=[
            pl.BlockSpec((tq, gw), lambda b, g, i: (b * nq + i, g)),
            pl.BlockSpec((seq, kvw), lambda b, g, i: (b, kv0 + g)),
            pl.BlockSpec((seq, kvw), lambda b, g, i: (b, kv0 + NSA_GROUPS // ng + g)),
            pl.BlockSpec((1, ng, n_cmp_pad, LANES), lambda b, g, i: (b, g, 0, 0)),
            pl.BlockSpec((tq, LANES), lambda b, g, i: (b * nq + i, 0)),
        ],
        out_specs=pl.BlockSpec((tq, gw), lambda b, g, i: (b * nq + i, g)),
        out_shape=jax.ShapeDtypeStruct((t, NSA_HEADS * HEAD_DIM), BF16),
        scratch_shapes=[pltpu.VMEM((ng, 1, m), F32),
                        pltpu.VMEM((ng, ACC_ROWS, m), F32),
                        pltpu.VMEM((ng, seq // SLC_BLOCK, tq), F32),
                        pltpu.VMEM((gw, tq), F32)],
        compiler_params=_cparams(("parallel", "parallel", "arbitrary")),
        name="nsa_attn",
    )(main, main, main, kvc, gates)


def _interleave_heads(wa, wb, n):
    d = wa.shape[0]
    return jnp.stack([wa.reshape(d, n, HEAD_DIM), wb.reshape(d, n, HEAD_DIM)],
                     axis=2).reshape(d, 2 * n * HEAD_DIM)


def _nsa_mixer(h, gain, w_in, pos_k, pos_v, k_w1, k_w2, v_w1, v_w2, batch, seq, cast, layer):
    nh = NSA_HEADS * HEAD_DIM
    c = [nh + i * NSA_KV for i in range(7)]
    wq = w_in[:, :c[0]]
    wkc, wvc, wks, wvs, wkw, wvw = (w_in[:, c[i]:c[i + 1]] for i in range(6))
    w_gate = w_in[:, c[6]:]
    w_main = jnp.concatenate(
        [wq, _interleave_heads(wks, wvs, NSA_GROUPS), _interleave_heads(wkw, wvw, NSA_GROUPS)],
        axis=1).astype(BF16)
    w_cmp = jnp.concatenate([wkc, wvc], axis=1).astype(BF16)
    wg = jnp.pad(w_gate, ((0, 0), (0, LANES - 3 * NSA_HEADS))).astype(BF16)

    (main, gates, cmp_in), casted, _ = _proj(h, gain, _q_scale_row(nh, w_main.shape[1]),
                                             [w_main, wg, w_cmp], [BF16, F32, F32], cast, layer)

    half = CMP_STRIDE * HEAD_DIM

    def seg_weights(w1):
        return jnp.concatenate([w1[:half], w1[half:]], axis=1).astype(BF16)

    def pos_rows(p):
        rows = p.reshape(2, half)
        return jnp.pad(rows, ((0, SUBLANES - 2), (0, 0))).astype(BF16)

    kvc = _nsa_cmp(cmp_in, batch, seq, seg_weights(k_w1), k_w2.astype(BF16), pos_rows(pos_k),
                   seg_weights(v_w1), v_w2.astype(BF16), pos_rows(pos_v))
    return _nsa_attn(main, kvc, gates, batch, seq), casted


def _moba_attn_kernel(q_ref, kv_ref, kmean_ref, o_ref, m_sc, acc_sc, selb_sc, *, seq, nh):
    tq, tk = ATT_TQ, ATT_TK
    i = pl.program_id(2)
    q0 = i * tq
    n_blk = seq // MOBA_BLOCK
    heads = range(nh)
    lanes = [slice(h * LANES, (h + 1) * LANES) for h in heads]

    qt = q_ref[...].astype(F32).T
    zpad = jnp.zeros((HEAD_DIM, tq), F32)
    qps = [jnp.concatenate([qt[h * HEAD_DIM:(h + 1) * HEAD_DIM], zpad], axis=0).astype(BF16)
           for h in heads]
    d0 = pl.multiple_of(q0, tk)
    kvds = [kv_ref[pl.ds(d0, tk), lanes[h]] for h in heads]
    s_ds = [_dot(kvds[h], qps[h]) for h in heads]
    past = _iota((n_blk, tq), 0) < i
    for h in heads:
        score = _dot(kmean_ref[0, :, lanes[h]].astype(BF16), qps[h])
        adj = jnp.where(past, score, -jnp.inf)
        sel = past & _rank_select(adj, n_blk, MOBA_TOPK)
        selb_sc[h] = jnp.where(sel, 0.0, MASKED)

    m_scs = [m_sc.at[h] for h in heads]
    acc_scs = [acc_sc.at[h] for h in heads]

    causal = _causal_bias(tk, tq)
    _flash_first_multi(kvds, [s_ds[h] + causal for h in heads], [None] * nh, m_scs, acc_scs)

    def body(kt, carry):
        k0 = pl.multiple_of(kt * tk, tk)
        _flash_step_multi([kv_ref[pl.ds(k0, tk), lanes[h]] for h in heads], qps,
                          [[selb_sc[h, pl.ds(kt, 1), :]] for h in heads], m_scs, acc_scs)
        return carry

    lax.fori_loop(0, i, body, 0)
    outs =[_flash_out(acc_scs[h][...]) for h in heads]
    o_ref[...] = jnp.concatenate(outs, axis=0).T.astype(o_ref.dtype)


def _moba_attn(q, kv, kv_mean, batch, seq):
    tq = ATT_TQ
    nh = MOBA_HEADS
    assert ATT_TK == tq and MOBA_BLOCK == tq and seq % tq == 0
    nq = seq // tq
    t = batch * seq
    d_heads = MOBA_HEADS * HEAD_DIM
    n_blk = seq // MOBA_BLOCK
    qw = nh * HEAD_DIM
    kvw = nh * LANES
    return pl.pallas_call(
        functools.partial(_moba_attn_kernel, seq=seq, nh=nh),
        grid=(batch, MOBA_HEADS // nh, nq),
        in_specs=[
            pl.BlockSpec((tq, qw), lambda b, hg, i: (b * nq + i, hg)),
            pl.BlockSpec((seq, kvw), lambda b, hg, i: (b, hg)),
            pl.BlockSpec((1, n_blk, kvw), lambda b, hg, i: (b, 0, hg)),
        ],
        out_specs=pl.BlockSpec((tq, qw), lambda b, hg, i: (b * nq + i, hg)),
        out_shape=jax.ShapeDtypeStruct((t, d_heads), BF16),
        scratch_shapes=[pltpu.VMEM((nh, 1, tq), F32),
                        pltpu.VMEM((nh, ACC_ROWS, tq), F32),
                        pltpu.VMEM((nh, n_blk, tq), F32)],
        compiler_params=_cparams(("parallel", "parallel", "arbitrary")),
        name="moba_attn",
    )(q, kv, kv_mean.reshape(batch, n_blk, -1))


def _moba_mixer(h, gain, w_in, batch, seq, cast, layer):
    nh = MOBA_HEADS * HEAD_DIM
    wq = w_in[:, :nh]
    wk = w_in[:, nh:2 * nh]
    wv = w_in[:, 2 * nh:]
    w_kv = _interleave_heads(wk, wv, MOBA_HEADS).astype(BF16)
    (q, kv), casted, kv_mean = _proj(h, gain, _q_scale_row(nh, nh), [wq.astype(BF16), w_kv],
                                     [BF16, BF16], cast, layer, mean_of=1, mean_rows=MOBA_BLOCK)
    return _moba_attn(q, kv, kv_mean, batch, seq), casted


def _conv_kernel(*refs, n_cast):
    h_ref, g_ref, win_ref, cw_ref = refs[:4]
    c_refs = refs[4:4 + n_cast]
    o_ref = refs[4 + n_cast]
    ztail_sc = refs[-1]
    _cast_blocks(c_refs, refs[5 + n_cast:5 + 2 * n_cast])
    i = pl.program_id(1)
    d = D_MODEL
    a = _rms(h_ref[...], g_ref[...]).astype(BF16)
    ts = a.shape[0]

    @pl.when(i == 0)
    def _():
        ztail_sc[...] = jnp.zeros_like(ztail_sc)

    row = _iota((ts, MXU_TILE), 0)
    for c0 in range(0, d, MXU_TILE):
        cols = slice(c0, c0 + MXU_TILE)
        b_gate = _dot(a, win_ref[:, c0:c0 + MXU_TILE])
        z = (_dot(a, win_ref[:, d + c0:d + c0 + MXU_TILE])
             * _dot(a, win_ref[:, 2 * d + c0:2 * d + c0 + MXU_TILE]))
        zp = ztail_sc[:, cols]
        ztail_sc[:, cols] = z[ts - SUBLANES:ts]
        z1 = jnp.where(row == 0, zp[7:8], pltpu.roll(z, 1, 0))
        z2 = jnp.where(row == 0, zp[6:7], jnp.where(row == 1, zp[7:8], pltpu.roll(z, 2, 0)))
        y = cw_ref[0:1, cols] * z2 + cw_ref[1:2, cols] * z1 + cw_ref[2:3, cols] * z
        o_ref[:, cols] = (b_gate * y).astype(o_ref.dtype)


def _conv_mixer(h, gain, w_in, conv_w, batch, seq, cast, layer):
    t = batch * seq
    ts = min(ROW_TILE, seq)
    ns = seq // ts
    assert CONV_WIDTH - 1 <= SUBLANES
    cw = jnp.pad(conv_w, ((0, SUBLANES - CONV_WIDTH), (0, 0)))
    c_in, c_out, c_shapes = _cast_rider(cast, layer, batch * ns, lambda b, i: b * ns + i)
    outs = pl.pallas_call(
        functools.partial(_conv_kernel, n_cast=len(cast)),
        grid=(batch, ns),
        in_specs=[
            pl.BlockSpec((ts, D_MODEL), lambda b, i: (b * ns + i, 0)),
            pl.BlockSpec((1, D_MODEL), lambda b, i: (0, 0)),
            pl.BlockSpec(w_in.shape, lambda b, i: (0, 0)),
            pl.BlockSpec(cw.shape, lambda b, i: (0, 0)),
        ] + c_in,
        out_specs=[pl.BlockSpec((ts, D_MODEL), lambda b, i: (b * ns + i, 0))] + c_out,
        out_shape=[jax.ShapeDtypeStruct((t, D_MODEL), BF16)] + c_shapes,
        scratch_shapes=[pltpu.VMEM((SUBLANES, D_MODEL), F32)],
        compiler_params=_cparams(("arbitrary", "arbitrary")),
        name="conv_mix",
    )(h, gain.reshape(1, D_MODEL), w_in.astype(BF16), cw, *cast)
    return outs[0], outs[1:]


def kernel(x, norm_mix, norm_ffn, norm_final, ffn_w_gate, ffn_w_up, ffn_w_down, nsa_w_in, nsa_w_out, nsa_cmp_pos_k, nsa_cmp_pos_v, nsa_cmp_k_w1, nsa_cmp_k_w2, nsa_cmp_v_w1, nsa_cmp_v_w2, moba_w_in, moba_w_out, conv_w_in, conv_w, conv_w_out):
    batch, seq, d = x.shape
    depth = norm_mix.shape[0]
    h = x.reshape(batch * seq, d)
    ffn_f32 = [ffn_w_gate, ffn_w_up, ffn_w_down]
    for i in range(depth):
        kind, j = i % N_MIXERS, i // N_MIXERS
        if kind == 0:
            o, ffn_w = _nsa_mixer(h, norm_mix[i], nsa_w_in[j], nsa_cmp_pos_k[j], nsa_cmp_pos_v[j],
                                  nsa_cmp_k_w1[j], nsa_cmp_k_w2[j], nsa_cmp_v_w1[j], nsa_cmp_v_w2[j],
                                  batch, seq, ffn_f32, i)
            wo = nsa_w_out[j]
        elif kind == 1:
            o, ffn_w = _moba_mixer(h, norm_mix[i], moba_w_in[j], batch, seq, ffn_f32, i)
            wo = moba_w_out[j]
        else:
            o, ffn_w = _conv_mixer(h, norm_mix[i], conv_w_in[j], conv_w[j], batch, seq, ffn_f32, i)
            wo = conv_w_out[j]
        wg, wu, wd = ffn_w
        h = _ffn(h, o, wo.astype(BF16), norm_ffn[i], wg, wu, wd, norm_final,
                 final_norm=(i == depth - 1))
    return h.reshape(batch, seq, d)
```

```python
import functools

import jax
import jax.numpy as jnp
from jax import lax
from jax.experimental import pallas as pl
from jax.experimental.pallas import tpu as pltpu

F32 = jnp.float32
BF16 = jnp.bfloat16

D_MODEL = 1024
HEAD_DIM = 64
RMS_EPS = 1e-6
N_MIXERS = 3

NSA_HEADS = 16
NSA_GROUPS = 4
NSA_REP = NSA_HEADS // NSA_GROUPS
NSA_KV = NSA_GROUPS * HEAD_DIM
CMP_BLOCK = 32
CMP_STRIDE = 16
CMP_HIDDEN = 256
SLC_BLOCK = 64
SLC_TOPK = 16
WINDOW = 512

MOBA_HEADS = 16
MOBA_BLOCK = 256
MOBA_TOPK = 3

CONV_WIDTH = 3

LANES = 128
SUBLANES = 8
BF16_SUBLANES = 16
MXU_TILE = 256
VMEM_LIMIT = 56 * 1024 * 1024
MASKED = -1e30
M_FLOOR = -1e29
LOG2E = 1.4426950408889634

ONES_ROWS = 16
ACC_ROWS = ONES_ROWS + HEAD_DIM

ROW_TILE = 1024
ATT_TQ = 256
ATT_TK = 256


def _dot(a, b):
    return jnp.dot(a, b, preferred_element_type=F32)


def _rms(x, g):
    y = x * lax.rsqrt(jnp.mean(x * x, axis=-1, keepdims=True) + RMS_EPS)
    return y * g


def _cparams(sem):
    return pltpu.CompilerParams(dimension_semantics=sem, vmem_limit_bytes=VMEM_LIMIT)


def _cast_rider(stacked, layer, n_steps, step_of):
    in_specs, out_specs, out_shapes = [], [], []
    for w in stacked:
        _, r, c = w.shape
        rb = next(b for b in range(BF16_SUBLANES, r + 1, BF16_SUBLANES)
                  if r % b == 0 and r // b <= n_steps)
        last = r // rb - 1
        out_idx = lambda *g, last=last: (jnp.minimum(step_of(*g), last), 0)
        in_idx = lambda *g, last=last: (layer, jnp.minimum(step_of(*g), last), 0)
        in_specs.append(pl.BlockSpec((None, rb, c), in_idx))
        out_specs.append(pl.BlockSpec((rb, c), out_idx))
        out_shapes.append(jax.ShapeDtypeStruct((r, c), BF16))
    return in_specs, out_specs, out_shapes


def _cast_blocks(w_refs, o_refs):
    for w_ref, o_ref in zip(w_refs, o_refs):
        o_ref[...] = w_ref[...].astype(o_ref.dtype)


def _proj_kernel(*refs, n_out, n_cast, chunk, mean_of, mean_rows):
    x_ref, g_ref, sc_ref = refs[0], refs[1], refs[2]
    w_refs = refs[3:3 + n_out]
    c_refs = refs[3 + n_out:3 + n_out + n_cast]
    o_refs = refs[3 + n_out + n_cast:3 + 2 * n_out + n_cast]
    _cast_blocks(c_refs, refs[3 + 2 * n_out + n_cast:3 + 2 * n_out + 2 * n_cast])
    a = _rms(x_ref[...], g_ref[...]).astype(BF16)
    for k, (w_ref, o_ref) in enumerate(zip(w_refs, o_refs)):
        n = w_ref.shape[1]
        for n0 in range(0, n, chunk):
            n1 = min(n0 + chunk, n)
            y = _dot(a, w_ref[:, n0:n1])
            if k == 0:
                y = y * sc_ref[:, n0:n1]
            o_ref[:, n0:n1] = y.astype(o_ref.dtype)
            if k == mean_of:
                blocks = y.reshape(y.shape[0] // mean_rows, mean_rows, n1 - n0)
                refs[-1][0, :, n0:n1] = jnp.sum(blocks, axis=1) * (1.0 / mean_rows)


def _proj(h, gain, scale0, weights, out_dtypes, cast, layer, mean_of=None, mean_rows=None):
    t = h.shape[0]
    tm = min(ROW_TILE, t)
    n_out = len(weights)
    n_steps = t // tm
    const = lambda a: pl.BlockSpec(a.shape, lambda i: (0, 0))
    gain = gain.reshape(1, D_MODEL)
    c_in, c_out, c_shapes = _cast_rider(cast, layer, n_steps, lambda i: i)
    in_specs = [pl.BlockSpec((tm, D_MODEL), lambda i: (i, 0)), const(gain), const(scale0)]
    in_specs += [const(w) for w in weights] + c_in
    out_specs = [pl.BlockSpec((tm, w.shape[1]), lambda i: (i, 0)) for w in weights] + c_out
    out_shape = [jax.ShapeDtypeStruct((t, w.shape[1]), dt) for w, dt in zip(weights, out_dtypes)]
    out_shape += c_shapes
    if mean_of is not None:
        per_step, cols = tm // mean_rows, weights[mean_of].shape[1]
        out_specs.append(pl.BlockSpec((1, per_step, cols), lambda i: (i, 0, 0)))
        out_shape.append(jax.ShapeDtypeStruct((n_steps, per_step, cols), F32))
    outs = pl.pallas_call(
        functools.partial(_proj_kernel, n_out=n_out, n_cast=len(cast), chunk=512,
                          mean_of=mean_of, mean_rows=mean_rows),
        grid=(n_steps,),
        in_specs=in_specs, out_specs=out_specs, out_shape=out_shape,
        compiler_params=_cparams(("arbitrary",)),
        name="proj",
    )(h, gain, scale0, *weights, *cast)
    n_c = len(cast)
    means = outs[n_out + n_c].reshape(t // mean_rows, -1) if mean_of is not None else None
    return outs[:n_out], outs[n_out:n_out + n_c], means


def _q_scale_row(n_q, n_total):
    q = jnp.full((1, n_q), (HEAD_DIM ** -0.5) * LOG2E, F32)
    return jnp.concatenate([q, jnp.ones((1, n_total - n_q), F32)], axis=1)


def _ffn_kernel(h_ref, o_ref, wo_ref, g_ref, wg_ref, wu_ref, wd_ref, gf_ref, out_ref,
                *, final_norm, chunks):
    h1 = h_ref[...] + _dot(o_ref[...], wo_ref[...])
    a = _rms(h1, g_ref[...]).astype(BF16)
    out_ref[...] = h1
    for c0, c1 in chunks:
        gate = _dot(a, wg_ref[:, c0:c1])
        up = _dot(a, wu_ref[:, c0:c1])
        t = (jax.nn.silu(gate) * up).astype(BF16)
        out_ref[...] += _dot(t, wd_ref[c0:c1, :])
    if final_norm:
        out_ref[...] = _rms(out_ref[...], gf_ref[...])


def _ffn(h, o, wo, gain, wg, wu, wd, final_gain, final_norm):
    t = h.shape[0]
    tm = min(ROW_TILE, t)
    d_ff = wg.shape[1]
    step = 3 * MXU_TILE
    chunks = tuple((c0, min(c0 + step, d_ff)) for c0 in range(0, d_ff, step))
    assert d_ff % MXU_TILE == 0
    gain = gain.reshape(1, D_MODEL)
    final_gain = final_gain.reshape(1, D_MODEL)
    const = lambda a: pl.BlockSpec(a.shape, lambda i: (0, 0), pipeline_mode=pl.Buffered(1))
    return pl.pallas_call(
        functools.partial(_ffn_kernel, final_norm=final_norm, chunks=chunks),
        grid=(t // tm,),
        in_specs=[
            pl.BlockSpec((tm, D_MODEL), lambda i: (i, 0)),
            pl.BlockSpec((tm, o.shape[1]), lambda i: (i, 0)),
            const(wo), const(gain), const(wg), const(wu), const(wd), const(final_gain),
        ],
        out_specs=pl.BlockSpec((tm, D_MODEL), lambda i: (i, 0)),
        out_shape=jax.ShapeDtypeStruct((t, D_MODEL), F32),
        compiler_params=_cparams(("parallel",)),
        name="ffn",
    )(h, o, wo, gain, wg, wu, wd, final_gain)


def _block_max(s, rows):
    if rows is None:
        return jnp.max(s, axis=0, keepdims=True)
    h = s.shape[0] // len(rows)
    ms = [jnp.max(s[a * h:(a + 1) * h].reshape(h // SUBLANES, SUBLANES, s.shape[1]), axis=0) + row
          for a, row in enumerate(rows)]
    return jnp.max(functools.reduce(jnp.maximum, ms), axis=0, keepdims=True)


def _block_exp2(s, rows, m):
    if rows is None:
        return jnp.exp2(s - m)
    h = s.shape[0] // len(rows)
    return jnp.concatenate([jnp.exp2(s[a * h:(a + 1) * h] - (m - row))
                            for a, row in enumerate(rows)], axis=0)


def _flash_step_multi(kvs, qps, rows, m_scs, acc_scs):
    n = len(kvs)
    ss = [_dot(kvs[c], qps[c]) for c in range(n)]
    m_olds = [m_scs[c][...] for c in range(n)]
    acc_olds = [acc_scs[c][...] for c in range(n)]
    m_news = [jnp.maximum(m_olds[c], _block_max(ss[c], rows[c])) for c in range(n)]
    alphas = [jnp.exp2(m_olds[c] - m_news[c]) for c in range(n)]
    ps = [_block_exp2(ss[c], rows[c], m_news[c]) for c in range(n)]
    pvs = [_pv(kvs[c], ps[c]) for c in range(n)]
    acc_news = [alphas[c] * acc_olds[c] + pvs[c] for c in range(n)]
    for c in range(n):
        m_scs[c][...] = m_news[c]
        acc_scs[c][...] = acc_news[c]


def _pv(kv, p):
    tk = kv.shape[0]
    lhs = jnp.concatenate([jnp.ones((ONES_ROWS, tk), BF16), kv.T[HEAD_DIM:2 * HEAD_DIM]], axis=0)
    return _dot(lhs, p.astype(BF16))


def _flash_out(acc):
    l = acc[0:1, :]
    return acc[ONES_ROWS:ACC_ROWS, :] / jnp.where(l > 0, l, 1.0)


def _iota(shape, dim):
    return lax.broadcasted_iota(jnp.int32, shape, dim)


def _rank_select(adj, n_rows, topk):
    j = _iota(adj.shape, 0)
    rank = jnp.zeros(adj.shape, jnp.int32)
    for k in range(n_rows):
        rk = adj[k:k + 1, :]
        beats = (rk > adj) | ((rk == adj) & (j > k))
        rank = rank + beats.astype(jnp.int32)
    return rank < topk


def _nsa_cmp_kernel(*refs):
    n_slab = 2 * NSA_KV // LANES
    x_refs = refs[:n_slab]
    w1k_ref, w2k_ref, pk_ref, w1v_ref, w2v_ref, pv_ref, o_ref = refs[n_slab:]
    per_slab = LANES // HEAD_DIM
    nb, _, n_seg, _ = o_ref.shape
    seq = n_seg * CMP_STRIDE
    for t, x_ref in enumerate(x_refs):
        is_v = t >= n_slab // 2
        w1_ref, w2_ref, p_ref = (w1v_ref, w2v_ref, pv_ref) if is_v else (w1k_ref, w2k_ref, pk_ref)
        lane0 = HEAD_DIM if is_v else 0
        tok = [jnp.concatenate([x_ref[pl.ds(bb * seq + l, n_seg, stride=CMP_STRIDE), :]
                                for bb in range(nb)], axis=0) for l in range(CMP_STRIDE)]
        pp = _dot(p_ref[...], w1_ref[...])
        posb = pp[0:1, :CMP_HIDDEN] + pp[1:2, CMP_HIDDEN:]
        for e in range(per_slab):
            g = (t % (n_slab // 2)) * per_slab + e
            x = jnp.concatenate([tk_[:, e * HEAD_DIM:(e + 1) * HEAD_DIM] for tk_ in tok],
                                axis=1).astype(BF16)
            acc = _dot(x, w1_ref[...])
            hid = acc[:, :CMP_HIDDEN] + pltpu.roll(acc[:, CMP_HIDDEN:], nb * n_seg - 1, 0) + posb
            y = _dot(jax.nn.gelu(hid).astype(BF16), w2_ref[...])
            for bb in range(nb):
                o_ref[bb, g, :, lane0:lane0 + HEAD_DIM] = (
                    y[bb * n_seg:(bb + 1) * n_seg].astype(o_ref.dtype))


def _nsa_cmp(cmp_in, batch, seq, w1k, w2k, pk, w1v, w2v, pv):
    full = lambda a: pl.BlockSpec(a.shape, lambda i: (0,) * a.ndim)
    n_seg = seq // CMP_STRIDE
    n_slab = 2 * NSA_KV // LANES
    nb = max(d for d in (1, 2, 4) if batch % d == 0)
    slabs = [pl.BlockSpec((nb * seq, LANES), functools.partial(lambda i, t: (i, t), t=t))
             for t in range(n_slab)]
    return pl.pallas_call(
        _nsa_cmp_kernel,
        grid=(batch // nb,),
        in_specs=slabs + [full(w1k), full(w2k), full(pk), full(w1v), full(w2v), full(pv)],
        out_specs=pl.BlockSpec((nb, NSA_GROUPS, n_seg, 2 * HEAD_DIM), lambda i: (i, 0, 0, 0)),
        out_shape=jax.ShapeDtypeStruct((batch, NSA_GROUPS, n_seg, 2 * HEAD_DIM), BF16),
        compiler_params=_cparams(("parallel",)),
        name="nsa_cmp",
    )(*([cmp_in] * n_slab), w1k, w2k, pk, w1v, w2v, pv)


def _nsa_attn_kernel(q_ref, kvs_ref, kvw_ref, kvc_ref, gt_ref, o_ref,
                     m_sc, acc_sc, selb_sc, out_sc, *, seq, ng):
    tq, tk = ATT_TQ, ATT_TK
    i = pl.program_id(2)
    q0 = i * tq
    n_slc = seq // SLC_BLOCK
    n_cmp_pad = seq // CMP_STRIDE
    rep = NSA_REP
    sub = tk // SLC_BLOCK
    parts = tq // LANES
    win_rows = WINDOW + LANES
    groups = range(ng)
    lanes = [slice(g * LANES, (g + 1) * LANES) for g in groups]

    def heads(x):
        return jnp.concatenate([x] * rep, axis=1)

    def part_lanes(x, part):
        return jnp.concatenate([x[:, (r * parts + part) * LANES:(r * parts + part + 1) * LANES]
                                for r in range(rep)], axis=1)

    def head_major(xs):
        return jnp.concatenate([xs[p][:, r * LANES:(r + 1) * LANES]
                                for r in range(rep) for p in range(parts)], axis=1)

    qt = q_ref[...].astype(F32).T
    zpad = jnp.zeros((HEAD_DIM, tq), F32)
    qps = [jnp.concatenate(
        [jnp.concatenate([qt[(g * rep + r) * HEAD_DIM:(g * rep + r + 1) * HEAD_DIM], zpad], axis=0)
         for r in range(rep)], axis=1).astype(BF16) for g in groups]
    gates = jax.nn.sigmoid(gt_ref[...]).T

    def emit(g, branch, o_t, first):
        for r in range(rep):
            row = branch * NSA_HEADS + g * rep + r
            val = gates[row:row + 1, :] * o_t[:, r * tq:(r + 1) * tq]
            sl = slice((g * rep + r) * HEAD_DIM, (g * rep + r + 1) * HEAD_DIM)
            if first:
                out_sc[sl, :] = val
            else:
                out_sc[sl, :] += val

    d0 = pl.multiple_of(q0, tk)
    kvcs = [kvc_ref[0, g] for g in groups]
    kvds = [kvs_ref[pl.ds(d0, tk), lanes[g]] for g in groups]
    s_cs = [_dot(kvcs[g], qps[g]) for g in groups]
    kvws, s_ws, w_biases = [], [], []
    for part in range(parts):
        h0 = q0 + part * LANES
        w0 = pl.multiple_of(jnp.maximum(h0 - WINDOW, 0), LANES)
        k_pos = w0 + _iota((win_rows, LANES), 0)
        t_pos = h0 + _iota((win_rows, LANES), 1)
        ok = (k_pos <= t_pos) & (k_pos > t_pos - WINDOW)
        w_biases.append(jnp.concatenate([jnp.where(ok, 0.0, MASKED)] * rep, axis=1))
        kvws.append([kvw_ref[pl.ds(w0, win_rows), lanes[g]] for g in groups])
        s_ws.append([_dot(kvws[part][g], part_lanes(qps[g], part)) for g in groups])

    c_end = _iota((n_cmp_pad, tq), 0) * CMP_STRIDE + (CMP_BLOCK - 1)
    c_ok = (c_end <= q0 + _iota((n_cmp_pad, tq), 1)) & (c_end < seq)
    c_bias = heads(jnp.where(c_ok, 0.0, MASKED))

    jj = _iota((n_slc, n_cmp_pad), 0) * SLC_BLOCK
    cc = _iota((n_slc, n_cmp_pad), 1) * CMP_STRIDE
    overlap_t = jnp.where((cc < jj + SLC_BLOCK) & (cc + CMP_BLOCK > jj) & (cc + CMP_BLOCK <= seq),
                          1.0, 0.0).astype(BF16)
    j_blk = _iota((n_slc, tq), 0)
    cur = (q0 + _iota((n_slc, tq), 1)) // SLC_BLOCK
    forced = (j_blk == 0) | (j_blk == cur) | (j_blk == cur - 1)
    valid = j_blk <= cur

    ss = [s_cs[g] + c_bias for g in groups]
    ms = [jnp.maximum(jnp.max(s, axis=0, keepdims=True), M_FLOOR) for s in ss]
    es = [jnp.exp2(ss[g] - ms[g]) for g in groups]
    ds = [jnp.sum(e, axis=0, keepdims=True) for e in es]
    p_cs = [es[g] / jnp.where(ds[g] > 0, ds[g], 1.0) for g in groups]
    o_cs = [_pv(kvcs[g], p_cs[g])[ONES_ROWS:ACC_ROWS, :] for g in groups]

    o_parts = []
    for part in range(parts):
        ss = [s_ws[part][g] + w_biases[part] for g in groups]
        ms = [jnp.max(s, axis=0, keepdims=True) for s in ss]
        o_parts.append([_flash_out(_pv(kvws[part][g], jnp.exp2(ss[g] - ms[g]))) for g in groups])
    o_ws = [head_major([o_parts[p][g] for p in range(parts)]) for g in groups]

    for g in groups:
        emit(g, 0, o_cs[g], True)
        emit(g, 2, o_ws[g], False)
        selb_sc[g] = jnp.where(valid, 0.0, MASKED)

    @pl.when((i + 1) * (tq // SLC_BLOCK) > SLC_TOPK)
    def _():
        selbs = []
        for g in groups:
            p_sum = p_cs[g][:, 0:tq]
            for r in range(1, rep):
                p_sum = p_sum + p_cs[g][:, r * tq:(r + 1) * tq]
            p_hi = p_sum.astype(BF16)
            p_lo = (p_sum - p_hi.astype(F32)).astype(BF16)
            imp = _dot(overlap_t, p_hi) + _dot(overlap_t, p_lo)
            adj = jnp.where(forced, jnp.inf, jnp.where(valid, imp, -jnp.inf))
            sel = valid & _rank_select(adj, n_slc, min(SLC_TOPK, n_slc))
            selbs.append(jnp.where(sel, 0.0, MASKED))
        for g in groups:
            selb_sc[g] = selbs[g]

    def sel_rows(g, blk0):
        return [heads(selb_sc[g, pl.ds(blk0 + a, 1), :]) for a in range(sub)]

    m_scs = [m_sc.at[g] for g in groups]
    acc_scs = [acc_sc.at[g] for g in groups]

    d_rows = [sel_rows(g, i * sub) for g in groups]
    blk_per_part = LANES // SLC_BLOCK
    d_kvs, d_ss, d_rws = [], [], []
    for part in range(parts):
        nk = (part + 1) * LANES
        ok = _iota((nk, LANES), 0) <= part * LANES + _iota((nk, LANES), 1)
        bias = jnp.concatenate([jnp.where(ok, 0.0, MASKED)] * rep, axis=1)
        d_kvs.append([kvds[g][:nk] for g in groups])
        d_ss.append([_dot(d_kvs[part][g], part_lanes(qps[g], part)) + bias for g in groups])
        d_rws.append([[part_lanes(row, part) for row in d_rows[g][:(part + 1) * blk_per_part]]
                      for g in groups])
    d_ms = [[jnp.maximum(_block_max(d_ss[p][g], d_rws[p][g]), M_FLOOR) for g in groups]
            for p in range(parts)]
    d_ps = [[_block_exp2(d_ss[p][g], d_rws[p][g], d_ms[p][g]) for g in groups] for p in range(parts)]
    d_accs = [[_pv(d_kvs[p][g], d_ps[p][g]) for g in groups] for p in range(parts)]

    for g in groups:
        m_scs[g][...] = head_major([d_ms[p][g] for p in range(parts)])
        acc_scs[g][...] = head_major([d_accs[p][g] for p in range(parts)])

    def slc_body(kt, carry):
        k0 = pl.multiple_of(kt * tk, tk)
        _flash_step_multi([kvs_ref[pl.ds(k0, tk), lanes[g]] for g in groups], qps,
                          [sel_rows(g, kt * sub) for g in groups], m_scs, acc_scs)
        return carry

    lax.fori_loop(0, i, slc_body, 0)
    for g in groups:
        emit(g, 1, _flash_out(acc_scs[g][...]), False)

    o_ref[...] = out_sc[...].T.astype(o_ref.dtype)


def _nsa_attn(main, kvc, gates, batch, seq):
    tq = ATT_TQ
    ng = NSA_GROUPS
    assert ATT_TK == tq and WINDOW % tq == 0 and seq % tq == 0 and seq >= WINDOW + tq
    assert tq % SLC_BLOCK == 0
    nq = seq // tq
    t = batch * seq
    gw = ng * NSA_REP * HEAD_DIM
    kvw = ng * LANES
    assert (NSA_HEADS * HEAD_DIM) % kvw == 0
    kv0 = NSA_HEADS * HEAD_DIM // kvw
    m = NSA_REP * tq
    n_cmp_pad = seq // CMP_STRIDE
    return pl.pallas_call(
        functools.partial(_nsa_attn_kernel, seq=seq, ng=ng),
        grid=(batch, NSA_GROUPS // ng, nq),
        in_specs=[
            pl.BlockSpec((tq, gw), lambda b, g, i: (b * nq + i, g)),
            pl.BlockSpec((seq, kvw), lambda b, g, i: (b, kv0 + g)),
            pl.BlockSpec((seq, kvw), lambda b, g, i: (b, kv0 + NSA_GROUPS // ng + g)),
            pl.BlockSpec((1, ng, n_cmp_pad, LANES), lambda b, g, i: (b, g, 0, 0)),
            pl.BlockSpec((tq, LANES), lambda b, g, i: (b * nq + i, 0)),
        ],
        out_specs=pl.BlockSpec((tq, gw), lambda b, g, i: (b * nq + i, g)),
        out_shape=jax.ShapeDtypeStruct((t, NSA_HEADS * HEAD_DIM), BF16),
        scratch_shapes=[pltpu.VMEM((ng, 1, m), F32),
                        pltpu.VMEM((ng, ACC_ROWS, m), F32),
                        pltpu.VMEM((ng, seq // SLC_BLOCK, tq), F32),
                        pltpu.VMEM((gw, tq), F32)],
        compiler_params=_cparams(("parallel", "parallel", "arbitrary")),
        name="nsa_attn",
    )(main, main, main, kvc, gates)


def _interleave_heads(wa, wb, n):
    d = wa.shape[0]
    return jnp.stack([wa.reshape(d, n, HEAD_DIM), wb.reshape(d, n, HEAD_DIM)],
                     axis=2).reshape(d, 2 * n * HEAD_DIM)


def _nsa_mixer(h, gain, w_in, pos_k, pos_v, k_w1, k_w2, v_w1, v_w2, batch, seq, cast, layer):
    nh = NSA_HEADS * HEAD_DIM
    c = [nh + i * NSA_KV for i in range(7)]
    wq = w_in[:, :c[0]]
    wkc, wvc, wks, wvs, wkw, wvw = (w_in[:, c[i]:c[i + 1]] for i in range(6))
    w_gate = w_in[:, c[6]:]
    w_main = jnp.concatenate(
        [wq, _interleave_heads(wks, wvs, NSA_GROUPS), _interleave_heads(wkw, wvw, NSA_GROUPS)],
        axis=1).astype(BF16)
    w_cmp = jnp.concatenate([wkc, wvc], axis=1).astype(BF16)
    wg = jnp.pad(w_gate, ((0, 0), (0, LANES - 3 * NSA_HEADS))).astype(BF16)

    (main, gates, cmp_in), casted, _ = _proj(h, gain, _q_scale_row(nh, w_main.shape[1]),
                                             [w_main, wg, w_cmp], [BF16, F32, F32], cast, layer)

    half = CMP_STRIDE * HEAD_DIM

    def seg_weights(w1):
        return jnp.concatenate([w1[:half], w1[half:]], axis=1).astype(BF16)

    def pos_rows(p):
        rows = p.reshape(2, half)
        return jnp.pad(rows, ((0, SUBLANES - 2), (0, 0))).astype(BF16)

    kvc = _nsa_cmp(cmp_in, batch, seq, seg_weights(k_w1), k_w2.astype(BF16), pos_rows(pos_k),
                   seg_weights(v_w1), v_w2.astype(BF16), pos_rows(pos_v))
    return _nsa_attn(main, kvc, gates, batch, seq), casted


def _moba_attn_kernel(q_ref, kv_ref, kmean_ref, o_ref, m_sc, acc_sc, selb_sc, *, seq, nh):
    tq, tk = ATT_TQ, ATT_TK
    i = pl.program_id(2)
    q0 = i * tq
    n_blk = seq // MOBA_BLOCK
    heads = range(nh)
    lanes = [slice(h * LANES, (h + 1) * LANES) for h in heads]

    qt = q_ref[...].astype(F32).T
    zpad = jnp.zeros((HEAD_DIM, tq), F32)
    qps = [jnp.concatenate([qt[h * HEAD_DIM:(h + 1) * HEAD_DIM], zpad], axis=0).astype(BF16)
           for h in heads]
    d0 = pl.multiple_of(q0, tk)
    kvds = [kv_ref[pl.ds(d0, tk), lanes[h]] for h in heads]
    past = _iota((n_blk, tq), 0) < i
    for h in heads:
        score = _dot(kmean_ref[0, :, lanes[h]].astype(BF16), qps[h])
        adj = jnp.where(past, score, -jnp.inf)
        sel = past & _rank_select(adj, n_blk, MOBA_TOPK)
        selb_sc[h] = jnp.where(sel, 0.0, MASKED)

    m_scs = [m_sc.at[h] for h in heads]
    acc_scs = [acc_sc.at[h] for h in heads]

    parts = tq // LANES
    d_ss, d_kvs = [], []
    for part in range(parts):
        nk = (part + 1) * LANES
        ok = _iota((nk, LANES), 0) <= part * LANES + _iota((nk, LANES), 1)
        bias = jnp.where(ok, 0.0, MASKED)
        d_kvs.append([kvds[h][:nk] for h in heads])
        d_ss.append([_dot(d_kvs[part][h], qps[h][:, part * LANES:(part + 1) * LANES]) + bias
                     for h in heads])
    d_ms = [[jnp.maximum(jnp.max(d_ss[p][h], axis=0, keepdims=True), M_FLOOR) for h in heads]
            for p in range(parts)]
    d_ps = [[jnp.exp2(d_ss[p][h] - d_ms[p][h]) for h in heads] for p in range(parts)]
    d_accs = [[_pv(d_kvs[p][h], d_ps[p][h]) for h in heads] for p in range(parts)]
    for h in heads:
        m_scs[h][...] = jnp.concatenate([d_ms[p][h] for p in range(parts)], axis=1)
        acc_scs[h][...] = jnp.concatenate([d_accs[p][h] for p in range(parts)], axis=1)

    def body(kt, carry):
        k0 = pl.multiple_of(kt * tk, tk)
        _flash_step_multi([kv_ref[pl.ds(k0, tk), lanes[h]] for h in heads], qps,
                          [[selb_sc[h, pl.ds(kt, 1), :]] for h in heads], m_scs, acc_scs)
        return carry

    lax.fori_loop(0, i, body, 0)
    outs =[_flash_out(acc_scs[h][...]) for h in heads]
    o_ref[...] = jnp.concatenate(outs, axis=0).T.astype(o_ref.dtype)


def _moba_attn(q, kv, kv_mean, batch, seq):
    tq = ATT_TQ
    nh = MOBA_HEADS
    assert ATT_TK == tq and MOBA_BLOCK == tq and seq % tq == 0
    nq = seq // tq
    t = batch * seq
    d_heads = MOBA_HEADS * HEAD_DIM
    n_blk = seq // MOBA_BLOCK
    qw = nh * HEAD_DIM
    kvw = nh * LANES
    return pl.pallas_call(
        functools.partial(_moba_attn_kernel, seq=seq, nh=nh),
        grid=(batch, MOBA_HEADS // nh, nq),
        in_specs=[
            pl.BlockSpec((tq, qw), lambda b, hg, i: (b * nq + i, hg)),
            pl.BlockSpec((seq, kvw), lambda b, hg, i: (b, hg)),
            pl.BlockSpec((1, n_blk, kvw), lambda b, hg, i: (b, 0, hg)),
        ],
        out_specs=pl.BlockSpec((tq, qw), lambda b, hg, i: (b * nq + i, hg)),
        out_shape=jax.ShapeDtypeStruct((t, d_heads), BF16),
        scratch_shapes=[pltpu.VMEM((nh, 1, tq), F32),
                        pltpu.VMEM((nh, ACC_ROWS, tq), F32),
                        pltpu.VMEM((nh, n_blk, tq), F32)],
        compiler_params=_cparams(("parallel", "parallel", "arbitrary")),
        name="moba_attn",
    )(q, kv, kv_mean.reshape(batch, n_blk, -1))


def _moba_mixer(h, gain, w_in, batch, seq, cast, layer):
    nh = MOBA_HEADS * HEAD_DIM
    wq = w_in[:, :nh]
    wk = w_in[:, nh:2 * nh]
    wv = w_in[:, 2 * nh:]
    w_kv = _interleave_heads(wk, wv, MOBA_HEADS).astype(BF16)
    (q, kv), casted, kv_mean = _proj(h, gain, _q_scale_row(nh, nh), [wq.astype(BF16), w_kv],
                                     [BF16, BF16], cast, layer, mean_of=1, mean_rows=MOBA_BLOCK)
    return _moba_attn(q, kv, kv_mean, batch, seq), casted


def _conv_kernel(*refs, n_cast):
    h_ref, g_ref, win_ref, cw_ref = refs[:4]
    c_refs = refs[4:4 + n_cast]
    o_ref = refs[4 + n_cast]
    ztail_sc = refs[-1]
    _cast_blocks(c_refs, refs[5 + n_cast:5 + 2 * n_cast])
    i = pl.program_id(1)
    d = D_MODEL
    a = _rms(h_ref[...], g_ref[...]).astype(BF16)
    ts = a.shape[0]

    @pl.when(i == 0)
    def _():
        ztail_sc[...] = jnp.zeros_like(ztail_sc)

    row = _iota((ts, MXU_TILE), 0)
    for c0 in range(0, d, MXU_TILE):
        cols = slice(c0, c0 + MXU_TILE)
        b_gate = _dot(a, win_ref[:, c0:c0 + MXU_TILE])
        z = (_dot(a, win_ref[:, d + c0:d + c0 + MXU_TILE])
             * _dot(a, win_ref[:, 2 * d + c0:2 * d + c0 + MXU_TILE]))
        zp = ztail_sc[:, cols]
        ztail_sc[:, cols] = z[ts - SUBLANES:ts]
        z1 = jnp.where(row == 0, zp[7:8], pltpu.roll(z, 1, 0))
        z2 = jnp.where(row == 0, zp[6:7], jnp.where(row == 1, zp[7:8], pltpu.roll(z, 2, 0)))
        y = cw_ref[0:1, cols] * z2 + cw_ref[1:2, cols] * z1 + cw_ref[2:3, cols] * z
        o_ref[:, cols] = (b_gate * y).astype(o_ref.dtype)


def _conv_mixer(h, gain, w_in, conv_w, batch, seq, cast, layer):
    t = batch * seq
    ts = min(ROW_TILE, seq)
    ns = seq // ts
    assert CONV_WIDTH - 1 <= SUBLANES
    cw = jnp.pad(conv_w, ((0, SUBLANES - CONV_WIDTH), (0, 0)))
    c_in, c_out, c_shapes = _cast_rider(cast, layer, batch * ns, lambda b, i: b * ns + i)
    outs = pl.pallas_call(
        functools.partial(_conv_kernel, n_cast=len(cast)),
        grid=(batch, ns),
        in_specs=[
            pl.BlockSpec((ts, D_MODEL), lambda b, i: (b * ns + i, 0)),
            pl.BlockSpec((1, D_MODEL), lambda b, i: (0, 0)),
            pl.BlockSpec(w_in.shape, lambda b, i: (0, 0)),
            pl.BlockSpec(cw.shape, lambda b, i: (0, 0)),
        ] + c_in,
        out_specs=[pl.BlockSpec((ts, D_MODEL), lambda b, i: (b * ns + i, 0))] + c_out,
        out_shape=[jax.ShapeDtypeStruct((t, D_MODEL), BF16)] + c_shapes,
        scratch_shapes=[pltpu.VMEM((SUBLANES, D_MODEL), F32)],
        compiler_params=_cparams(("arbitrary", "arbitrary")),
        name="conv_mix",
    )(h, gain.reshape(1, D_MODEL), w_in.astype(BF16), cw, *cast)
    return outs[0], outs[1:]


def kernel(x, norm_mix, norm_ffn, norm_final, ffn_w_gate, ffn_w_up, ffn_w_down, nsa_w_in, nsa_w_out, nsa_cmp_pos_k, nsa_cmp_pos_v, nsa_cmp_k_w1, nsa_cmp_k_w2, nsa_cmp_v_w1, nsa_cmp_v_w2, moba_w_in, moba_w_out, conv_w_in, conv_w, conv_w_out):
    batch, seq, d = x.shape
    depth = norm_mix.shape[0]
    h = x.reshape(batch * seq, d)
    ffn_f32 = [ffn_w_gate, ffn_w_up, ffn_w_down]
    for i in range(depth):
        kind, j = i % N_MIXERS, i // N_MIXERS
        if kind == 0:
            o, ffn_w = _nsa_mixer(h, norm_mix[i], nsa_w_in[j], nsa_cmp_pos_k[j], nsa_cmp_pos_v[j],
                                  nsa_cmp_k_w1[j], nsa_cmp_k_w2[j], nsa_cmp_v_w1[j], nsa_cmp_v_w2[j],
                                  batch, seq, ffn_f32, i)
            wo = nsa_w_out[j]
        elif kind == 1:
            o, ffn_w = _moba_mixer(h, norm_mix[i], moba_w_in[j], batch, seq, ffn_f32, i)
            wo = moba_w_out[j]
        else:
            o, ffn_w = _conv_mixer(h, norm_mix[i], conv_w_in[j], conv_w[j], batch, seq, ffn_f32, i)
            wo = conv_w_out[j]
        wg, wu, wd = ffn_w
        h = _ffn(h, o, wo.astype(BF16), norm_ffn[i], wg, wu, wd, norm_final,
                 final_norm=(i == depth - 1))
    return h.reshape(batch, seq, d)
```
